```python
import math
import jax, jax.numpy as jnp
from jax import lax
import numpy as np

D_MODEL = 1024
BATCH = 4
SEQ = 8192
DEPTH = 1
DEC_BATCH = 16
DEC_SEQ = 64
PAST_LEN = 2048

CHUNK = 64
Q_BLOCK = 128
MIX_WIDTH = D_MODEL
ATTN_WIDTH = MIX_WIDTH // 2
CONV_WIDTH = MIX_WIDTH - ATTN_WIDTH
N_HEADS = 4
HEAD_V = ATTN_WIDTH // N_HEADS
HEAD_QK = HEAD_V // 2
CONV_K = 3
D_FF = 4 * D_MODEL
N_BUCKETS = 32
MAX_DISTANCE = 128
LN_EPS = 1e-5
SUBLN_EPS = 1e-5
DN_ALPHA = (2.0 * DEPTH) ** 0.25
DN_BETA = (8.0 * DEPTH) ** -0.25
IN_WIDTH = 3 * ATTN_WIDTH + 3 * CONV_WIDTH

kernel_name = "hybrid_diffattn_shortconv_stream_step"


def _lambda_init(layer):
    return 0.8 - 0.6 * math.exp(-0.3 * layer)


def layer_norm(x, g, b):
    xf = x.astype(jnp.float32)
    mu = jnp.mean(xf, axis=-1, keepdims=True)
    var = jnp.mean(jnp.square(xf - mu), axis=-1, keepdims=True)
    return ((xf - mu) * lax.rsqrt(var + LN_EPS) * g + b).astype(x.dtype)


def rms_norm(x, g):
    xf = x.astype(jnp.float32)
    return xf * lax.rsqrt(jnp.mean(jnp.square(xf), axis=-1, keepdims=True) + SUBLN_EPS) * g


def rel_bucket(rel):
    half = N_BUCKETS // 2
    max_exact = half // 2
    ret = jnp.where(rel > 0, half, 0)
    n = jnp.abs(rel)
    nf = jnp.maximum(n, 1).astype(jnp.float32)
    large = max_exact + (jnp.log(nf / max_exact) / math.log(MAX_DISTANCE / max_exact)
                         * (half - max_exact)).astype(jnp.int32)
    large = jnp.minimum(large, half - 1)
    return ret + jnp.where(n < max_exact, n, large)


def diff_attention(q, k, v, q_pos, k_pos, rel_bias, lam):
    scale = HEAD_QK ** -0.5
    bias = jnp.transpose(rel_bias[rel_bucket(k_pos[None, :] - q_pos[:, None])], (2, 0, 1)).astype(jnp.float32)
    visible = (k_pos[None, :] // CHUNK) <= (q_pos[:, None] // CHUNK)

    def probs(qi, ki):
        s = jnp.einsum('bqhd,bkhd->bhqk', qi, ki, preferred_element_type=jnp.float32) * scale + bias
        return jax.nn.softmax(jnp.where(visible, s, -jnp.inf), axis=-1)

    p = probs(q[..., :HEAD_QK], k[..., :HEAD_QK]) - lam * probs(q[..., HEAD_QK:], k[..., HEAD_QK:])
    return jnp.einsum('bhqk,bkhd->bqhd', p.astype(v.dtype), v, preferred_element_type=jnp.float32)


def prompt_attention(q, k, v, rel_bias, lam):
    b, s = q.shape[0], q.shape[1]
    nb = s // Q_BLOCK
    qb = jnp.moveaxis(q.reshape(b, nb, Q_BLOCK, N_HEADS, HEAD_V), 1, 0)
    k_pos = jnp.arange(s, dtype=jnp.int32)

    def one_block(args):
        q_blk, i = args
        q_pos = i * Q_BLOCK + jnp.arange(Q_BLOCK, dtype=jnp.int32)
        return diff_attention(q_blk, k, v, q_pos, k_pos, rel_bias, lam)

    out = lax.map(one_block, (qb, jnp.arange(nb, dtype=jnp.int32)))
    return jnp.moveaxis(out, 0, 1).reshape(b, s, N_HEADS, HEAD_V)


def project_in(x, w_in):
    z = jnp.einsum('btd,de->bte', x, w_in)
    a, c = ATTN_WIDTH, CONV_WIDTH
    q, k, v, gb, gc, h = jnp.split(z, [a, 2 * a, 3 * a, 3 * a + c, 3 * a + 2 * c], axis=-1)
    bsz, t = x.shape[0], x.shape[1]
    heads = lambda y: y.reshape(bsz, t, N_HEADS, HEAD_V)
    return heads(q), heads(k), heads(v), gb, gc, h


def causal_conv(u_ext, w):
    t = u_ext.shape[1] - (CONV_K - 1)
    return sum(u_ext[:, j:j + t] * w[j] for j in range(CONV_K))


def finish(x, attn, conv, subln_g, lam_init, w_out, ln1_g, ln1_b, w_ff1, w_ff2, ln2_g, ln2_b):
    bsz, t = x.shape[0], x.shape[1]
    a = (rms_norm(attn, subln_g) * (1.0 - lam_init)).reshape(bsz, t, ATTN_WIDTH).astype(x.dtype)
    y = jnp.einsum('bte,ed->btd', jnp.concatenate([a, conv.astype(x.dtype)], axis=-1), w_out)
    x = layer_norm(DN_ALPHA * x + y, ln1_g, ln1_b)
    hdn = jnp.square(jax.nn.relu(jnp.einsum('btd,df->btf', x, w_ff1)))
    return layer_norm(DN_ALPHA * x + jnp.einsum('btf,fd->btd', hdn, w_ff2), ln2_g, ln2_b)


def setup_inputs(seed: int = 0) -> dict:
    key = jax.random.key(seed)
    ks = jax.random.split(key, 32)
    f32 = jnp.float32
    nrm = lambda k, shape, s: jax.random.normal(k, shape, f32) * s
    sd = D_MODEL ** -0.5
    w_in = jnp.concatenate([
        nrm(ks[0], (DEPTH, D_MODEL, ATTN_WIDTH), sd),
        nrm(ks[1], (DEPTH, D_MODEL, ATTN_WIDTH), sd),
        nrm(ks[2], (DEPTH, D_MODEL, ATTN_WIDTH), sd * DN_BETA),
        nrm(ks[3], (DEPTH, D_MODEL, CONV_WIDTH), sd),
        nrm(ks[4], (DEPTH, D_MODEL, CONV_WIDTH), sd),
        nrm(ks[5], (DEPTH, D_MODEL, CONV_WIDTH), sd * DN_BETA),
    ], axis=-1)
    return {
        "x_prompt": nrm(ks[6], (BATCH, SEQ, D_MODEL), 1.0),
        "x_sample": nrm(ks[7], (DEC_BATCH, DEC_SEQ, D_MODEL), 1.0),
        "cache_k": nrm(ks[8], (DEPTH, DEC_BATCH, PAST_LEN, N_HEADS, HEAD_V), 1.0),
        "cache_v": nrm(ks[9], (DEPTH, DEC_BATCH, PAST_LEN, N_HEADS, HEAD_V), DN_BETA),
        "cache_conv": nrm(ks[10], (DEPTH, DEC_BATCH, CONV_K - 1, CONV_WIDTH), DN_BETA),
        "ln0_g": 1.0 + nrm(ks[11], (D_MODEL,), 0.02),
        "ln0_b": nrm(ks[12], (D_MODEL,), 0.02),
        "rel_bias": nrm(ks[13], (N_BUCKETS, N_HEADS), 0.5),
        "w_in": w_in,
        "conv_w": nrm(ks[14], (DEPTH, CONV_K, CONV_WIDTH), CONV_K ** -0.5),
        "lambda_q1": nrm(ks[15], (DEPTH, HEAD_QK), 0.1),
        "lambda_k1": nrm(ks[16], (DEPTH, HEAD_QK), 0.1),
        "lambda_q2": nrm(ks[17], (DEPTH, HEAD_QK), 0.1),
        "lambda_k2": nrm(ks[18], (DEPTH, HEAD_QK), 0.1),
        "subln_g": 1.0 + nrm(ks[19], (DEPTH, HEAD_V), 0.02),
        "w_out": nrm(ks[20], (DEPTH, MIX_WIDTH, D_MODEL), MIX_WIDTH ** -0.5 * DN_BETA),
        "ln1_g": 1.0 + nrm(ks[21], (DEPTH, D_MODEL), 0.02),
        "ln1_b": nrm(ks[22], (DEPTH, D_MODEL), 0.02),
        "w_ff1": nrm(ks[23], (DEPTH, D_MODEL, D_FF), sd * DN_BETA),
        "w_ff2": nrm(ks[24], (DEPTH, D_FF, D_MODEL), D_FF ** -0.5 * DN_BETA),
        "ln2_g": 1.0 + nrm(ks[25], (DEPTH, D_MODEL), 0.02),
        "ln2_b": nrm(ks[26], (DEPTH, D_MODEL), 0.02),
    }


def reference(x_prompt, x_sample, cache_k, cache_v, cache_conv, ln0_g, ln0_b, rel_bias, w_in, conv_w,
              lambda_q1, lambda_k1, lambda_q2, lambda_k2, subln_g, w_out, ln1_g, ln1_b,
              w_ff1, w_ff2, ln2_g, ln2_b):
    f32 = jnp.float32
    xp = layer_norm(x_prompt, ln0_g, ln0_b)
    xs = layer_norm(x_sample, ln0_g, ln0_b)
    past = cache_k.shape[2]
    t_s = x_sample.shape[1]
    q_pos_s = past + jnp.arange(t_s, dtype=jnp.int32)
    k_pos_s = jnp.arange(past + t_s, dtype=jnp.int32)
    kp_l, vp_l, cp_l, ks_l, vs_l, cs_l = [], [], [], [], [], []
    for l in range(DEPTH):
        lam_init = _lambda_init(l)
        lam = (jnp.exp(jnp.sum(lambda_q1[l].astype(f32) * lambda_k1[l].astype(f32)))
               - jnp.exp(jnp.sum(lambda_q2[l].astype(f32) * lambda_k2[l].astype(f32))) + lam_init)
        ffn_args = (subln_g[l], lam_init, w_out[l], ln1_g[l], ln1_b[l], w_ff1[l], w_ff2[l], ln2_g[l], ln2_b[l])

        q, k, v, gb, gc, h = project_in(xp, w_in[l])
        attn = prompt_attention(q, k, v, rel_bias, lam)
        u_ext = jnp.pad(gc * h, ((0, 0), (CONV_K - 1, 0), (0, 0)))
        conv = gb * causal_conv(u_ext, conv_w[l])
        kp_l.append(k)
        vp_l.append(v)
        cp_l.append(u_ext[:, -(CONV_K - 1):])
        xp = finish(xp, attn, conv, *ffn_args)

        q, k, v, gb, gc, h = project_in(xs, w_in[l])
        k_all = jnp.concatenate([cache_k[l].astype(k.dtype), k], axis=1)
        v_all = jnp.concatenate([cache_v[l].astype(v.dtype), v], axis=1)
        attn = diff_attention(q, k_all, v_all, q_pos_s, k_pos_s, rel_bias, lam)
        u_ext = jnp.concatenate([cache_conv[l].astype(h.dtype), gc * h], axis=1)
        conv = gb * causal_conv(u_ext, conv_w[l])
        ks_l.append(k)
        vs_l.append(v)
        cs_l.append(u_ext[:, -(CONV_K - 1):])
        xs = finish(xs, attn, conv, *ffn_args)

    return (xp, xs, jnp.stack(kp_l), jnp.stack(vp_l), jnp.stack(cp_l),
            jnp.stack(ks_l), jnp.stack(vs_l), jnp.stack(cs_l))
```

```python
import functools
import math

import jax
import jax.numpy as jnp
from jax import lax
from jax.experimental import pallas as pl
from jax.experimental.pallas import tpu as pltpu

CHUNK = 64
N_HEADS = 4
HEAD_V = 128
HEAD_QK = HEAD_V // 2
ATTN_WIDTH = N_HEADS * HEAD_V
CONV_K = 3
N_BUCKETS = 32
MAX_DISTANCE = 128
LN_EPS = 1e-5
SUBLN_EPS = 1e-5

V7X_SUBLANES = 8
V7X_VMEM_LIMIT_BYTES = 56 * 1024 * 1024

ROW_TILE = 512
Q_TILE = 256

F32 = jnp.float32
BF16 = jnp.bfloat16


def _lambda_init(layer):
    return 0.8 - 0.6 * math.exp(-0.3 * layer)


def _layer_norm(x, g, b):
    mu = jnp.mean(x, axis=-1, keepdims=True)
    xc = x - mu
    var = jnp.mean(xc * xc, axis=-1, keepdims=True)
    return xc * lax.rsqrt(var + LN_EPS) * g + b


def _resident(shape):
    nd = len(shape)
    return pl.BlockSpec(shape, lambda *_: (0,) * nd, pipeline_mode=pl.Buffered(1))


def _in_proj_kernel(x_ref, g_ref, b_ref, w_ref, cw_ref, hist_ref,
                    q_ref, k_ref, v_ref, kb_ref, vb_ref, conv_ref, utail_ref,
                    carry_ref, *, seg_len, tm):
    a = ATTN_WIDTH
    xn = _layer_norm(x_ref[...], g_ref[...], b_ref[...])
    z = jnp.dot(xn.astype(BF16), w_ref[...], preferred_element_type=F32)
    q_ref[...] = (z[:, :a] * (HEAD_QK ** -0.5)).astype(BF16)
    k = z[:, a:2 * a]
    v = z[:, 2 * a:3 * a]
    k_ref[...] = k
    v_ref[...] = v
    kb_ref[...] = k.astype(BF16)
    vb_ref[...] = v.astype(BF16)
    c = z.shape[1] - 3 * a
    c = c // 3
    gb = z[:, 3 * a:3 * a + c]
    u = z[:, 3 * a + c:3 * a + 2 * c] * z[:, 3 * a + 2 * c:]
    rows = lax.broadcasted_iota(jnp.int32, (tm, 1), 0)
    if seg_len >= tm:
        tiles_per_seg = seg_len // tm
        first = (pl.program_id(0) % tiles_per_seg) == 0
        hist = hist_ref[0]
        prev = carry_ref[...]
        p1 = jnp.where(first, hist[1:2, :], prev[V7X_SUBLANES - 1:, :])
        p0 = jnp.where(first, hist[0:1, :], prev[V7X_SUBLANES - 2:V7X_SUBLANES - 1, :])
        pos = rows
        tail = u[tm - V7X_SUBLANES:, :]
        carry_ref[...] = tail
        utail_ref[0] = tail
    else:
        nseg = tm // seg_len
        hist = hist_ref[...]
        p1 = jnp.broadcast_to(hist[:, 1:2, :], (nseg, seg_len, c)).reshape(tm, c)
        p0 = jnp.broadcast_to(hist[:, 0:1, :], (nseg, seg_len, c)).reshape(tm, c)
        pos = rows & (seg_len - 1)
        utail_ref[...] = u.reshape(nseg, seg_len, c)[:, seg_len - V7X_SUBLANES:, :]
    u1 = jnp.where(pos == 0, p1, pltpu.roll(u, 1, 0))
    u2 = jnp.where(pos == 0, p0, jnp.where(pos == 1, p1, pltpu.roll(u, 2, 0)))
    cw = cw_ref[...]
    conv = gb * (u2 * cw[0:1, :] + u1 * cw[1:2, :] + u * cw[2:3, :])
    conv_ref[...] = conv.astype(BF16)


def _in_proj(x2d, seg_len, ln_g, ln_b, w_in_bf, conv_w, hist):
    n, d = x2d.shape
    e = w_in_bf.shape[1]
    a = ATTN_WIDTH
    c = (e - 3 * a) // 3
    tm = min(ROW_TILE, n)
    assert n % tm == 0 and (seg_len % tm == 0 or tm % seg_len == 0)
    assert seg_len & (seg_len - 1) == 0 and seg_len % V7X_SUBLANES == 0
    nseq = n // seg_len
    if seg_len >= tm:
        tiles_per_seg = seg_len // tm
        hist_spec = pl.BlockSpec((1, CONV_K - 1, c), lambda i: (i // tiles_per_seg, 0, 0))
        tail_spec = pl.BlockSpec((1, V7X_SUBLANES, c), lambda i: (i // tiles_per_seg, 0, 0))
    else:
        nseg = tm // seg_len
        hist_spec = pl.BlockSpec((nseg, CONV_K - 1, c), lambda i: (i, 0, 0))
        tail_spec = pl.BlockSpec((nseg, V7X_SUBLANES, c), lambda i: (i, 0, 0))
    row = lambda w: pl.BlockSpec((tm, w), lambda i: (i, 0))
    out_shape = (
        jax.ShapeDtypeStruct((n, a), BF16),
        jax.ShapeDtypeStruct((n, a), F32),
        jax.ShapeDtypeStruct((n, a), F32),
        jax.ShapeDtypeStruct((n, a), BF16),
        jax.ShapeDtypeStruct((n, a), BF16),
        jax.ShapeDtypeStruct((n, c), BF16),
        jax.ShapeDtypeStruct((nseq, V7X_SUBLANES, c), F32),
    )
    return pl.pallas_call(
        functools.partial(_in_proj_kernel, seg_len=seg_len, tm=tm),
        out_shape=out_shape,
        grid=(n // tm,),
        in_specs=[row(d), _resident((1, d)), _resident((1, d)), _resident((d, e)),
                  _resident((CONV_K, c)), hist_spec],
        out_specs=(row(a), row(a), row(a), row(a), row(a), row(c), tail_spec),
        scratch_shapes=[pltpu.VMEM((V7X_SUBLANES, c), F32)],
        compiler_params=pltpu.CompilerParams(
            dimension_semantics=("arbitrary",), vmem_limit_bytes=V7X_VMEM_LIMIT_BYTES),
        name="in_proj",
    )(x2d, ln_g.reshape(1, d), ln_b.reshape(1, d), w_in_bf, conv_w, hist)


def _stack_maps(q):
    lane = lax.broadcasted_iota(jnp.int32, q.shape, 1)
    zero = jnp.zeros_like(q)
    return jnp.concatenate([jnp.where(lane < HEAD_QK, q, zero), jnp.where(lane < HEAD_QK, zero, q)], axis=0)


def _scores(qz, kblk):
    return lax.dot_general(qz, kblk, (((1,), (1,)), ((), ())), preferred_element_type=F32)


def _lambda(lamv_ref, lam_init):
    lv = lamv_ref[...]
    s1 = jnp.sum(lv[0:1, :] * lv[1:2, :], axis=-1, keepdims=True)
    s2 = jnp.sum(lv[2:3, :] * lv[3:4, :], axis=-1, keepdims=True)
    return jnp.exp(s1) - jnp.exp(s2) + lam_init


def _diff_combine(acc, l, lam, g, lam_init, t):
    o = acc[:t] / l[:t] - lam * (acc[t:] / l[t:])
    ms = jnp.mean(o * o, axis=-1, keepdims=True)
    return o * lax.rsqrt(ms + SUBLN_EPS) * g * (1.0 - lam_init)


def _prompt_attn_kernel(lamv_ref, g_ref, q_ref, k_ref, v_ref, bd_ref, bp_ref, o_ref,
                        m_ref, l_ref, acc_ref, *, tq, lam_init):
    qt = pl.program_id(2)
    qz = _stack_maps(q_ref[0])

    def block(j):
        start = pl.multiple_of(j * tq, tq)
        return k_ref[0, pl.ds(start, tq), :], v_ref[0, pl.ds(start, tq), :]

    def accumulate(s, vblk):
        m_prev = m_ref[...]
        m_new = jnp.maximum(m_prev, jnp.max(s, axis=-1, keepdims=True))
        alpha = jnp.exp(m_prev - m_new)
        p = jnp.exp(s - m_new)
        l_ref[...] = alpha * l_ref[...] + jnp.sum(p, axis=-1, keepdims=True)
        acc_ref[...] = alpha * acc_ref[...] + jnp.dot(p.astype(BF16), vblk, preferred_element_type=F32)
        m_ref[...] = m_new

    kd, vd = block(qt)
    s = _scores(qz, kd) + bd_ref[0]
    m0 = jnp.max(s, axis=-1, keepdims=True)
    p = jnp.exp(s - m0)
    m_ref[...] = m0
    l_ref[...] = jnp.sum(p, axis=-1, keepdims=True)
    acc_ref[...] = jnp.dot(p.astype(BF16), vd, preferred_element_type=F32)

    @pl.when(qt > 0)
    def _():
        kp, vp = block(qt - 1)
        accumulate(_scores(qz, kp) + bp_ref[0], vp)

    def far(j, carry):
        kj, vj = block(j)
        accumulate(_scores(qz, kj), vj)
        return carry

    lax.fori_loop(0, qt - 1, far, 0)

    lam = _lambda(lamv_ref, lam_init)
    o_ref[0] = _diff_combine(acc_ref[...], l_ref[...], lam, g_ref[...], lam_init, tq).astype(o_ref.dtype)


def _rel_bucket(rel):
    half = N_BUCKETS // 2
    max_exact = half // 2
    ret = jnp.where(rel > 0, half, 0)
    n = jnp.abs(rel)
    nf = jnp.maximum(n, 1).astype(F32)
    large = max_exact + (jnp.log(nf / max_exact) / math.log(MAX_DISTANCE / max_exact)
                         * (half - max_exact)).astype(jnp.int32)
    large = jnp.minimum(large, half - 1)
    return ret + jnp.where(n < max_exact, n, large)


def _bias_tile(rel_bias, q_pos, k_pos):
    rel = k_pos[None, :] - q_pos[:, None]
    bias = jnp.transpose(rel_bias[_rel_bucket(rel)], (2, 0, 1)).astype(F32)
    far = rel_bias[N_BUCKETS // 2 - 1].astype(F32)
    visible = (k_pos[None, :] // CHUNK) <= (q_pos[:, None] // CHUNK)
    tile = jnp.where(visible[None], bias - far[:, None, None], -jnp.inf)
    return jnp.concatenate([tile, tile], axis=1)


def _prompt_attention(q, kb, vb, rel_bias, lamv, subln_g, lam_init):
    b, t, _ = q.shape
    tq = min(Q_TILE, t)
    assert t % tq == 0 and tq % CHUNK == 0 and tq >= MAX_DISTANCE
    pos = jnp.arange(tq, dtype=jnp.int32)
    bias_diag = _bias_tile(rel_bias, pos + tq, pos + tq)
    bias_prev = _bias_tile(rel_bias, pos + tq, pos)
    qspec = pl.BlockSpec((1, tq, HEAD_V), lambda bi, h, i: (bi, i, h))
    kvspec = pl.BlockSpec((1, t, HEAD_V), lambda bi, h, i: (bi, 0, h))
    bspec = pl.BlockSpec((1, 2 * tq, tq), lambda bi, h, i: (h, 0, 0))
    return pl.pallas_call(
        functools.partial(_prompt_attn_kernel, tq=tq, lam_init=lam_init),
        out_shape=jax.ShapeDtypeStruct(q.shape, BF16),
        grid=(b, N_HEADS, t // tq),
        in_specs=[_resident(lamv.shape), _resident((1, HEAD_V)), qspec, kvspec, kvspec, bspec, bspec],
        out_specs=qspec,
        scratch_shapes=[pltpu.VMEM((2 * tq, 1), F32), pltpu.VMEM((2 * tq, 1), F32),
                        pltpu.VMEM((2 * tq, HEAD_V), F32)],
        compiler_params=pltpu.CompilerParams(
            dimension_semantics=("arbitrary", "arbitrary", "arbitrary"),
            vmem_limit_bytes=V7X_VMEM_LIMIT_BYTES),
        name="prompt_attention",
    )(lamv, subln_g.reshape(1, HEAD_V), q, kb, vb, bias_diag, bias_prev)


def _sample_attn_kernel(lamv_ref, g_ref, q_ref, ck_ref, cv_ref, kn_ref, vn_ref, bias_ref, o_ref,
                        *, ts, past, near, lam_init):
    qz = _stack_maps(q_ref[0])
    far = past - near
    kf = ck_ref[0, :far, :].astype(BF16)
    kc = ck_ref[0, far:, :].astype(BF16)
    vf = cv_ref[0, :far, :].astype(BF16)
    vc = cv_ref[0, far:, :].astype(BF16)
    bias = bias_ref[0]
    s_far = _scores(qz, kf)
    s_near = _scores(qz, kc) + bias[:, :near]
    s_new = _scores(qz, kn_ref[0]) + bias[:, near:]
    m = jnp.maximum(jnp.max(s_far, axis=-1, keepdims=True),
                    jnp.maximum(jnp.max(s_near, axis=-1, keepdims=True),
                                jnp.max(s_new, axis=-1, keepdims=True)))
    p_far = jnp.exp(s_far - m)
    p_near = jnp.exp(s_near - m)
    p_new = jnp.exp(s_new - m)
    l = (jnp.sum(p_far, axis=-1, keepdims=True) + jnp.sum(p_near, axis=-1, keepdims=True)
         + jnp.sum(p_new, axis=-1, keepdims=True))
    acc = (jnp.dot(p_far.astype(BF16), vf, preferred_element_type=F32)
           + jnp.dot(p_near.astype(BF16), vc, preferred_element_type=F32)
           + jnp.dot(p_new.astype(BF16), vn_ref[0], preferred_element_type=F32))
    lam = _lambda(lamv_ref, lam_init)
    o_ref[0] = _diff_combine(acc, l, lam, g_ref[...], lam_init, ts).astype(o_ref.dtype)


def _sample_attention(q, kb, vb, cache_k, cache_v, rel_bias, lamv, subln_g, lam_init):
    s, ts, _ = q.shape
    past = cache_k.shape[1]
    near = MAX_DISTANCE
    assert past >= near and past % V7X_SUBLANES == 0
    q_pos = past + jnp.arange(ts, dtype=jnp.int32)
    k_pos = (past - near) + jnp.arange(near + ts, dtype=jnp.int32)
    bias = _bias_tile(rel_bias, q_pos, k_pos)
    new = pl.BlockSpec((1, ts, HEAD_V), lambda si, h: (si, 0, h))
    cache = pl.BlockSpec((1, past, HEAD_V), lambda si, h: (si, 0, h))
    return pl.pallas_call(
        functools.partial(_sample_attn_kernel, ts=ts, past=past, near=near, lam_init=lam_init),
        out_shape=jax.ShapeDtypeStruct(q.shape, BF16),
        grid=(s, N_HEADS),
        in_specs=[_resident(lamv.shape), _resident((1, HEAD_V)), new, cache, cache, new, new,
                  pl.BlockSpec((1, 2 * ts, near + ts), lambda si, h: (h, 0, 0))],
        out_specs=new,
        compiler_params=pltpu.CompilerParams(
            dimension_semantics=("arbitrary", "arbitrary"), vmem_limit_bytes=V7X_VMEM_LIMIT_BYTES),
        name="sample_attention",
    )(lamv, subln_g.reshape(1, HEAD_V), q, cache_k, cache_v, kb, vb, bias)


def _finish_kernel(x_ref, a_ref, c_ref, g0_ref, b0_ref, wo_ref, g1_ref, b1_ref, w1_ref, w2_ref,
                   g2_ref, b2_ref, o_ref, *, alpha):
    xn = _layer_norm(x_ref[...], g0_ref[...], b0_ref[...])
    mix = jnp.concatenate([a_ref[...], c_ref[...]], axis=-1)
    y = jnp.dot(mix, wo_ref[...], preferred_element_type=F32)
    x1 = _layer_norm(alpha * xn + y, g1_ref[...], b1_ref[...])
    h = jnp.dot(x1.astype(BF16), w1_ref[...], preferred_element_type=F32)
    h = jnp.square(jnp.maximum(h, 0.0)).astype(BF16)
    f = jnp.dot(h, w2_ref[...], preferred_element_type=F32)
    o_ref[...] = _layer_norm(alpha * x1 + f, g2_ref[...], b2_ref[...])


def _finish(x2d, attn, conv, ln0_g, ln0_b, w_out_bf, ln1_g, ln1_b, w1_bf, w2_bf, ln2_g, ln2_b, alpha):
    n, d = x2d.shape
    a = attn.shape[1]
    c = conv.shape[1]
    f = w1_bf.shape[1]
    tm = min(ROW_TILE, n)
    assert n % tm == 0
    row = lambda w: pl.BlockSpec((tm, w), lambda i: (i, 0))
    vec = lambda p: p.reshape(1, d)
    return pl.pallas_call(
        functools.partial(_finish_kernel, alpha=alpha),
        out_shape=jax.ShapeDtypeStruct((n, d), F32),
        grid=(n // tm,),
        in_specs=[row(d), row(a), row(c), _resident((1, d)), _resident((1, d)), _resident((a + c, d)),
                  _resident((1, d)), _resident((1, d)), _resident((d, f)), _resident((f, d)),
                  _resident((1, d)), _resident((1, d))],
        out_specs=row(d),
        compiler_params=pltpu.CompilerParams(
            dimension_semantics=("arbitrary",), vmem_limit_bytes=V7X_VMEM_LIMIT_BYTES),
        name="finish",
    )(x2d, attn, conv, vec(ln0_g), vec(ln0_b), w_out_bf, vec(ln1_g), vec(ln1_b), w1_bf, w2_bf,
      vec(ln2_g), vec(ln2_b))


def kernel(x_prompt, x_sample, cache_k, cache_v, cache_conv, ln0_g, ln0_b, rel_bias, w_in, conv_w,
           lambda_q1, lambda_k1, lambda_q2, lambda_k2, subln_g, w_out, ln1_g, ln1_b,
           w_ff1, w_ff2, ln2_g, ln2_b):
    depth = w_in.shape[0]
    assert depth == 1, "single-layer step"
    layer = 0
    b, t, d = x_prompt.shape
    s, ts, _ = x_sample.shape
    past = cache_k.shape[2]
    c = conv_w.shape[-1]
    alpha = (2.0 * depth) ** 0.25
    lam_init = _lambda_init(layer)

    w_in_bf = w_in[layer].astype(BF16)
    w_out_bf = w_out[layer].astype(BF16)
    w1_bf = w_ff1[layer].astype(BF16)
    w2_bf = w_ff2[layer].astype(BF16)
    lamv = jnp.stack([lambda_q1[layer], lambda_k1[layer], lambda_q2[layer], lambda_k2[layer]]).astype(F32)
    ffn = (ln0_g, ln0_b, w_out_bf, ln1_g[layer], ln1_b[layer], w1_bf, w2_bf, ln2_g[layer], ln2_b[layer], alpha)

    xp2 = x_prompt.reshape(b * t, d)
    hist0 = jnp.zeros((b, CONV_K - 1, c), F32)
    qp, kp, vp, kpb, vpb, convp, tailp = _in_proj(xp2, t, ln0_g, ln0_b, w_in_bf, conv_w[layer], hist0)
    shp = (b, t, ATTN_WIDTH)
    attnp = _prompt_attention(qp.reshape(shp), kpb.reshape(shp), vpb.reshape(shp), rel_bias, lamv,
                              subln_g[layer], lam_init)
    yp = _finish(xp2, attnp.reshape(b * t, ATTN_WIDTH), convp, *ffn).reshape(b, t, d)

    xs2 = x_sample.reshape(s * ts, d)
    qs, ks, vs, ksb, vsb, convs, tails = _in_proj(xs2, ts, ln0_g, ln0_b, w_in_bf, conv_w[layer],
                                                  cache_conv[layer].astype(F32))
    shs = (s, ts, ATTN_WIDTH)
    attns = _sample_attention(qs.reshape(shs), ksb.reshape(shs), vsb.reshape(shs),
                              cache_k[layer].reshape(s, past, ATTN_WIDTH),
                              cache_v[layer].reshape(s, past, ATTN_WIDTH),
                              rel_bias, lamv, subln_g[layer], lam_init)
    ys = _finish(xs2, attns.reshape(s * ts, ATTN_WIDTH), convs, *ffn).reshape(s, ts, d)

    keep = CONV_K - 1
    return (yp, ys,
            kp.reshape(1, b, t, N_HEADS, HEAD_V), vp.reshape(1, b, t, N_HEADS, HEAD_V),
            tailp[:, V7X_SUBLANES - keep:, :].reshape(1, b, keep, c),
            ks.reshape(1, s, ts, N_HEADS, HEAD_V), vs.reshape(1, s, ts, N_HEADS, HEAD_V),
            tails[:, V7X_SUBLANES - keep:, :].reshape(1, s, keep, c))
```

```python
import functools
import math

import jax
import jax.numpy as jnp
from jax import lax
from jax.experimental import pallas as pl
from jax.experimental.pallas import tpu as pltpu

CHUNK = 64
N_HEADS = 4
HEAD_V = 128
HEAD_QK = HEAD_V // 2
ATTN_WIDTH = N_HEADS * HEAD_V
CONV_K = 3
N_BUCKETS = 32
MAX_DISTANCE = 128
LN_EPS = 1e-5
SUBLN_EPS = 1e-5

V7X_SUBLANES = 8
V7X_VMEM_LIMIT_BYTES = 56 * 1024 * 1024

ROW_TILE = 512
Q_TILE = 256
K_BLOCK = 1024
HEADS_PER_STEP = 2

F32 = jnp.float32
BF16 = jnp.bfloat16


def _lambda_init(layer):
    return 0.8 - 0.6 * math.exp(-0.3 * layer)


def _layer_norm(x, g, b):
    mu = jnp.mean(x, axis=-1, keepdims=True)
    xc = x - mu
    var = jnp.mean(xc * xc, axis=-1, keepdims=True)
    return xc * lax.rsqrt(var + LN_EPS) * g + b


def _resident(shape):
    nd = len(shape)
    return pl.BlockSpec(shape, lambda *_: (0,) * nd, pipeline_mode=pl.Buffered(1))


def _in_proj_kernel(x_ref, g_ref, b_ref, w_ref, cw_ref, hist_ref,
                    q_ref, k_ref, v_ref, kb_ref, vb_ref, conv_ref, utail_ref,
                    carry_ref, *, seg_len, tm):
    a = ATTN_WIDTH
    xn = _layer_norm(x_ref[...], g_ref[...], b_ref[...])
    z = jnp.dot(xn.astype(BF16), w_ref[...], preferred_element_type=F32)
    q_ref[...] = (z[:, :a] * (HEAD_QK ** -0.5)).astype(BF16)
    k = z[:, a:2 * a]
    v = z[:, 2 * a:3 * a]
    for h in range(N_HEADS):
        k_ref[pl.ds(h, tm, stride=N_HEADS), :] = k[:, h * HEAD_V:(h + 1) * HEAD_V]
        v_ref[pl.ds(h, tm, stride=N_HEADS), :] = v[:, h * HEAD_V:(h + 1) * HEAD_V]
    kb_ref[...] = k.astype(BF16)
    vb_ref[...] = v.astype(BF16)
    c = (z.shape[1] - 3 * a) // 3
    gb = z[:, 3 * a:3 * a + c]
    u = z[:, 3 * a + c:3 * a + 2 * c] * z[:, 3 * a + 2 * c:]
    rows = lax.broadcasted_iota(jnp.int32, (tm, 1), 0)
    if seg_len >= tm:
        tiles_per_seg = seg_len // tm
        first = (pl.program_id(0) % tiles_per_seg) == 0
        hist = hist_ref[0]
        prev = carry_ref[...]
        p1 = jnp.where(first, hist[1:2, :], prev[V7X_SUBLANES - 1:, :])
        p0 = jnp.where(first, hist[0:1, :], prev[V7X_SUBLANES - 2:V7X_SUBLANES - 1, :])
        pos = rows
        tail = u[tm - V7X_SUBLANES:, :]
        carry_ref[...] = tail
        utail_ref[0] = tail
    else:
        nseg = tm // seg_len
        hist = hist_ref[...]
        p1 = jnp.broadcast_to(hist[:, 1:2, :], (nseg, seg_len, c)).reshape(tm, c)
        p0 = jnp.broadcast_to(hist[:, 0:1, :], (nseg, seg_len, c)).reshape(tm, c)
        pos = rows & (seg_len - 1)
        utail_ref[...] = u.reshape(nseg, seg_len, c)[:, seg_len - V7X_SUBLANES:, :]
    u1 = jnp.where(pos == 0, p1, pltpu.roll(u, 1, 0))
    u2 = jnp.where(pos == 0, p0, jnp.where(pos == 1, p1, pltpu.roll(u, 2, 0)))
    cw = cw_ref[...]
    conv = gb * (u2 * cw[0:1, :] + u1 * cw[1:2, :] + u * cw[2:3, :])
    conv_ref[...] = conv.astype(BF16)


def _in_proj(x2d, seg_len, ln_g, ln_b, w_in_bf, conv_w, hist):
    n, d = x2d.shape
    e = w_in_bf.shape[1]
    a = ATTN_WIDTH
    c = (e - 3 * a) // 3
    tm = min(ROW_TILE, n)
    assert n % tm == 0 and (seg_len % tm == 0 or tm % seg_len == 0)
    assert seg_len & (seg_len - 1) == 0 and seg_len % V7X_SUBLANES == 0
    nseq = n // seg_len
    if seg_len >= tm:
        tiles_per_seg = seg_len // tm
        hist_spec = pl.BlockSpec((1, CONV_K - 1, c), lambda i: (i // tiles_per_seg, 0, 0))
        tail_spec = pl.BlockSpec((1, V7X_SUBLANES, c), lambda i: (i // tiles_per_seg, 0, 0))
    else:
        nseg = tm // seg_len
        hist_spec = pl.BlockSpec((nseg, CONV_K - 1, c), lambda i: (i, 0, 0))
        tail_spec = pl.BlockSpec((nseg, V7X_SUBLANES, c), lambda i: (i, 0, 0))
    row = lambda w: pl.BlockSpec((tm, w), lambda i: (i, 0))
    headrow = pl.BlockSpec((tm * N_HEADS, HEAD_V), lambda i: (i, 0))
    out_shape = (
        jax.ShapeDtypeStruct((n, a), BF16),
        jax.ShapeDtypeStruct((n * N_HEADS, HEAD_V), F32),
        jax.ShapeDtypeStruct((n * N_HEADS, HEAD_V), F32),
        jax.ShapeDtypeStruct((n, a), BF16),
        jax.ShapeDtypeStruct((n, a), BF16),
        jax.ShapeDtypeStruct((n, c), BF16),
        jax.ShapeDtypeStruct((nseq, V7X_SUBLANES, c), F32),
    )
    return pl.pallas_call(
        functools.partial(_in_proj_kernel, seg_len=seg_len, tm=tm),
        out_shape=out_shape,
        grid=(n // tm,),
        in_specs=[row(d), _resident((1, d)), _resident((1, d)), _resident((d, e)),
                  _resident((CONV_K, c)), hist_spec],
        out_specs=(row(a), headrow, headrow, row(a), row(a), row(c), tail_spec),
        scratch_shapes=[pltpu.VMEM((V7X_SUBLANES, c), F32)],
        compiler_params=pltpu.CompilerParams(
            dimension_semantics=("arbitrary",), vmem_limit_bytes=V7X_VMEM_LIMIT_BYTES),
        name="in_proj",
    )(x2d, ln_g.reshape(1, d), ln_b.reshape(1, d), w_in_bf, conv_w, hist)


def _stack_maps(q):
    lane = lax.broadcasted_iota(jnp.int32, q.shape, 1)
    zero = jnp.zeros_like(q)
    return jnp.concatenate([jnp.where(lane < HEAD_QK, q, zero), jnp.where(lane < HEAD_QK, zero, q)], axis=0)


def _scores(qz, kblk):
    return lax.dot_general(qz, kblk, (((1,), (1,)), ((), ())), preferred_element_type=F32)


def _add_to_both_maps(s, bias):
    t, n = bias.shape
    return (s.reshape(2, t, n) + bias[None]).reshape(2 * t, n)


def _lambda(lamv_ref, lam_init):
    lv = lamv_ref[...]
    s1 = jnp.sum(lv[0:1, :] * lv[1:2, :], axis=-1, keepdims=True)
    s2 = jnp.sum(lv[2:3, :] * lv[3:4, :], axis=-1, keepdims=True)
    return jnp.exp(s1) - jnp.exp(s2) + lam_init


def _diff_combine(acc, l, lam, g, lam_init, t):
    o = acc[:t] / l[:t] - lam * (acc[t:] / l[t:])
    ms = jnp.mean(o * o, axis=-1, keepdims=True)
    return o * lax.rsqrt(ms + SUBLN_EPS) * g * (1.0 - lam_init)


def _rel_bucket(rel):
    half = N_BUCKETS // 2
    max_exact = half // 2
    ret = jnp.where(rel > 0, half, 0)
    n = jnp.abs(rel)
    nf = jnp.maximum(n, 1).astype(F32)
    large = max_exact + (jnp.log(nf / max_exact) / math.log(MAX_DISTANCE / max_exact)
                         * (half - max_exact)).astype(jnp.int32)
    large = jnp.minimum(large, half - 1)
    return ret + jnp.where(n < max_exact, n, large)


def _bias_band(rel_bias, n_q, n_k, rel00):
    length = n_q + n_k
    rel = rel00 - (n_q - 1) + jnp.arange(length, dtype=jnp.int32)
    far = rel_bias[N_BUCKETS // 2 - 1].astype(F32)
    w = (rel_bias[_rel_bucket(rel)].astype(F32) - far[None, :]).T
    skew = jnp.tile(w, (1, n_q))[:, :n_q * (length - 1)].reshape(-1, n_q, length - 1)
    return skew[:, :, n_q - 1:n_q - 1 + n_k]


def _chunk_mask(tile, q_pos, k_pos):
    visible = (k_pos[None, :] // CHUNK) <= (q_pos[:, None] // CHUNK)
    return jnp.where(visible[None], tile, -jnp.inf)


def _prompt_attn_kernel(lamv_ref, g_ref, q_ref, k_ref, v_ref, *rest, tq, tk, hp, lam_init):
    sub = tk // tq
    win_refs, (bp_ref, o_ref, m_ref, l_ref, acc_ref) = rest[:sub], rest[sub:]
    qt = pl.program_id(2)
    last = qt // sub
    r = qt % sub
    lanes = [slice(h * HEAD_V, (h + 1) * HEAD_V) for h in range(hp)]
    qz = [_stack_maps(q_ref[0, :, ln]) for ln in lanes]

    def block(j, h):
        start = pl.multiple_of(j * tk, tk)
        return k_ref[0, pl.ds(start, tk), lanes[h]], v_ref[0, pl.ds(start, tk), lanes[h]]

    def accumulate(h, s, vblk):
        m_prev = m_ref[h]
        m_new = jnp.maximum(m_prev, jnp.max(s, axis=-1, keepdims=True))
        alpha = jnp.exp(m_prev - m_new)
        p = jnp.exp(s - m_new)
        l_ref[h] = alpha * l_ref[h] + jnp.sum(p, axis=-1, keepdims=True)
        acc_ref[h] = alpha * acc_ref[h] + jnp.dot(p.astype(BF16), vblk, preferred_element_type=F32)
        m_ref[h] = m_new

    for h in range(hp):
        kl, vl = block(last, h)
        s = _scores(qz[h], kl)
        s = _add_to_both_maps(s, jnp.concatenate([w[h] for w in win_refs], axis=-1))
        m0 = jnp.max(s, axis=-1, keepdims=True)
        p = jnp.exp(s - m0)
        m_ref[h] = m0
        l_ref[h] = jnp.sum(p, axis=-1, keepdims=True)
        acc_ref[h] = jnp.dot(p.astype(BF16), vl, preferred_element_type=F32)

    @pl.when((r == 0) & (qt > 0))
    def _():
        for h in range(hp):
            kp, vp = block(last - 1, h)
            s = _scores(qz[h], kp)
            tail = _add_to_both_maps(s[:, tk - tq:], bp_ref[h])
            accumulate(h, jnp.concatenate([s[:, :tk - tq], tail], axis=-1), vp)

    def far(j, carry):
        for h in range(hp):
            kj, vj = block(j, h)
            accumulate(h, _scores(qz[h], kj), vj)
        return carry

    lax.fori_loop(0, last - jnp.where(r == 0, 1, 0), far, 0)

    lam = _lambda(lamv_ref, lam_init)
    for h in range(hp):
        o_ref[0, :, lanes[h]] = _diff_combine(
            acc_ref[h], l_ref[h], lam, g_ref[...], lam_init, tq).astype(o_ref.dtype)


def _prompt_attention(q, kb, vb, rel_bias, lamv, subln_g, lam_init):
    b, t, _ = q.shape
    tq = min(Q_TILE, t)
    tk = min(K_BLOCK, t)
    sub = tk // tq
    assert t % tk == 0 and tk % tq == 0 and sub >= 2 and tq % CHUNK == 0 and tq >= MAX_DISTANCE
    pos = jnp.arange(tq, dtype=jnp.int32)
    band = _bias_band(rel_bias, tq, 2 * tq, -tq)
    prev = band[:, :, :tq]
    diag = _chunk_mask(band[:, :, tq:], pos, pos)
    zeros = jnp.zeros_like(prev)
    hidden = jnp.full_like(prev, -jnp.inf)
    strip = jnp.concatenate([zeros] * (sub - 2) + [prev, diag] + [hidden] * (sub - 1), axis=-1)
    hp = HEADS_PER_STEP
    assert N_HEADS % hp == 0
    qspec = pl.BlockSpec((1, tq, hp * HEAD_V), lambda bi, g, i: (bi, i, g))
    kvspec = pl.BlockSpec((1, t, hp * HEAD_V), lambda bi, g, i: (bi, 0, g), pipeline_mode=pl.Buffered(1))
    win = [pl.BlockSpec((hp, tq, tq), functools.partial(lambda bi, g, i, w: (g, 0, sub - 1 - i % sub + w), w=w))
           for w in range(sub)]
    return pl.pallas_call(
        functools.partial(_prompt_attn_kernel, tq=tq, tk=tk, hp=hp, lam_init=lam_init),
        out_shape=jax.ShapeDtypeStruct(q.shape, BF16),
        grid=(b, N_HEADS // hp, t // tq),
        in_specs=[_resident(lamv.shape), _resident((1, HEAD_V)), qspec, kvspec, kvspec, *win,
                  pl.BlockSpec((hp, tq, tq), lambda bi, g, i: (g, 0, 0))],
        out_specs=qspec,
        scratch_shapes=[pltpu.VMEM((hp, 2 * tq, 1), F32), pltpu.VMEM((hp, 2 * tq, 1), F32),
                        pltpu.VMEM((hp, 2 * tq, HEAD_V), F32)],
        compiler_params=pltpu.CompilerParams(
            dimension_semantics=("arbitrary", "arbitrary", "arbitrary"),
            vmem_limit_bytes=V7X_VMEM_LIMIT_BYTES),
        name="prompt_attention",
    )(lamv, subln_g.reshape(1, HEAD_V), q, kb, vb, *([strip] * sub), prev)


def _sample_attn_kernel(lamv_ref, g_ref, q_ref, ck_ref, cv_ref, kn_ref, vn_ref, bias_ref, o_ref,
                        *, ts, past, near, lam_init):
    far = past - near
    lam = _lambda(lamv_ref, lam_init)
    for h in range(N_HEADS):
        lanes = slice(h * HEAD_V, (h + 1) * HEAD_V)
        qz = _stack_maps(q_ref[0, :, lanes])
        ck = ck_ref[0, pl.ds(h, past, stride=N_HEADS), :].astype(BF16)
        cv = cv_ref[0, pl.ds(h, past, stride=N_HEADS), :].astype(BF16)
        bias = bias_ref[h]
        s_far = _scores(qz, ck[:far])
        s_near = _add_to_both_maps(_scores(qz, ck[far:]), bias[:, :near])
        s_new = _add_to_both_maps(_scores(qz, kn_ref[0, :, lanes]), bias[:, near:])
        m = jnp.maximum(jnp.max(s_far, axis=-1, keepdims=True),
                        jnp.maximum(jnp.max(s_near, axis=-1, keepdims=True),
                                    jnp.max(s_new, axis=-1, keepdims=True)))
        p_far = jnp.exp(s_far - m)
        p_near = jnp.exp(s_near - m)
        p_new = jnp.exp(s_new - m)
        l = (jnp.sum(p_far, axis=-1, keepdims=True) + jnp.sum(p_near, axis=-1, keepdims=True)
             + jnp.sum(p_new, axis=-1, keepdims=True))
        acc = (jnp.dot(p_far.astype(BF16), cv[:far], preferred_element_type=F32)
               + jnp.dot(p_near.astype(BF16), cv[far:], preferred_element_type=F32)
               + jnp.dot(p_new.astype(BF16), vn_ref[0, :, lanes], preferred_element_type=F32))
        o_ref[0, :, lanes] = _diff_combine(acc, l, lam, g_ref[...], lam_init, ts).astype(o_ref.dtype)


def _sample_attention(q, kb, vb, cache_k, cache_v, rel_bias, lamv, subln_g, lam_init):
    s, ts, _ = q.shape
    past = cache_k.shape[1] // N_HEADS
    near = MAX_DISTANCE
    assert past >= near and past % V7X_SUBLANES == 0
    q_pos = past + jnp.arange(ts, dtype=jnp.int32)
    k_pos = (past - near) + jnp.arange(near + ts, dtype=jnp.int32)
    bias = _chunk_mask(_bias_band(rel_bias, ts, near + ts, -near), q_pos, k_pos)
    new = pl.BlockSpec((1, ts, ATTN_WIDTH), lambda si: (si, 0, 0))
    cache = pl.BlockSpec((1, past * N_HEADS, HEAD_V), lambda si: (si, 0, 0))
    return pl.pallas_call(
        functools.partial(_sample_attn_kernel, ts=ts, past=past, near=near, lam_init=lam_init),
        out_shape=jax.ShapeDtypeStruct(q.shape, BF16),
        grid=(s,),
        in_specs=[_resident(lamv.shape), _resident((1, HEAD_V)), new, cache, cache, new, new,
                  _resident(bias.shape)],
        out_specs=new,
        compiler_params=pltpu.CompilerParams(
            dimension_semantics=("arbitrary",), vmem_limit_bytes=V7X_VMEM_LIMIT_BYTES),
        name="sample_attention",
    )(lamv, subln_g.reshape(1, HEAD_V), q, cache_k, cache_v, kb, vb, bias)


def _finish_kernel(x_ref, a_ref, c_ref, g0_ref, b0_ref, wo_ref, g1_ref, b1_ref, w1_ref, w2_ref,
                   g2_ref, b2_ref, o_ref, *, alpha):
    xn = _layer_norm(x_ref[...], g0_ref[...], b0_ref[...])
    mix = jnp.concatenate([a_ref[...], c_ref[...]], axis=-1)
    y = jnp.dot(mix, wo_ref[...], preferred_element_type=F32)
    x1 = _layer_norm(alpha * xn + y, g1_ref[...], b1_ref[...])
    h = jnp.dot(x1.astype(BF16), w1_ref[...], preferred_element_type=F32)
    h = jnp.square(jnp.maximum(h, 0.0)).astype(BF16)
    f = jnp.dot(h, w2_ref[...], preferred_element_type=F32)
    o_ref[...] = _layer_norm(alpha * x1 + f, g2_ref[...], b2_ref[...])


def _finish(x2d, attn, conv, ln0_g, ln0_b, w_out_bf, ln1_g, ln1_b, w1_bf, w2_bf, ln2_g, ln2_b, alpha):
    n, d = x2d.shape
    a = attn.shape[1]
    c = conv.shape[1]
    f = w1_bf.shape[1]
    tm = min(ROW_TILE, n)
    assert n % tm == 0
    row = lambda w: pl.BlockSpec((tm, w), lambda i: (i, 0))
    vec = lambda p: p.reshape(1, d)
    return pl.pallas_call(
        functools.partial(_finish_kernel, alpha=alpha),
        out_shape=jax.ShapeDtypeStruct((n, d), F32),
        grid=(n // tm,),
        in_specs=[row(d), row(a), row(c), _resident((1, d)), _resident((1, d)), _resident((a + c, d)),
                  _resident((1, d)), _resident((1, d)), _resident((d, f)), _resident((f, d)),
                  _resident((1, d)), _resident((1, d))],
        out_specs=row(d),
        compiler_params=pltpu.CompilerParams(
            dimension_semantics=("arbitrary",), vmem_limit_bytes=V7X_VMEM_LIMIT_BYTES),
        name="finish",
    )(x2d, attn, conv, vec(ln0_g), vec(ln0_b), w_out_bf, vec(ln1_g), vec(ln1_b), w1_bf, w2_bf,
      vec(ln2_g), vec(ln2_b))


def kernel(x_prompt, x_sample, cache_k, cache_v, cache_conv, ln0_g, ln0_b, rel_bias, w_in, conv_w,
           lambda_q1, lambda_k1, lambda_q2, lambda_k2, subln_g, w_out, ln1_g, ln1_b,
           w_ff1, w_ff2, ln2_g, ln2_b):
    depth = w_in.shape[0]
    assert depth == 1, "single-layer step"
    layer = 0
    b, t, d = x_prompt.shape
    s, ts, _ = x_sample.shape
    past = cache_k.shape[2]
    c = conv_w.shape[-1]
    alpha = (2.0 * depth) ** 0.25
    lam_init = _lambda_init(layer)

    w_in_bf = w_in[layer].astype(BF16)
    w_out_bf = w_out[layer].astype(BF16)
    w1_bf = w_ff1[layer].astype(BF16)
    w2_bf = w_ff2[layer].astype(BF16)
    lamv = jnp.stack([lambda_q1[layer], lambda_k1[layer], lambda_q2[layer], lambda_k2[layer]]).astype(F32)
    ffn = (ln0_g, ln0_b, w_out_bf, ln1_g[layer], ln1_b[layer], w1_bf, w2_bf, ln2_g[layer], ln2_b[layer], alpha)

    xp2 = x_prompt.reshape(b * t, d)
    hist0 = jnp.zeros((b, CONV_K - 1, c), F32)
    qp, kp, vp, kpb, vpb, convp, tailp = _in_proj(xp2, t, ln0_g, ln0_b, w_in_bf, conv_w[layer], hist0)
    shp = (b, t, ATTN_WIDTH)
    attnp = _prompt_attention(qp.reshape(shp), kpb.reshape(shp), vpb.reshape(shp), rel_bias, lamv,
                              subln_g[layer], lam_init)
    yp = _finish(xp2, attnp.reshape(b * t, ATTN_WIDTH), convp, *ffn).reshape(b, t, d)

    xs2 = x_sample.reshape(s * ts, d)
    qs, ks, vs, ksb, vsb, convs, tails = _in_proj(xs2, ts, ln0_g, ln0_b, w_in_bf, conv_w[layer],
                                                  cache_conv[layer].astype(F32))
    shs = (s, ts, ATTN_WIDTH)
    attns = _sample_attention(qs.reshape(shs), ksb.reshape(shs), vsb.reshape(shs),
                              cache_k[layer].astype(F32).reshape(s, past * N_HEADS, HEAD_V),
                              cache_v[layer].astype(F32).reshape(s, past * N_HEADS, HEAD_V),
                              rel_bias, lamv, subln_g[layer], lam_init)
    ys = _finish(xs2, attns.reshape(s * ts, ATTN_WIDTH), convs, *ffn).reshape(s, ts, d)

    keep = CONV_K - 1
    return (yp, ys,
            kp.reshape(1, b, t, N_HEADS, HEAD_V), vp.reshape(1, b, t, N_HEADS, HEAD_V),
            tailp[:, V7X_SUBLANES - keep:, :].reshape(1, b, keep, c),
            ks.reshape(1, s, ts, N_HEADS, HEAD_V), vs.reshape(1, s, ts, N_HEADS, HEAD_V),
            tails[:, V7X_SUBLANES - keep:, :].reshape(1, s, keep, c))
```

```python
import functools
import math

import jax
import jax.numpy as jnp
from jax import lax
from jax.experimental import pallas as pl
from jax.experimental.pallas import tpu as pltpu

CHUNK = 64
N_HEADS = 4
HEAD_V = 128
HEAD_QK = HEAD_V // 2
ATTN_WIDTH = N_HEADS * HEAD_V
CONV_K = 3
N_BUCKETS = 32
MAX_DISTANCE = 128
LN_EPS = 1e-5
SUBLN_EPS = 1e-5

V7X_SUBLANES = 8
V7X_VMEM_LIMIT_BYTES = 56 * 1024 * 1024

ROW_TILE = 512
Q_TILE = 256
K_BLOCK = 1024
HEADS_PER_STEP = 4

F32 = jnp.float32
BF16 = jnp.bfloat16


def _lambda_init(layer):
    return 0.8 - 0.6 * math.exp(-0.3 * layer)


def _layer_norm(x, g, b):
    mu = jnp.mean(x, axis=-1, keepdims=True)
    xc = x - mu
    var = jnp.mean(xc * xc, axis=-1, keepdims=True)
    return xc * lax.rsqrt(var + LN_EPS) * g + b


def _resident(shape):
    nd = len(shape)
    return pl.BlockSpec(shape, lambda *_: (0,) * nd, pipeline_mode=pl.Buffered(1))


def _in_proj_kernel(x_ref, g_ref, b_ref, w_ref, cw_ref, hist_ref,
                    q_ref, k_ref, v_ref, kb_ref, vb_ref, conv_ref, utail_ref,
                    carry_ref, *, seg_len, tm):
    a = ATTN_WIDTH
    xn = _layer_norm(x_ref[...], g_ref[...], b_ref[...])
    z = jnp.dot(xn.astype(BF16), w_ref[...], preferred_element_type=F32)
    q_ref[...] = (z[:, :a] * (HEAD_QK ** -0.5)).astype(BF16)
    k = z[:, a:2 * a]
    v = z[:, 2 * a:3 * a]
    for h in range(N_HEADS):
        k_ref[pl.ds(h, tm, stride=N_HEADS), :] = k[:, h * HEAD_V:(h + 1) * HEAD_V]
        v_ref[pl.ds(h, tm, stride=N_HEADS), :] = v[:, h * HEAD_V:(h + 1) * HEAD_V]
    kb_ref[...] = k.astype(BF16)
    vb_ref[...] = v.astype(BF16)
    c = (z.shape[1] - 3 * a) // 3
    gb = z[:, 3 * a:3 * a + c]
    u = z[:, 3 * a + c:3 * a + 2 * c] * z[:, 3 * a + 2 * c:]
    rows = lax.broadcasted_iota(jnp.int32, (tm, 1), 0)
    if seg_len >= tm:
        tiles_per_seg = seg_len // tm
        first = (pl.program_id(0) % tiles_per_seg) == 0
        hist = hist_ref[0]
        prev = carry_ref[...]
        p1 = jnp.where(first, hist[1:2, :], prev[V7X_SUBLANES - 1:, :])
        p0 = jnp.where(first, hist[0:1, :], prev[V7X_SUBLANES - 2:V7X_SUBLANES - 1, :])
        pos = rows
        tail = u[tm - V7X_SUBLANES:, :]
        carry_ref[...] = tail
        utail_ref[0] = tail
    else:
        nseg = tm // seg_len
        hist = hist_ref[...]
        p1 = jnp.broadcast_to(hist[:, 1:2, :], (nseg, seg_len, c)).reshape(tm, c)
        p0 = jnp.broadcast_to(hist[:, 0:1, :], (nseg, seg_len, c)).reshape(tm, c)
        pos = rows & (seg_len - 1)
        utail_ref[...] = u.reshape(nseg, seg_len, c)[:, seg_len - V7X_SUBLANES:, :]
    u1 = jnp.where(pos == 0, p1, pltpu.roll(u, 1, 0))
    u2 = jnp.where(pos == 0, p0, jnp.where(pos == 1, p1, pltpu.roll(u, 2, 0)))
    cw = cw_ref[...]
    conv = gb * (u2 * cw[0:1, :] + u1 * cw[1:2, :] + u * cw[2:3, :])
    conv_ref[...] = conv.astype(BF16)


def _in_proj(x2d, seg_len, ln_g, ln_b, w_in_bf, conv_w, hist):
    n, d = x2d.shape
    e = w_in_bf.shape[1]
    a = ATTN_WIDTH
    c = (e - 3 * a) // 3
    tm = min(ROW_TILE, n)
    assert n % tm == 0 and (seg_len % tm == 0 or tm % seg_len == 0)
    assert seg_len & (seg_len - 1) == 0 and seg_len % V7X_SUBLANES == 0
    nseq = n // seg_len
    if seg_len >= tm:
        tiles_per_seg = seg_len // tm
        hist_spec = pl.BlockSpec((1, CONV_K - 1, c), lambda i: (i // tiles_per_seg, 0, 0))
        tail_spec = pl.BlockSpec((1, V7X_SUBLANES, c), lambda i: (i // tiles_per_seg, 0, 0))
    else:
        nseg = tm // seg_len
        hist_spec = pl.BlockSpec((nseg, CONV_K - 1, c), lambda i: (i, 0, 0))
        tail_spec = pl.BlockSpec((nseg, V7X_SUBLANES, c), lambda i: (i, 0, 0))
    row = lambda w: pl.BlockSpec((tm, w), lambda i: (i, 0))
    headrow = pl.BlockSpec((tm * N_HEADS, HEAD_V), lambda i: (i, 0))
    out_shape = (
        jax.ShapeDtypeStruct((n, a), BF16),
        jax.ShapeDtypeStruct((n * N_HEADS, HEAD_V), F32),
        jax.ShapeDtypeStruct((n * N_HEADS, HEAD_V), F32),
        jax.ShapeDtypeStruct((n, a), BF16),
        jax.ShapeDtypeStruct((n, a), BF16),
        jax.ShapeDtypeStruct((n, c), BF16),
        jax.ShapeDtypeStruct((nseq, V7X_SUBLANES, c), F32),
    )
    return pl.pallas_call(
        functools.partial(_in_proj_kernel, seg_len=seg_len, tm=tm),
        out_shape=out_shape,
        grid=(n // tm,),
        in_specs=[row(d), _resident((1, d)), _resident((1, d)), _resident((d, e)),
                  _resident((CONV_K, c)), hist_spec],
        out_specs=(row(a), headrow, headrow, row(a), row(a), row(c), tail_spec),
        scratch_shapes=[pltpu.VMEM((V7X_SUBLANES, c), F32)],
        compiler_params=pltpu.CompilerParams(
            dimension_semantics=("arbitrary",), vmem_limit_bytes=V7X_VMEM_LIMIT_BYTES),
        name="in_proj",
    )(x2d, ln_g.reshape(1, d), ln_b.reshape(1, d), w_in_bf, conv_w, hist)


def _stack_maps(q):
    lane = lax.broadcasted_iota(jnp.int32, q.shape, 1)
    zero = jnp.zeros_like(q)
    return jnp.concatenate([jnp.where(lane < HEAD_QK, q, zero), jnp.where(lane < HEAD_QK, zero, q)], axis=0)


def _scores(qz, kblk):
    return lax.dot_general(qz, kblk, (((1,), (1,)), ((), ())), preferred_element_type=F32)


def _add_to_both_maps(s, bias):
    t, n = bias.shape
    return (s.reshape(2, t, n) + bias[None]).reshape(2 * t, n)


def _lambda(lamv_ref, lam_init):
    lv = lamv_ref[...]
    s1 = jnp.sum(lv[0:1, :] * lv[1:2, :], axis=-1, keepdims=True)
    s2 = jnp.sum(lv[2:3, :] * lv[3:4, :], axis=-1, keepdims=True)
    return jnp.exp(s1) - jnp.exp(s2) + lam_init


def _diff_combine(acc, l, lam, g, lam_init, t):
    o = acc[:t] / l[:t] - lam * (acc[t:] / l[t:])
    ms = jnp.mean(o * o, axis=-1, keepdims=True)
    return o * lax.rsqrt(ms + SUBLN_EPS) * g * (1.0 - lam_init)


def _rel_bucket(rel):
    half = N_BUCKETS // 2
    max_exact = half // 2
    ret = jnp.where(rel > 0, half, 0)
    n = jnp.abs(rel)
    nf = jnp.maximum(n, 1).astype(F32)
    large = max_exact + (jnp.log(nf / max_exact) / math.log(MAX_DISTANCE / max_exact)
                         * (half - max_exact)).astype(jnp.int32)
    large = jnp.minimum(large, half - 1)
    return ret + jnp.where(n < max_exact, n, large)


def _bias_band(rel_bias, n_q, n_k, rel00):
    length = n_q + n_k
    rel = rel00 - (n_q - 1) + jnp.arange(length, dtype=jnp.int32)
    far = rel_bias[N_BUCKETS // 2 - 1].astype(F32)
    w = (rel_bias[_rel_bucket(rel)].astype(F32) - far[None, :]).T
    skew = jnp.tile(w, (1, n_q))[:, :n_q * (length - 1)].reshape(-1, n_q, length - 1)
    return skew[:, :, n_q - 1:n_q - 1 + n_k]


def _chunk_mask(tile, q_pos, k_pos):
    visible = (k_pos[None, :] // CHUNK) <= (q_pos[:, None] // CHUNK)
    return jnp.where(visible[None], tile, -jnp.inf)


SUM_LO = 2.0 ** -60
SUM_HI = 2.0 ** 60


def _prompt_attn_kernel(lamv_ref, g_ref, q_ref, k_ref, v_ref, bd_ref, bp_ref, o_ref,
                        acc_ref, lp_ref, l_ref, m_ref, *, tq, tk, hp, lam_init):
    sub = tk // tq
    qt = pl.program_id(2)
    lanes = [slice(h * HEAD_V, (h + 1) * HEAD_V) for h in range(hp)]
    qz = [_stack_maps(q_ref[0, :, ln]) for ln in lanes]

    def kv(j, h):
        start = pl.multiple_of(j * tq, tq)
        return k_ref[0, pl.ds(start, tq), lanes[h]], v_ref[0, pl.ds(start, tq), lanes[h]]

    def lane_fold(p):
        out = p[:, :HEAD_V]
        for c in range(1, p.shape[1] // HEAD_V):
            out = out + p[:, c * HEAD_V:(c + 1) * HEAD_V]
        return out

    def numerators(h, j, bias):
        kj, vj = kv(j, h)
        s = _scores(qz[h], kj)
        if bias is not None:
            s = _add_to_both_maps(s, bias)
        p = jnp.exp(s)
        return jnp.dot(p.astype(BF16), vj, preferred_element_type=F32), lane_fold(p)

    for h in range(hp):
        pv, lp = numerators(h, qt, bd_ref[h])
        acc_ref[h] = pv
        lp_ref[h] = lp

    @pl.when(qt > 0)
    def _():
        for h in range(hp):
            pv, lp = numerators(h, qt - 1, bp_ref[h])
            acc_ref[h] += pv
            lp_ref[h] += lp

    n_far = jnp.maximum(qt - 1, 0)
    n_full = n_far // sub

    def single(j, carry):
        for h in range(hp):
            pv, lp = numerators(h, j, None)
            acc_ref[h] += pv
            lp_ref[h] += lp
        return carry

    lax.fori_loop(n_full * sub, n_far, single, 0)

    def full(jb, carry):
        for h in range(hp):
            pv, lp = numerators(h, jb * sub, None)
            for c in range(1, sub):
                pv_c, lp_c = numerators(h, jb * sub + c, None)
                pv, lp = pv + pv_c, lp + lp_c
            acc_ref[h] += pv
            lp_ref[h] += lp
        return carry

    lax.fori_loop(0, n_full, full, 0)

    for h in range(hp):
        l_ref[h] = jnp.sum(lp_ref[h], axis=-1, keepdims=True)
    l_all = l_ref[...]
    in_range = (l_all >= SUM_LO) & (l_all <= SUM_HI)

    @pl.when(jnp.min(jnp.where(in_range, 1.0, 0.0)) < 0.5)
    def _():
        for h in range(hp):
            kd, vd = kv(qt, h)
            s = _add_to_both_maps(_scores(qz[h], kd), bd_ref[h])
            m0 = jnp.max(s, axis=-1, keepdims=True)
            p = jnp.exp(s - m0)
            m_ref[h] = m0
            l_ref[h] = jnp.sum(p, axis=-1, keepdims=True)
            acc_ref[h] = jnp.dot(p.astype(BF16), vd, preferred_element_type=F32)

            def step(j, carry, h=h):
                kj, vj = kv(j, h)
                near = jnp.where(j == qt - 1, bp_ref[h], jnp.zeros_like(bp_ref[h]))
                s = _add_to_both_maps(_scores(qz[h], kj), near)
                m_prev = m_ref[h]
                m_new = jnp.maximum(m_prev, jnp.max(s, axis=-1, keepdims=True))
                alpha = jnp.exp(m_prev - m_new)
                p = jnp.exp(s - m_new)
                l_ref[h] = alpha * l_ref[h] + jnp.sum(p, axis=-1, keepdims=True)
                acc_ref[h] = alpha * acc_ref[h] + jnp.dot(p.astype(BF16), vj, preferred_element_type=F32)
                m_ref[h] = m_new
                return carry

            lax.fori_loop(0, qt, step, 0)

    lam = _lambda(lamv_ref, lam_init)
    for h in range(hp):
        o_ref[0, :, lanes[h]] = _diff_combine(
            acc_ref[h], l_ref[h], lam, g_ref[...], lam_init, tq).astype(o_ref.dtype)


def _prompt_attention(q, kb, vb, rel_bias, lamv, subln_g, lam_init):
    b, t, _ = q.shape
    tq = min(Q_TILE, t)
    tk = min(K_BLOCK, t)
    sub = tk // tq
    assert t % tk == 0 and tk % tq == 0 and sub >= 2 and tq % CHUNK == 0 and tq >= MAX_DISTANCE
    pos = jnp.arange(tq, dtype=jnp.int32)
    band = _bias_band(rel_bias, tq, 2 * tq, -tq)
    prev = band[:, :, :tq]
    diag = _chunk_mask(band[:, :, tq:], pos, pos)
    hp = HEADS_PER_STEP
    assert N_HEADS % hp == 0
    qspec = pl.BlockSpec((1, tq, hp * HEAD_V), lambda bi, g, i: (bi, i, g))
    kvspec = pl.BlockSpec((1, t, hp * HEAD_V), lambda bi, g, i: (bi, 0, g), pipeline_mode=pl.Buffered(1))
    bspec = pl.BlockSpec((hp, tq, tq), lambda bi, g, i: (g, 0, 0))
    return pl.pallas_call(
        functools.partial(_prompt_attn_kernel, tq=tq, tk=tk, hp=hp, lam_init=lam_init),
        out_shape=jax.ShapeDtypeStruct(q.shape, BF16),
        grid=(b, N_HEADS // hp, t // tq),
        in_specs=[_resident(lamv.shape), _resident((1, HEAD_V)), qspec, kvspec, kvspec, bspec, bspec],
        out_specs=qspec,
        scratch_shapes=[pltpu.VMEM((hp, 2 * tq, HEAD_V), F32), pltpu.VMEM((hp, 2 * tq, HEAD_V), F32),
                        pltpu.VMEM((hp, 2 * tq, 1), F32), pltpu.VMEM((hp, 2 * tq, 1), F32)],
        compiler_params=pltpu.CompilerParams(
            dimension_semantics=("arbitrary", "arbitrary", "arbitrary"),
            vmem_limit_bytes=V7X_VMEM_LIMIT_BYTES),
        name="prompt_attention",
    )(lamv, subln_g.reshape(1, HEAD_V), q, kb, vb, diag, prev)


def _sample_attn_kernel(lamv_ref, g_ref, q_ref, ck_ref, cv_ref, kn_ref, vn_ref, bias_ref, o_ref,
                        *, ts, past, near, lam_init):
    far = past - near
    lam = _lambda(lamv_ref, lam_init)
    for h in range(N_HEADS):
        lanes = slice(h * HEAD_V, (h + 1) * HEAD_V)
        qz = _stack_maps(q_ref[0, :, lanes])
        ck = ck_ref[0, pl.ds(h, past, stride=N_HEADS), :].astype(BF16)
        cv = cv_ref[0, pl.ds(h, past, stride=N_HEADS), :].astype(BF16)
        bias = bias_ref[h]
        s_far = _scores(qz, ck[:far])
        s_near = _add_to_both_maps(_scores(qz, ck[far:]), bias[:, :near])
        s_new = _add_to_both_maps(_scores(qz, kn_ref[0, :, lanes]), bias[:, near:])
        m = jnp.maximum(jnp.max(s_far, axis=-1, keepdims=True),
                        jnp.maximum(jnp.max(s_near, axis=-1, keepdims=True),
                                    jnp.max(s_new, axis=-1, keepdims=True)))
        p_far = jnp.exp(s_far - m)
        p_near = jnp.exp(s_near - m)
        p_new = jnp.exp(s_new - m)
        l = (jnp.sum(p_far, axis=-1, keepdims=True) + jnp.sum(p_near, axis=-1, keepdims=True)
             + jnp.sum(p_new, axis=-1, keepdims=True))
        acc = (jnp.dot(p_far.astype(BF16), cv[:far], preferred_element_type=F32)
               + jnp.dot(p_near.astype(BF16), cv[far:], preferred_element_type=F32)
               + jnp.dot(p_new.astype(BF16), vn_ref[0, :, lanes], preferred_element_type=F32))
        o_ref[0, :, lanes] = _diff_combine(acc, l, lam, g_ref[...], lam_init, ts).astype(o_ref.dtype)


def _sample_attention(q, kb, vb, cache_k, cache_v, rel_bias, lamv, subln_g, lam_init):
    s, ts, _ = q.shape
    past = cache_k.shape[1] // N_HEADS
    near = MAX_DISTANCE
    assert past >= near and past % V7X_SUBLANES == 0
    q_pos = past + jnp.arange(ts, dtype=jnp.int32)
    k_pos = (past - near) + jnp.arange(near + ts, dtype=jnp.int32)
    bias = _chunk_mask(_bias_band(rel_bias, ts, near + ts, -near), q_pos, k_pos)
    new = pl.BlockSpec((1, ts, ATTN_WIDTH), lambda si: (si, 0, 0))
    cache = pl.BlockSpec((1, past * N_HEADS, HEAD_V), lambda si: (si, 0, 0))
    return pl.pallas_call(
        functools.partial(_sample_attn_kernel, ts=ts, past=past, near=near, lam_init=lam_init),
        out_shape=jax.ShapeDtypeStruct(q.shape, BF16),
        grid=(s,),
        in_specs=[_resident(lamv.shape), _resident((1, HEAD_V)), new, cache, cache, new, new,
                  _resident(bias.shape)],
        out_specs=new,
        compiler_params=pltpu.CompilerParams(
            dimension_semantics=("arbitrary",), vmem_limit_bytes=V7X_VMEM_LIMIT_BYTES),
        name="sample_attention",
    )(lamv, subln_g.reshape(1, HEAD_V), q, cache_k, cache_v, kb, vb, bias)


def _finish_kernel(x_ref, a_ref, c_ref, g0_ref, b0_ref, wo_ref, g1_ref, b1_ref, w1_ref, w2_ref,
                   g2_ref, b2_ref, o_ref, *, alpha):
    xn = _layer_norm(x_ref[...], g0_ref[...], b0_ref[...])
    mix = jnp.concatenate([a_ref[...], c_ref[...]], axis=-1)
    y = jnp.dot(mix, wo_ref[...], preferred_element_type=F32)
    x1 = _layer_norm(alpha * xn + y, g1_ref[...], b1_ref[...])
    h = jnp.dot(x1.astype(BF16), w1_ref[...], preferred_element_type=F32)
    h = jnp.square(jnp.maximum(h, 0.0)).astype(BF16)
    f = jnp.dot(h, w2_ref[...], preferred_element_type=F32)
    o_ref[...] = _layer_norm(alpha * x1 + f, g2_ref[...], b2_ref[...])


def _finish(x2d, attn, conv, ln0_g, ln0_b, w_out_bf, ln1_g, ln1_b, w1_bf, w2_bf, ln2_g, ln2_b, alpha):
    n, d = x2d.shape
    a = attn.shape[1]
    c = conv.shape[1]
    f = w1_bf.shape[1]
    tm = min(ROW_TILE, n)
    assert n % tm == 0
    row = lambda w: pl.BlockSpec((tm, w), lambda i: (i, 0))
    vec = lambda p: p.reshape(1, d)
    return pl.pallas_call(
        functools.partial(_finish_kernel, alpha=alpha),
        out_shape=jax.ShapeDtypeStruct((n, d), F32),
        grid=(n // tm,),
        in_specs=[row(d), row(a), row(c), _resident((1, d)), _resident((1, d)), _resident((a + c, d)),
                  _resident((1, d)), _resident((1, d)), _resident((d, f)), _resident((f, d)),
                  _resident((1, d)), _resident((1, d))],
        out_specs=row(d),
        compiler_params=pltpu.CompilerParams(
            dimension_semantics=("arbitrary",), vmem_limit_bytes=V7X_VMEM_LIMIT_BYTES),
        name="finish",
    )(x2d, attn, conv, vec(ln0_g), vec(ln0_b), w_out_bf, vec(ln1_g), vec(ln1_b), w1_bf, w2_bf,
      vec(ln2_g), vec(ln2_b))


def kernel(x_prompt, x_sample, cache_k, cache_v, cache_conv, ln0_g, ln0_b, rel_bias, w_in, conv_w,
           lambda_q1, lambda_k1, lambda_q2, lambda_k2, subln_g, w_out, ln1_g, ln1_b,
           w_ff1, w_ff2, ln2_g, ln2_b):
    depth = w_in.shape[0]
    assert depth == 1, "single-layer step"
    layer = 0
    b, t, d = x_prompt.shape
    s, ts, _ = x_sample.shape
    past = cache_k.shape[2]
    c = conv_w.shape[-1]
    alpha = (2.0 * depth) ** 0.25
    lam_init = _lambda_init(layer)

    w_in_bf = w_in[layer].astype(BF16)
    w_out_bf = w_out[layer].astype(BF16)
    w1_bf = w_ff1[layer].astype(BF16)
    w2_bf = w_ff2[layer].astype(BF16)
    lamv = jnp.stack([lambda_q1[layer], lambda_k1[layer], lambda_q2[layer], lambda_k2[layer]]).astype(F32)
    ffn = (ln0_g, ln0_b, w_out_bf, ln1_g[layer], ln1_b[layer], w1_bf, w2_bf, ln2_g[layer], ln2_b[layer], alpha)

    xp2 = x_prompt.reshape(b * t, d)
    hist0 = jnp.zeros((b, CONV_K - 1, c), F32)
    qp, kp, vp, kpb, vpb, convp, tailp = _in_proj(xp2, t, ln0_g, ln0_b, w_in_bf, conv_w[layer], hist0)
    shp = (b, t, ATTN_WIDTH)
    attnp = _prompt_attention(qp.reshape(shp), kpb.reshape(shp), vpb.reshape(shp), rel_bias, lamv,
                              subln_g[layer], lam_init)
    yp = _finish(xp2, attnp.reshape(b * t, ATTN_WIDTH), convp, *ffn).reshape(b, t, d)

    xs2 = x_sample.reshape(s * ts, d)
    qs, ks, vs, ksb, vsb, convs, tails = _in_proj(xs2, ts, ln0_g, ln0_b, w_in_bf, conv_w[layer],
                                                  cache_conv[layer].astype(F32))
    shs = (s, ts, ATTN_WIDTH)
    attns = _sample_attention(qs.reshape(shs), ksb.reshape(shs), vsb.reshape(shs),
                              cache_k[layer].astype(F32).reshape(s, past * N_HEADS, HEAD_V),
                              cache_v[layer].astype(F32).reshape(s, past * N_HEADS, HEAD_V),
                              rel_bias, lamv, subln_g[layer], lam_init)
    ys = _finish(xs2, attns.reshape(s * ts, ATTN_WIDTH), convs, *ffn).reshape(s, ts, d)

    keep = CONV_K - 1
    return (yp, ys,
            kp.reshape(1, b, t, N_HEADS, HEAD_V), vp.reshape(1, b, t, N_HEADS, HEAD_V),
            tailp[:, V7X_SUBLANES - keep:, :].reshape(1, b, keep, c),
            ks.reshape(1, s, ts, N_HEADS, HEAD_V), vs.reshape(1, s, ts, N_HEADS, HEAD_V),
            tails[:, V7X_SUBLANES - keep:, :].reshape(1, s, keep, c))
```

```python
import functools
import math

import jax
import jax.numpy as jnp
from jax import lax
from jax.experimental import pallas as pl
from jax.experimental.pallas import tpu as pltpu

CHUNK = 64
N_HEADS = 4
HEAD_V = 128
HEAD_QK = HEAD_V // 2
ATTN_WIDTH = N_HEADS * HEAD_V
CONV_K = 3
N_BUCKETS = 32
MAX_DISTANCE = 128
LN_EPS = 1e-5
SUBLN_EPS = 1e-5

V7X_SUBLANES = 8
V7X_VMEM_LIMIT_BYTES = 56 * 1024 * 1024

ROW_TILE = 512
Q_TILE = 256
K_BLOCK = 1024
HEADS_PER_STEP = 4

F32 = jnp.float32
BF16 = jnp.bfloat16


def _lambda_init(layer):
    return 0.8 - 0.6 * math.exp(-0.3 * layer)


def _layer_norm(x, g, b):
    mu = jnp.mean(x, axis=-1, keepdims=True)
    xc = x - mu
    var = jnp.mean(xc * xc, axis=-1, keepdims=True)
    return xc * lax.rsqrt(var + LN_EPS) * g + b


def _resident(shape):
    nd = len(shape)
    return pl.BlockSpec(shape, lambda *_: (0,) * nd, pipeline_mode=pl.Buffered(1))


def _in_proj_kernel(x_ref, g_ref, b_ref, w_ref, cw_ref, hist_ref,
                    q_ref, k_ref, v_ref, kb_ref, vb_ref, conv_ref, utail_ref,
                    carry_ref, *, seg_len, tm, qv_tile):
    a = ATTN_WIDTH
    xn = _layer_norm(x_ref[...], g_ref[...], b_ref[...])
    z = jnp.dot(xn.astype(BF16), w_ref[...], preferred_element_type=F32)
    q = z[:, :a] * (HEAD_QK ** -0.5)
    k = z[:, a:2 * a]
    v = z[:, 2 * a:3 * a]
    for h in range(N_HEADS):
        k_ref[pl.ds(h, tm, stride=N_HEADS), :] = k[:, h * HEAD_V:(h + 1) * HEAD_V]
        v_ref[pl.ds(h, tm, stride=N_HEADS), :] = v[:, h * HEAD_V:(h + 1) * HEAD_V]
    kb_ref[...] = k.astype(BF16)
    if qv_tile is None:
        q_ref[...] = q.astype(BF16)
        vb_ref[...] = v.astype(BF16)
    else:
        for s in range(tm // qv_tile):
            rows_s = slice(s * qv_tile, (s + 1) * qv_tile)
            q_ref[s] = q[rows_s, :].T.astype(BF16)
            vb_ref[s] = v[rows_s, :].T.astype(BF16)
    c = (z.shape[1] - 3 * a) // 3
    gb = z[:, 3 * a:3 * a + c]
    u = z[:, 3 * a + c:3 * a + 2 * c] * z[:, 3 * a + 2 * c:]
    rows = lax.broadcasted_iota(jnp.int32, (tm, 1), 0)
    if seg_len >= tm:
        tiles_per_seg = seg_len // tm
        first = (pl.program_id(0) % tiles_per_seg) == 0
        hist = hist_ref[0]
        prev = carry_ref[...]
        p1 = jnp.where(first, hist[1:2, :], prev[V7X_SUBLANES - 1:, :])
        p0 = jnp.where(first, hist[0:1, :], prev[V7X_SUBLANES - 2:V7X_SUBLANES - 1, :])
        pos = rows
        tail = u[tm - V7X_SUBLANES:, :]
        carry_ref[...] = tail
        utail_ref[0] = tail
    else:
        nseg = tm // seg_len
        hist = hist_ref[...]
        p1 = jnp.broadcast_to(hist[:, 1:2, :], (nseg, seg_len, c)).reshape(tm, c)
        p0 = jnp.broadcast_to(hist[:, 0:1, :], (nseg, seg_len, c)).reshape(tm, c)
        pos = rows & (seg_len - 1)
        utail_ref[...] = u.reshape(nseg, seg_len, c)[:, seg_len - V7X_SUBLANES:, :]
    u1 = jnp.where(pos == 0, p1, pltpu.roll(u, 1, 0))
    u2 = jnp.where(pos == 0, p0, jnp.where(pos == 1, p1, pltpu.roll(u, 2, 0)))
    cw = cw_ref[...]
    conv = gb * (u2 * cw[0:1, :] + u1 * cw[1:2, :] + u * cw[2:3, :])
    conv_ref[...] = conv.astype(BF16)


def _in_proj(x2d, seg_len, ln_g, ln_b, w_in_bf, conv_w, hist, qv_tile=None):
    n, d = x2d.shape
    e = w_in_bf.shape[1]
    a = ATTN_WIDTH
    c = (e - 3 * a) // 3
    tm = min(ROW_TILE, n)
    assert n % tm == 0 and (seg_len % tm == 0 or tm % seg_len == 0)
    assert seg_len & (seg_len - 1) == 0 and seg_len % V7X_SUBLANES == 0
    nseq = n // seg_len
    if seg_len >= tm:
        tiles_per_seg = seg_len // tm
        hist_spec = pl.BlockSpec((1, CONV_K - 1, c), lambda i: (i // tiles_per_seg, 0, 0))
        tail_spec = pl.BlockSpec((1, V7X_SUBLANES, c), lambda i: (i // tiles_per_seg, 0, 0))
    else:
        nseg = tm // seg_len
        hist_spec = pl.BlockSpec((nseg, CONV_K - 1, c), lambda i: (i, 0, 0))
        tail_spec = pl.BlockSpec((nseg, V7X_SUBLANES, c), lambda i: (i, 0, 0))
    row = lambda w: pl.BlockSpec((tm, w), lambda i: (i, 0))
    headrow = pl.BlockSpec((tm * N_HEADS, HEAD_V), lambda i: (i, 0))
    if qv_tile is None:
        qv_shape = jax.ShapeDtypeStruct((n, a), BF16)
        qv_spec = row(a)
    else:
        assert tm % qv_tile == 0
        qv_shape = jax.ShapeDtypeStruct((n // qv_tile, a, qv_tile), BF16)
        qv_spec = pl.BlockSpec((tm // qv_tile, a, qv_tile), lambda i: (i, 0, 0))
    out_shape = (
        qv_shape,
        jax.ShapeDtypeStruct((n * N_HEADS, HEAD_V), F32),
        jax.ShapeDtypeStruct((n * N_HEADS, HEAD_V), F32),
        jax.ShapeDtypeStruct((n, a), BF16),
        qv_shape,
        jax.ShapeDtypeStruct((n, c), BF16),
        jax.ShapeDtypeStruct((nseq, V7X_SUBLANES, c), F32),
    )
    return pl.pallas_call(
        functools.partial(_in_proj_kernel, seg_len=seg_len, tm=tm, qv_tile=qv_tile),
        out_shape=out_shape,
        grid=(n // tm,),
        in_specs=[row(d), _resident((1, d)), _resident((1, d)), _resident((d, e)),
                  _resident((CONV_K, c)), hist_spec],
        out_specs=(qv_spec, headrow, headrow, row(a), qv_spec, row(c), tail_spec),
        scratch_shapes=[pltpu.VMEM((V7X_SUBLANES, c), F32)],
        compiler_params=pltpu.CompilerParams(
            dimension_semantics=("arbitrary",), vmem_limit_bytes=V7X_VMEM_LIMIT_BYTES),
        name="in_proj",
    )(x2d, ln_g.reshape(1, d), ln_b.reshape(1, d), w_in_bf, conv_w, hist)


def _stack_maps(q):
    lane = lax.broadcasted_iota(jnp.int32, q.shape, 1)
    zero = jnp.zeros_like(q)
    return jnp.concatenate([jnp.where(lane < HEAD_QK, q, zero), jnp.where(lane < HEAD_QK, zero, q)], axis=0)


def _scores(qz, kblk):
    return lax.dot_general(qz, kblk, (((1,), (1,)), ((), ())), preferred_element_type=F32)


def _add_to_both_maps(s, bias):
    t, n = bias.shape
    return (s.reshape(2, t, n) + bias[None]).reshape(2 * t, n)


def _lambda(lamv_ref, lam_init):
    lv = lamv_ref[...]
    s1 = jnp.sum(lv[0:1, :] * lv[1:2, :], axis=-1, keepdims=True)
    s2 = jnp.sum(lv[2:3, :] * lv[3:4, :], axis=-1, keepdims=True)
    return jnp.exp(s1) - jnp.exp(s2) + lam_init


def _diff_combine(acc, l, lam, g, lam_init, t):
    o = acc[:t] / l[:t] - lam * (acc[t:] / l[t:])
    ms = jnp.mean(o * o, axis=-1, keepdims=True)
    return o * lax.rsqrt(ms + SUBLN_EPS) * g * (1.0 - lam_init)


def _rel_bucket(rel):
    half = N_BUCKETS // 2
    max_exact = half // 2
    ret = jnp.where(rel > 0, half, 0)
    n = jnp.abs(rel)
    nf = jnp.maximum(n, 1).astype(F32)
    large = max_exact + (jnp.log(nf / max_exact) / math.log(MAX_DISTANCE / max_exact)
                         * (half - max_exact)).astype(jnp.int32)
    large = jnp.minimum(large, half - 1)
    return ret + jnp.where(n < max_exact, n, large)


def _bias_band(rel_bias, n_q, n_k, rel00):
    length = n_q + n_k
    rel = rel00 - (n_q - 1) + jnp.arange(length, dtype=jnp.int32)
    far = rel_bias[N_BUCKETS // 2 - 1].astype(F32)
    w = (rel_bias[_rel_bucket(rel)].astype(F32) - far[None, :]).T
    skew = jnp.tile(w, (1, n_q))[:, :n_q * (length - 1)].reshape(-1, n_q, length - 1)
    return skew[:, :, n_q - 1:n_q - 1 + n_k]


def _chunk_mask(tile, q_pos, k_pos):
    visible = (k_pos[None, :] // CHUNK) <= (q_pos[:, None] // CHUNK)
    return jnp.where(visible[None], tile, -jnp.inf)


SUM_LO = 2.0 ** -60
SUM_HI = 2.0 ** 60


def _prompt_attn_kernel(lamv_ref, g_ref, q_ref, k_ref, v_ref, bd_ref, bp_ref, o_ref,
                        acc_ref, lp_ref, l_ref, m_ref, *, tq, tk, hp, lam_init):
    sub = tk // tq
    qt = pl.program_id(2)
    feats = [slice(h * HEAD_V, (h + 1) * HEAD_V) for h in range(hp)]

    def stack_maps(qT):
        row = lax.broadcasted_iota(jnp.int32, qT.shape, 0)
        zero = jnp.zeros_like(qT)
        return jnp.concatenate([jnp.where(row < HEAD_QK, qT, zero), jnp.where(row < HEAD_QK, zero, qT)], axis=1)

    qz = [stack_maps(q_ref[0, 0, ft, :]) for ft in feats]

    def kv(j, h):
        start = pl.multiple_of(j * tq, tq)
        return k_ref[0, pl.ds(start, tq), feats[h]], v_ref[0, j, feats[h], :]

    def logits(h, kj, bias):
        s = jnp.dot(kj, qz[h], preferred_element_type=F32)
        if bias is not None:
            s = s + jnp.concatenate([bias, bias], axis=1)
        return s

    def sublane_fold(p):
        return jnp.sum(p.reshape(p.shape[0] // V7X_SUBLANES, V7X_SUBLANES, p.shape[1]), axis=0)

    def numerators(chains, lag, first=False):
        totals = {}
        pending = []

        def values(h, vj, p):
            pv = jnp.dot(vj, p.astype(BF16), preferred_element_type=F32)
            lp = sublane_fold(p)
            totals[h] = (pv, lp) if h not in totals else (totals[h][0] + pv, totals[h][1] + lp)

        for h, j, bias in chains:
            kj, vj = kv(j, h)
            pending.append((h, vj, jnp.exp(logits(h, kj, bias))))
            if len(pending) > lag:
                values(*pending.pop(0))
        while pending:
            values(*pending.pop(0))
        for h, (pv, lp) in totals.items():
            if first:
                acc_ref[h] = pv
                lp_ref[h] = lp
            else:
                acc_ref[h] += pv
                lp_ref[h] += lp

    heads = range(hp)
    numerators([(h, qt, bd_ref[h]) for h in heads], hp, first=True)

    @pl.when(qt > 0)
    def _():
        numerators([(h, qt - 1, bp_ref[h]) for h in heads], hp)

    n_far = jnp.maximum(qt - 1, 0)
    n_full = n_far // sub

    def single(j, carry):
        numerators([(h, j, None) for h in heads], hp)
        return carry

    lax.fori_loop(n_full * sub, n_far, single, 0)

    def full(jb, carry):
        numerators([(h, jb * sub + c, None) for h in heads for c in range(sub)], sub)
        return carry

    lax.fori_loop(0, n_full, full, 0)

    for h in range(hp):
        l_ref[h] = jnp.sum(lp_ref[h], axis=0, keepdims=True)
    l_all = l_ref[...]
    in_range = (l_all >= SUM_LO) & (l_all <= SUM_HI)

    @pl.when(jnp.min(jnp.where(in_range, 1.0, 0.0)) < 0.5)
    def _():
        for h in range(hp):
            kd, vd = kv(qt, h)
            s = logits(h, kd, bd_ref[h])
            m0 = jnp.max(s, axis=0, keepdims=True)
            p = jnp.exp(s - m0)
            m_ref[h] = m0
            l_ref[h] = jnp.sum(p, axis=0, keepdims=True)
            acc_ref[h] = jnp.dot(vd, p.astype(BF16), preferred_element_type=F32)

            def step(j, carry, h=h):
                kj, vj = kv(j, h)
                s = logits(h, kj, jnp.where(j == qt - 1, bp_ref[h], jnp.zeros_like(bp_ref[h])))
                m_prev = m_ref[h]
                m_new = jnp.maximum(m_prev, jnp.max(s, axis=0, keepdims=True))
                alpha = jnp.exp(m_prev - m_new)
                p = jnp.exp(s - m_new)
                l_ref[h] = alpha * l_ref[h] + jnp.sum(p, axis=0, keepdims=True)
                acc_ref[h] = alpha * acc_ref[h] + jnp.dot(vj, p.astype(BF16), preferred_element_type=F32)
                m_ref[h] = m_new
                return carry

            lax.fori_loop(0, qt, step, 0)

    lam = _lambda(lamv_ref, lam_init)
    for h in range(hp):
        o = acc_ref[h] / l_ref[h]
        o = o[:, :tq] - lam * o[:, tq:]
        ms = jnp.mean(o * o, axis=0, keepdims=True)
        o = o * lax.rsqrt(ms + SUBLN_EPS) * g_ref[...] * (1.0 - lam_init)
        o_ref[0, :, feats[h]] = o.T.astype(o_ref.dtype)


def _prompt_attention(qT, kb, vT, rel_bias, lamv, subln_g, lam_init):
    b, t, _ = kb.shape
    tq = qT.shape[-1]
    tk = min(K_BLOCK, t)
    sub = tk // tq
    assert t % tq == 0 and tk % tq == 0 and tq % CHUNK == 0 and tq >= MAX_DISTANCE
    pos = jnp.arange(tq, dtype=jnp.int32)
    band = _bias_band(rel_bias, tq, 2 * tq, -tq)
    prev = jnp.swapaxes(band[:, :, :tq], 1, 2)
    diag = jnp.swapaxes(_chunk_mask(band[:, :, tq:], pos, pos), 1, 2)
    gain = jnp.broadcast_to(subln_g.astype(F32)[:, None], (HEAD_V, tq))
    hp = HEADS_PER_STEP
    assert N_HEADS % hp == 0
    qspec = pl.BlockSpec((1, 1, hp * HEAD_V, tq), lambda bi, g, i: (bi, i, g, 0))
    kspec = pl.BlockSpec((1, t, hp * HEAD_V), lambda bi, g, i: (bi, 0, g), pipeline_mode=pl.Buffered(1))
    vspec = pl.BlockSpec((1, t // tq, hp * HEAD_V, tq), lambda bi, g, i: (bi, 0, g, 0),
                         pipeline_mode=pl.Buffered(1))
    bspec = pl.BlockSpec((hp, tq, tq), lambda bi, g, i: (g, 0, 0))
    return pl.pallas_call(
        functools.partial(_prompt_attn_kernel, tq=tq, tk=tk, hp=hp, lam_init=lam_init),
        out_shape=jax.ShapeDtypeStruct(kb.shape, BF16),
        grid=(b, N_HEADS // hp, t // tq),
        in_specs=[_resident(lamv.shape), _resident((HEAD_V, tq)), qspec, kspec, vspec, bspec, bspec],
        out_specs=pl.BlockSpec((1, tq, hp * HEAD_V), lambda bi, g, i: (bi, i, g)),
        scratch_shapes=[pltpu.VMEM((hp, HEAD_V, 2 * tq), F32), pltpu.VMEM((hp, V7X_SUBLANES, 2 * tq), F32),
                        pltpu.VMEM((hp, 1, 2 * tq), F32), pltpu.VMEM((hp, 1, 2 * tq), F32)],
        compiler_params=pltpu.CompilerParams(
            dimension_semantics=("arbitrary", "arbitrary", "arbitrary"),
            vmem_limit_bytes=V7X_VMEM_LIMIT_BYTES),
        name="prompt_attention",
    )(lamv, gain, qT, kb, vT, diag, prev)


def _sample_attn_kernel(lamv_ref, g_ref, q_ref, ck_ref, cv_ref, kn_ref, vn_ref, bias_ref, o_ref,
                        *, ts, past, near, lam_init):
    far = past - near
    lam = _lambda(lamv_ref, lam_init)
    for h in range(N_HEADS):
        lanes = slice(h * HEAD_V, (h + 1) * HEAD_V)
        qz = _stack_maps(q_ref[0, :, lanes])
        ck = ck_ref[0, pl.ds(h, past, stride=N_HEADS), :].astype(BF16)
        cv = cv_ref[0, pl.ds(h, past, stride=N_HEADS), :].astype(BF16)
        bias = bias_ref[h]
        s_far = _scores(qz, ck[:far])
        s_near = _add_to_both_maps(_scores(qz, ck[far:]), bias[:, :near])
        s_new = _add_to_both_maps(_scores(qz, kn_ref[0, :, lanes]), bias[:, near:])
        m = jnp.maximum(jnp.max(s_far, axis=-1, keepdims=True),
                        jnp.maximum(jnp.max(s_near, axis=-1, keepdims=True),
                                    jnp.max(s_new, axis=-1, keepdims=True)))
        p_far = jnp.exp(s_far - m)
        p_near = jnp.exp(s_near - m)
        p_new = jnp.exp(s_new - m)
        l = (jnp.sum(p_far, axis=-1, keepdims=True) + jnp.sum(p_near, axis=-1, keepdims=True)
             + jnp.sum(p_new, axis=-1, keepdims=True))
        acc = (jnp.dot(p_far.astype(BF16), cv[:far], preferred_element_type=F32)
               + jnp.dot(p_near.astype(BF16), cv[far:], preferred_element_type=F32)
               + jnp.dot(p_new.astype(BF16), vn_ref[0, :, lanes], preferred_element_type=F32))
        o_ref[0, :, lanes] = _diff_combine(acc, l, lam, g_ref[...], lam_init, ts).astype(o_ref.dtype)


def _sample_attention(q, kb, vb, cache_k, cache_v, rel_bias, lamv, subln_g, lam_init):
    s, ts, _ = q.shape
    past = cache_k.shape[1] // N_HEADS
    near = MAX_DISTANCE
    assert past >= near and past % V7X_SUBLANES == 0
    q_pos = past + jnp.arange(ts, dtype=jnp.int32)
    k_pos = (past - near) + jnp.arange(near + ts, dtype=jnp.int32)
    bias = _chunk_mask(_bias_band(rel_bias, ts, near + ts, -near), q_pos, k_pos)
    new = pl.BlockSpec((1, ts, ATTN_WIDTH), lambda si: (si, 0, 0))
    cache = pl.BlockSpec((1, past * N_HEADS, HEAD_V), lambda si: (si, 0, 0))
    return pl.pallas_call(
        functools.partial(_sample_attn_kernel, ts=ts, past=past, near=near, lam_init=lam_init),
        out_shape=jax.ShapeDtypeStruct(q.shape, BF16),
        grid=(s,),
        in_specs=[_resident(lamv.shape), _resident((1, HEAD_V)), new, cache, cache, new, new,
                  _resident(bias.shape)],
        out_specs=new,
        compiler_params=pltpu.CompilerParams(
            dimension_semantics=("arbitrary",), vmem_limit_bytes=V7X_VMEM_LIMIT_BYTES),
        name="sample_attention",
    )(lamv, subln_g.reshape(1, HEAD_V), q, cache_k, cache_v, kb, vb, bias)


def _finish_kernel(x_ref, a_ref, c_ref, g0_ref, b0_ref, wo_ref, g1_ref, b1_ref, w1_ref, w2_ref,
                   g2_ref, b2_ref, o_ref, *, alpha):
    xn = _layer_norm(x_ref[...], g0_ref[...], b0_ref[...])
    mix = jnp.concatenate([a_ref[...], c_ref[...]], axis=-1)
    y = jnp.dot(mix, wo_ref[...], preferred_element_type=F32)
    x1 = _layer_norm(alpha * xn + y, g1_ref[...], b1_ref[...])
    h = jnp.dot(x1.astype(BF16), w1_ref[...], preferred_element_type=F32)
    h = jnp.square(jnp.maximum(h, 0.0)).astype(BF16)
    f = jnp.dot(h, w2_ref[...], preferred_element_type=F32)
    o_ref[...] = _layer_norm(alpha * x1 + f, g2_ref[...], b2_ref[...])


def _finish(x2d, attn, conv, ln0_g, ln0_b, w_out_bf, ln1_g, ln1_b, w1_bf, w2_bf, ln2_g, ln2_b, alpha):
    n, d = x2d.shape
    a = attn.shape[1]
    c = conv.shape[1]
    f = w1_bf.shape[1]
    tm = min(ROW_TILE, n)
    assert n % tm == 0
    row = lambda w: pl.BlockSpec((tm, w), lambda i: (i, 0))
    vec = lambda p: p.reshape(1, d)
    return pl.pallas_call(
        functools.partial(_finish_kernel, alpha=alpha),
        out_shape=jax.ShapeDtypeStruct((n, d), F32),
        grid=(n // tm,),
        in_specs=[row(d), row(a), row(c), _resident((1, d)), _resident((1, d)), _resident((a + c, d)),
                  _resident((1, d)), _resident((1, d)), _resident((d, f)), _resident((f, d)),
                  _resident((1, d)), _resident((1, d))],
        out_specs=row(d),
        compiler_params=pltpu.CompilerParams(
            dimension_semantics=("arbitrary",), vmem_limit_bytes=V7X_VMEM_LIMIT_BYTES),
        name="finish",
    )(x2d, attn, conv, vec(ln0_g), vec(ln0_b), w_out_bf, vec(ln1_g), vec(ln1_b), w1_bf, w2_bf,
      vec(ln2_g), vec(ln2_b))


def kernel(x_prompt, x_sample, cache_k, cache_v, cache_conv, ln0_g, ln0_b, rel_bias, w_in, conv_w,
           lambda_q1, lambda_k1, lambda_q2, lambda_k2, subln_g, w_out, ln1_g, ln1_b,
           w_ff1, w_ff2, ln2_g, ln2_b):
    depth = w_in.shape[0]
    assert depth == 1, "single-layer step"
    layer = 0
    b, t, d = x_prompt.shape
    s, ts, _ = x_sample.shape
    past = cache_k.shape[2]
    c = conv_w.shape[-1]
    alpha = (2.0 * depth) ** 0.25
    lam_init = _lambda_init(layer)

    w_in_bf = w_in[layer].astype(BF16)
    w_out_bf = w_out[layer].astype(BF16)
    w1_bf = w_ff1[layer].astype(BF16)
    w2_bf = w_ff2[layer].astype(BF16)
    lamv = jnp.stack([lambda_q1[layer], lambda_k1[layer], lambda_q2[layer], lambda_k2[layer]]).astype(F32)
    ffn = (ln0_g, ln0_b, w_out_bf, ln1_g[layer], ln1_b[layer], w1_bf, w2_bf, ln2_g[layer], ln2_b[layer], alpha)

    xp2 = x_prompt.reshape(b * t, d)
    hist0 = jnp.zeros((b, CONV_K - 1, c), F32)
    tq = min(Q_TILE, t)
    qp, kp, vp, kpb, vpb, convp, tailp = _in_proj(xp2, t, ln0_g, ln0_b, w_in_bf, conv_w[layer], hist0,
                                                  qv_tile=tq)
    tiles = (b, t // tq, ATTN_WIDTH, tq)
    attnp = _prompt_attention(qp.reshape(tiles), kpb.reshape(b, t, ATTN_WIDTH), vpb.reshape(tiles), rel_bias,
                              lamv, subln_g[layer], lam_init)
    yp = _finish(xp2, attnp.reshape(b * t, ATTN_WIDTH), convp, *ffn).reshape(b, t, d)

    xs2 = x_sample.reshape(s * ts, d)
    qs, ks, vs, ksb, vsb, convs, tails = _in_proj(xs2, ts, ln0_g, ln0_b, w_in_bf, conv_w[layer],
                                                  cache_conv[layer].astype(F32))
    shs = (s, ts, ATTN_WIDTH)
    attns = _sample_attention(qs.reshape(shs), ksb.reshape(shs), vsb.reshape(shs),
                              cache_k[layer].astype(F32).reshape(s, past * N_HEADS, HEAD_V),
                              cache_v[layer].astype(F32).reshape(s, past * N_HEADS, HEAD_V),
                              rel_bias, lamv, subln_g[layer], lam_init)
    ys = _finish(xs2, attns.reshape(s * ts, ATTN_WIDTH), convs, *ffn).reshape(s, ts, d)

    keep = CONV_K - 1
    return (yp, ys,
            kp.reshape(1, b, t, N_HEADS, HEAD_V), vp.reshape(1, b, t, N_HEADS, HEAD_V),
            tailp[:, V7X_SUBLANES - keep:, :].reshape(1, b, keep, c),
            ks.reshape(1, s, ts, N_HEADS, HEAD_V), vs.reshape(1, s, ts, N_HEADS, HEAD_V),
            tails[:, V7X_SUBLANES - keep:, :].reshape(1, s, keep, c))
```

```python
import functools
import math

import jax
import jax.numpy as jnp
from jax import lax
from jax.experimental import pallas as pl
from jax.experimental.pallas import tpu as pltpu

CHUNK = 64
N_HEADS = 4
HEAD_V = 128
HEAD_QK = HEAD_V // 2
ATTN_WIDTH = N_HEADS * HEAD_V
CONV_K = 3
N_BUCKETS = 32
MAX_DISTANCE = 128
LN_EPS = 1e-5
SUBLN_EPS = 1e-5

V7X_SUBLANES = 8
V7X_VMEM_LIMIT_BYTES = 56 * 1024 * 1024

ROW_TILE = 512
FINISH_CHUNKS = 2
Q_TILE = 256
K_BLOCK = 1024
HEADS_PER_STEP = 4

F32 = jnp.float32
BF16 = jnp.bfloat16


def _lambda_init(layer):
    return 0.8 - 0.6 * math.exp(-0.3 * layer)


def _layer_norm(x, g, b):
    mu = jnp.mean(x, axis=-1, keepdims=True)
    xc = x - mu
    var = jnp.mean(xc * xc, axis=-1, keepdims=True)
    return xc * lax.rsqrt(var + LN_EPS) * g + b


def _resident(shape):
    nd = len(shape)
    return pl.BlockSpec(shape, lambda *_: (0,) * nd, pipeline_mode=pl.Buffered(1))


def _in_proj_kernel(x_ref, g_ref, b_ref, w_ref, cw_ref, hist_ref,
                    q_ref, k_ref, v_ref, kb_ref, vb_ref, conv_ref, utail_ref,
                    carry_ref, *, seg_len, tm, qv_tile):
    a = ATTN_WIDTH
    xn = _layer_norm(x_ref[...], g_ref[...], b_ref[...])
    z = jnp.dot(xn.astype(BF16), w_ref[...], preferred_element_type=F32)
    q = z[:, :a] * (HEAD_QK ** -0.5)
    k = z[:, a:2 * a]
    v = z[:, 2 * a:3 * a]
    for h in range(N_HEADS):
        k_ref[pl.ds(h, tm, stride=N_HEADS), :] = k[:, h * HEAD_V:(h + 1) * HEAD_V]
        v_ref[pl.ds(h, tm, stride=N_HEADS), :] = v[:, h * HEAD_V:(h + 1) * HEAD_V]
    kb_ref[...] = k.astype(BF16)
    if qv_tile is None:
        q_ref[...] = q.astype(BF16)
        vb_ref[...] = v.astype(BF16)
    else:
        for s in range(tm // qv_tile):
            rows_s = slice(s * qv_tile, (s + 1) * qv_tile)
            q_ref[s] = q[rows_s, :].T.astype(BF16)
            vb_ref[s] = v[rows_s, :].T.astype(BF16)
    c = (z.shape[1] - 3 * a) // 3
    gb = z[:, 3 * a:3 * a + c]
    u = z[:, 3 * a + c:3 * a + 2 * c] * z[:, 3 * a + 2 * c:]
    rows = lax.broadcasted_iota(jnp.int32, (tm, 1), 0)
    if seg_len >= tm:
        tiles_per_seg = seg_len // tm
        first = (pl.program_id(0) % tiles_per_seg) == 0
        hist = hist_ref[0]
        prev = carry_ref[...]
        p1 = jnp.where(first, hist[1:2, :], prev[V7X_SUBLANES - 1:, :])
        p0 = jnp.where(first, hist[0:1, :], prev[V7X_SUBLANES - 2:V7X_SUBLANES - 1, :])
        pos = rows
        tail = u[tm - V7X_SUBLANES:, :]
        carry_ref[...] = tail
        utail_ref[0] = tail
    else:
        nseg = tm // seg_len
        hist = hist_ref[...]
        p1 = jnp.broadcast_to(hist[:, 1:2, :], (nseg, seg_len, c)).reshape(tm, c)
        p0 = jnp.broadcast_to(hist[:, 0:1, :], (nseg, seg_len, c)).reshape(tm, c)
        pos = rows & (seg_len - 1)
        utail_ref[...] = u.reshape(nseg, seg_len, c)[:, seg_len - V7X_SUBLANES:, :]
    u1 = jnp.where(pos == 0, p1, pltpu.roll(u, 1, 0))
    u2 = jnp.where(pos == 0, p0, jnp.where(pos == 1, p1, pltpu.roll(u, 2, 0)))
    cw = cw_ref[...]
    conv = gb * (u2 * cw[0:1, :] + u1 * cw[1:2, :] + u * cw[2:3, :])
    conv_ref[...] = conv.astype(BF16)


def _in_proj(x2d, seg_len, ln_g, ln_b, w_in_bf, conv_w, hist, qv_tile=None):
    n, d = x2d.shape
    e = w_in_bf.shape[1]
    a = ATTN_WIDTH
    c = (e - 3 * a) // 3
    tm = min(ROW_TILE, n)
    assert n % tm == 0 and (seg_len % tm == 0 or tm % seg_len == 0)
    assert seg_len & (seg_len - 1) == 0 and seg_len % V7X_SUBLANES == 0
    nseq = n // seg_len
    if seg_len >= tm:
        tiles_per_seg = seg_len // tm
        hist_spec = pl.BlockSpec((1, CONV_K - 1, c), lambda i: (i // tiles_per_seg, 0, 0))
        tail_spec = pl.BlockSpec((1, V7X_SUBLANES, c), lambda i: (i // tiles_per_seg, 0, 0))
    else:
        nseg = tm // seg_len
        hist_spec = pl.BlockSpec((nseg, CONV_K - 1, c), lambda i: (i, 0, 0))
        tail_spec = pl.BlockSpec((nseg, V7X_SUBLANES, c), lambda i: (i, 0, 0))
    row = lambda w: pl.BlockSpec((tm, w), lambda i: (i, 0))
    headrow = pl.BlockSpec((tm * N_HEADS, HEAD_V), lambda i: (i, 0))
    if qv_tile is None:
        qv_shape = jax.ShapeDtypeStruct((n, a), BF16)
        qv_spec = row(a)
    else:
        assert tm % qv_tile == 0
        qv_shape = jax.ShapeDtypeStruct((n // qv_tile, a, qv_tile), BF16)
        qv_spec = pl.BlockSpec((tm // qv_tile, a, qv_tile), lambda i: (i, 0, 0))
    out_shape = (
        qv_shape,
        jax.ShapeDtypeStruct((n * N_HEADS, HEAD_V), F32),
        jax.ShapeDtypeStruct((n * N_HEADS, HEAD_V), F32),
        jax.ShapeDtypeStruct((n, a), BF16),
        qv_shape,
        jax.ShapeDtypeStruct((n, c), BF16),
        jax.ShapeDtypeStruct((nseq, V7X_SUBLANES, c), F32),
    )
    return pl.pallas_call(
        functools.partial(_in_proj_kernel, seg_len=seg_len, tm=tm, qv_tile=qv_tile),
        out_shape=out_shape,
        grid=(n // tm,),
        in_specs=[row(d), _resident((1, d)), _resident((1, d)), _resident((d, e)),
                  _resident((CONV_K, c)), hist_spec],
        out_specs=(qv_spec, headrow, headrow, row(a), qv_spec, row(c), tail_spec),
        scratch_shapes=[pltpu.VMEM((V7X_SUBLANES, c), F32)],
        compiler_params=pltpu.CompilerParams(
            dimension_semantics=("arbitrary",), vmem_limit_bytes=V7X_VMEM_LIMIT_BYTES),
        name="in_proj",
    )(x2d, ln_g.reshape(1, d), ln_b.reshape(1, d), w_in_bf, conv_w, hist)


def _stack_maps(q):
    lane = lax.broadcasted_iota(jnp.int32, q.shape, 1)
    zero = jnp.zeros_like(q)
    return jnp.concatenate([jnp.where(lane < HEAD_QK, q, zero), jnp.where(lane < HEAD_QK, zero, q)], axis=0)


def _scores(qz, kblk):
    return lax.dot_general(qz, kblk, (((1,), (1,)), ((), ())), preferred_element_type=F32)


def _add_to_both_maps(s, bias):
    t, n = bias.shape
    return (s.reshape(2, t, n) + bias[None]).reshape(2 * t, n)


def _lambda(lamv_ref, lam_init):
    lv = lamv_ref[...]
    s1 = jnp.sum(lv[0:1, :] * lv[1:2, :], axis=-1, keepdims=True)
    s2 = jnp.sum(lv[2:3, :] * lv[3:4, :], axis=-1, keepdims=True)
    return jnp.exp(s1) - jnp.exp(s2) + lam_init


def _diff_combine(acc, l, lam, g, lam_init, t):
    o = acc[:t] / l[:t] - lam * (acc[t:] / l[t:])
    ms = jnp.mean(o * o, axis=-1, keepdims=True)
    return o * lax.rsqrt(ms + SUBLN_EPS) * g * (1.0 - lam_init)


def _rel_bucket(rel):
    half = N_BUCKETS // 2
    max_exact = half // 2
    ret = jnp.where(rel > 0, half, 0)
    n = jnp.abs(rel)
    nf = jnp.maximum(n, 1).astype(F32)
    large = max_exact + (jnp.log(nf / max_exact) / math.log(MAX_DISTANCE / max_exact)
                         * (half - max_exact)).astype(jnp.int32)
    large = jnp.minimum(large, half - 1)
    return ret + jnp.where(n < max_exact, n, large)


def _bias_band(rel_bias, n_q, n_k, rel00):
    length = n_q + n_k
    rel = rel00 - (n_q - 1) + jnp.arange(length, dtype=jnp.int32)
    far = rel_bias[N_BUCKETS // 2 - 1].astype(F32)
    w = (rel_bias[_rel_bucket(rel)].astype(F32) - far[None, :]).T
    skew = jnp.tile(w, (1, n_q))[:, :n_q * (length - 1)].reshape(-1, n_q, length - 1)
    return skew[:, :, n_q - 1:n_q - 1 + n_k]


def _chunk_mask(tile, q_pos, k_pos):
    visible = (k_pos[None, :] // CHUNK) <= (q_pos[:, None] // CHUNK)
    return jnp.where(visible[None], tile, -jnp.inf)


SUM_LO = 2.0 ** -60
SUM_HI = 2.0 ** 60


def _prompt_attn_kernel(lamv_ref, g_ref, q_ref, k_ref, v_ref, bd_ref, bp_ref, o_ref,
                        acc_ref, lp_ref, l_ref, m_ref, *, tq, tk, hp, lam_init):
    sub = tk // tq
    qt = pl.program_id(2)
    feats = [slice(h * HEAD_V, (h + 1) * HEAD_V) for h in range(hp)]

    def stack_maps(qT):
        row = lax.broadcasted_iota(jnp.int32, qT.shape, 0)
        zero = jnp.zeros_like(qT)
        return jnp.concatenate([jnp.where(row < HEAD_QK, qT, zero), jnp.where(row < HEAD_QK, zero, qT)], axis=1)

    qz = [stack_maps(q_ref[0, 0, ft, :]) for ft in feats]

    def kv(j, h):
        start = pl.multiple_of(j * tq, tq)
        return k_ref[0, pl.ds(start, tq), feats[h]], v_ref[0, j, feats[h], :]

    def logits(h, kj, bias):
        s = jnp.dot(kj, qz[h], preferred_element_type=F32)
        if bias is not None:
            s = s + jnp.concatenate([bias, bias], axis=1)
        return s

    def sublane_fold(p):
        return jnp.sum(p.reshape(p.shape[0] // V7X_SUBLANES, V7X_SUBLANES, p.shape[1]), axis=0)

    def numerators(chains, lag, first=False):
        totals = {}
        pending = []

        def values(h, vj, p):
            pv = jnp.dot(vj, p.astype(BF16), preferred_element_type=F32)
            lp = sublane_fold(p)
            totals[h] = (pv, lp) if h not in totals else (totals[h][0] + pv, totals[h][1] + lp)

        for h, j, bias in chains:
            kj, vj = kv(j, h)
            pending.append((h, vj, jnp.exp(logits(h, kj, bias))))
            if len(pending) > lag:
                values(*pending.pop(0))
        while pending:
            values(*pending.pop(0))
        for h, (pv, lp) in totals.items():
            if first:
                acc_ref[h] = pv
                lp_ref[h] = lp
            else:
                acc_ref[h] += pv
                lp_ref[h] += lp

    heads = range(hp)
    @pl.when(qt == 0)
    def _():
        numerators([(h, qt, bd_ref[h]) for h in heads], hp, first=True)

    @pl.when(qt > 0)
    def _():
        numerators([(h, qt - d, b_ref[h]) for h in heads for d, b_ref in ((0, bd_ref), (1, bp_ref))],
                   hp, first=True)

    n_far = jnp.maximum(qt - 1, 0)
    n_full = n_far // sub

    def single(j, carry):
        numerators([(h, j, None) for h in heads], hp)
        return carry

    lax.fori_loop(n_full * sub, n_far, single, 0)

    def full(jb, carry):
        numerators([(h, jb * sub + c, None) for h in heads for c in range(sub)], sub)
        return carry

    lax.fori_loop(0, n_full, full, 0)

    for h in range(hp):
        l_ref[h] = jnp.sum(lp_ref[h], axis=0, keepdims=True)
    l_all = l_ref[...]
    in_range = (l_all >= SUM_LO) & (l_all <= SUM_HI)

    @pl.when(jnp.min(jnp.where(in_range, 1.0, 0.0)) < 0.5)
    def _():
        for h in range(hp):
            kd, vd = kv(qt, h)
            s = logits(h, kd, bd_ref[h])
            m0 = jnp.max(s, axis=0, keepdims=True)
            p = jnp.exp(s - m0)
            m_ref[h] = m0
            l_ref[h] = jnp.sum(p, axis=0, keepdims=True)
            acc_ref[h] = jnp.dot(vd, p.astype(BF16), preferred_element_type=F32)

            def step(j, carry, h=h):
                kj, vj = kv(j, h)
                s = logits(h, kj, jnp.where(j == qt - 1, bp_ref[h], jnp.zeros_like(bp_ref[h])))
                m_prev = m_ref[h]
                m_new = jnp.maximum(m_prev, jnp.max(s, axis=0, keepdims=True))
                alpha = jnp.exp(m_prev - m_new)
                p = jnp.exp(s - m_new)
                l_ref[h] = alpha * l_ref[h] + jnp.sum(p, axis=0, keepdims=True)
                acc_ref[h] = alpha * acc_ref[h] + jnp.dot(vj, p.astype(BF16), preferred_element_type=F32)
                m_ref[h] = m_new
                return carry

            lax.fori_loop(0, qt, step, 0)

    lam = _lambda(lamv_ref, lam_init)
    for h in range(hp):
        o = acc_ref[h] / l_ref[h]
        o = o[:, :tq] - lam * o[:, tq:]
        ms = jnp.mean(o * o, axis=0, keepdims=True)
        o = o * lax.rsqrt(ms + SUBLN_EPS) * g_ref[...] * (1.0 - lam_init)
        o_ref[0, :, feats[h]] = o.T.astype(o_ref.dtype)


def _prompt_attention(qT, kb, vT, rel_bias, lamv, subln_g, lam_init):
    b, t, _ = kb.shape
    tq = qT.shape[-1]
    tk = min(K_BLOCK, t)
    sub = tk // tq
    assert t % tq == 0 and tk % tq == 0 and tq % CHUNK == 0 and tq >= MAX_DISTANCE
    pos = jnp.arange(tq, dtype=jnp.int32)
    band = _bias_band(rel_bias, tq, 2 * tq, -tq)
    prev = jnp.swapaxes(band[:, :, :tq], 1, 2)
    diag = jnp.swapaxes(_chunk_mask(band[:, :, tq:], pos, pos), 1, 2)
    gain = jnp.broadcast_to(subln_g.astype(F32)[:, None], (HEAD_V, tq))
    hp = HEADS_PER_STEP
    assert N_HEADS % hp == 0
    qspec = pl.BlockSpec((1, 1, hp * HEAD_V, tq), lambda bi, g, i: (bi, i, g, 0))
    kspec = pl.BlockSpec((1, t, hp * HEAD_V), lambda bi, g, i: (bi, 0, g))
    vspec = pl.BlockSpec((1, t // tq, hp * HEAD_V, tq), lambda bi, g, i: (bi, 0, g, 0))
    bspec = pl.BlockSpec((hp, tq, tq), lambda bi, g, i: (g, 0, 0))
    return pl.pallas_call(
        functools.partial(_prompt_attn_kernel, tq=tq, tk=tk, hp=hp, lam_init=lam_init),
        out_shape=jax.ShapeDtypeStruct(kb.shape, BF16),
        grid=(b, N_HEADS // hp, t // tq),
        in_specs=[_resident(lamv.shape), _resident((HEAD_V, tq)), qspec, kspec, vspec, bspec, bspec],
        out_specs=pl.BlockSpec((1, tq, hp * HEAD_V), lambda bi, g, i: (bi, i, g)),
        scratch_shapes=[pltpu.VMEM((hp, HEAD_V, 2 * tq), F32), pltpu.VMEM((hp, V7X_SUBLANES, 2 * tq), F32),
                        pltpu.VMEM((hp, 1, 2 * tq), F32), pltpu.VMEM((hp, 1, 2 * tq), F32)],
        compiler_params=pltpu.CompilerParams(
            dimension_semantics=("arbitrary", "arbitrary", "arbitrary"),
            vmem_limit_bytes=V7X_VMEM_LIMIT_BYTES),
        name="prompt_attention",
    )(lamv, gain, qT, kb, vT, diag, prev)


def _sample_attn_kernel(lamv_ref, g_ref, q_ref, ck_ref, cv_ref, kn_ref, vn_ref, bias_ref, o_ref,
                        *, ts, past, near, lam_init):
    far = past - near
    lam = _lambda(lamv_ref, lam_init)
    for h in range(N_HEADS):
        lanes = slice(h * HEAD_V, (h + 1) * HEAD_V)
        qz = _stack_maps(q_ref[0, :, lanes])
        ck = ck_ref[0, pl.ds(h, past, stride=N_HEADS), :].astype(BF16)
        cv = cv_ref[0, pl.ds(h, past, stride=N_HEADS), :].astype(BF16)
        bias = bias_ref[h]
        s_far = _scores(qz, ck[:far])
        s_near = _add_to_both_maps(_scores(qz, ck[far:]), bias[:, :near])
        s_new = _add_to_both_maps(_scores(qz, kn_ref[0, :, lanes]), bias[:, near:])
        m = jnp.maximum(jnp.max(s_far, axis=-1, keepdims=True),
                        jnp.maximum(jnp.max(s_near, axis=-1, keepdims=True),
                                    jnp.max(s_new, axis=-1, keepdims=True)))
        p_far = jnp.exp(s_far - m)
        p_near = jnp.exp(s_near - m)
        p_new = jnp.exp(s_new - m)
        l = (jnp.sum(p_far, axis=-1, keepdims=True) + jnp.sum(p_near, axis=-1, keepdims=True)
             + jnp.sum(p_new, axis=-1, keepdims=True))
        acc = (jnp.dot(p_far.astype(BF16), cv[:far], preferred_element_type=F32)
               + jnp.dot(p_near.astype(BF16), cv[far:], preferred_element_type=F32)
               + jnp.dot(p_new.astype(BF16), vn_ref[0, :, lanes], preferred_element_type=F32))
        o_ref[0, :, lanes] = _diff_combine(acc, l, lam, g_ref[...], lam_init, ts).astype(o_ref.dtype)


def _sample_attention(q, kb, vb, cache_k, cache_v, rel_bias, lamv, subln_g, lam_init):
    s, ts, _ = q.shape
    past = cache_k.shape[1] // N_HEADS
    near = MAX_DISTANCE
    assert past >= near and past % V7X_SUBLANES == 0
    q_pos = past + jnp.arange(ts, dtype=jnp.int32)
    k_pos = (past - near) + jnp.arange(near + ts, dtype=jnp.int32)
    bias = _chunk_mask(_bias_band(rel_bias, ts, near + ts, -near), q_pos, k_pos)
    new = pl.BlockSpec((1, ts, ATTN_WIDTH), lambda si: (si, 0, 0))
    cache = pl.BlockSpec((1, past * N_HEADS, HEAD_V), lambda si: (si, 0, 0))
    return pl.pallas_call(
        functools.partial(_sample_attn_kernel, ts=ts, past=past, near=near, lam_init=lam_init),
        out_shape=jax.ShapeDtypeStruct(q.shape, BF16),
        grid=(s,),
        in_specs=[_resident(lamv.shape), _resident((1, HEAD_V)), new, cache, cache, new, new,
                  _resident(bias.shape)],
        out_specs=new,
        compiler_params=pltpu.CompilerParams(
            dimension_semantics=("arbitrary",), vmem_limit_bytes=V7X_VMEM_LIMIT_BYTES),
        name="sample_attention",
    )(lamv, subln_g.reshape(1, HEAD_V), q, cache_k, cache_v, kb, vb, bias)


def _finish_kernel(x_ref, a_ref, c_ref, g0_ref, b0_ref, wo_ref, g1_ref, b1_ref, w1_ref, w2_ref,
                   g2_ref, b2_ref, o_ref, *, alpha, chunks):
    rows = x_ref.shape[0] // chunks
    rs = [pl.ds(c * rows, rows) for c in range(chunks)]
    xn, y, x1, h, f = [], [], [], [], []
    for c in range(chunks):
        xn.append(_layer_norm(x_ref[rs[c], :], g0_ref[...], b0_ref[...]))
        mix = jnp.concatenate([a_ref[rs[c], :], c_ref[rs[c], :]], axis=-1)
        y.append(jnp.dot(mix, wo_ref[...], preferred_element_type=F32))
    for c in range(chunks):
        x1.append(_layer_norm(alpha * xn[c] + y[c], g1_ref[...], b1_ref[...]))
        h.append(jnp.dot(x1[c].astype(BF16), w1_ref[...], preferred_element_type=F32))
    for c in range(chunks):
        hc = jnp.square(jnp.maximum(h[c], 0.0)).astype(BF16)
        f.append(jnp.dot(hc, w2_ref[...], preferred_element_type=F32))
    for c in range(chunks):
        o_ref[rs[c], :] = _layer_norm(alpha * x1[c] + f[c], g2_ref[...], b2_ref[...])


def _finish(x2d, attn, conv, ln0_g, ln0_b, w_out_bf, ln1_g, ln1_b, w1_bf, w2_bf, ln2_g, ln2_b, alpha):
    n, d = x2d.shape
    a = attn.shape[1]
    c = conv.shape[1]
    f = w1_bf.shape[1]
    tm = min(ROW_TILE, n)
    assert n % tm == 0
    row = lambda w: pl.BlockSpec((tm, w), lambda i: (i, 0))
    vec = lambda p: p.reshape(1, d)
    return pl.pallas_call(
        functools.partial(_finish_kernel, alpha=alpha, chunks=FINISH_CHUNKS),
        out_shape=jax.ShapeDtypeStruct((n, d), F32),
        grid=(n // tm,),
        in_specs=[row(d), row(a), row(c), _resident((1, d)), _resident((1, d)), _resident((a + c, d)),
                  _resident((1, d)), _resident((1, d)), _resident((d, f)), _resident((f, d)),
                  _resident((1, d)), _resident((1, d))],
        out_specs=row(d),
        compiler_params=pltpu.CompilerParams(
            dimension_semantics=("arbitrary",), vmem_limit_bytes=V7X_VMEM_LIMIT_BYTES),
        name="finish",
    )(x2d, attn, conv, vec(ln0_g), vec(ln0_b), w_out_bf, vec(ln1_g), vec(ln1_b), w1_bf, w2_bf,
      vec(ln2_g), vec(ln2_b))


def kernel(x_prompt, x_sample, cache_k, cache_v, cache_conv, ln0_g, ln0_b, rel_bias, w_in, conv_w,
           lambda_q1, lambda_k1, lambda_q2, lambda_k2, subln_g, w_out, ln1_g, ln1_b,
           w_ff1, w_ff2, ln2_g, ln2_b):
    depth = w_in.shape[0]
    assert depth == 1, "single-layer step"
    layer = 0
    b, t, d = x_prompt.shape
    s, ts, _ = x_sample.shape
    past = cache_k.shape[2]
    c = conv_w.shape[-1]
    alpha = (2.0 * depth) ** 0.25
    lam_init = _lambda_init(layer)

    w_in_bf = w_in[layer].astype(BF16)
    w_out_bf = w_out[layer].astype(BF16)
    w1_bf = w_ff1[layer].astype(BF16)
    w2_bf = w_ff2[layer].astype(BF16)
    lamv = jnp.stack([lambda_q1[layer], lambda_k1[layer], lambda_q2[layer], lambda_k2[layer]]).astype(F32)
    ffn = (ln0_g, ln0_b, w_out_bf, ln1_g[layer], ln1_b[layer], w1_bf, w2_bf, ln2_g[layer], ln2_b[layer], alpha)

    xp2 = x_prompt.reshape(b * t, d)
    hist0 = jnp.zeros((b, CONV_K - 1, c), F32)
    tq = min(Q_TILE, t)
    qp, kp, vp, kpb, vpb, convp, tailp = _in_proj(xp2, t, ln0_g, ln0_b, w_in_bf, conv_w[layer], hist0,
                                                  qv_tile=tq)
    tiles = (b, t // tq, ATTN_WIDTH, tq)
    attnp = _prompt_attention(qp.reshape(tiles), kpb.reshape(b, t, ATTN_WIDTH), vpb.reshape(tiles), rel_bias,
                              lamv, subln_g[layer], lam_init)
    yp = _finish(xp2, attnp.reshape(b * t, ATTN_WIDTH), convp, *ffn).reshape(b, t, d)

    xs2 = x_sample.reshape(s * ts, d)
    qs, ks, vs, ksb, vsb, convs, tails = _in_proj(xs2, ts, ln0_g, ln0_b, w_in_bf, conv_w[layer],
                                                  cache_conv[layer].astype(F32))
    shs = (s, ts, ATTN_WIDTH)
    attns = _sample_attention(qs.reshape(shs), ksb.reshape(shs), vsb.reshape(shs),
                              cache_k[layer].astype(F32).reshape(s, past * N_HEADS, HEAD_V),
                              cache_v[layer].astype(F32).reshape(s, past * N_HEADS, HEAD_V),
                              rel_bias, lamv, subln_g[layer], lam_init)
    ys = _finish(xs2, attns.reshape(s * ts, ATTN_WIDTH), convs, *ffn).reshape(s, ts, d)

    keep = CONV_K - 1
    return (yp, ys,
            kp.reshape(1, b, t, N_HEADS, HEAD_V), vp.reshape(1, b, t, N_HEADS, HEAD_V),
            tailp[:, V7X_SUBLANES - keep:, :].reshape(1, b, keep, c),
            ks.reshape(1, s, ts, N_HEADS, HEAD_V), vs.reshape(1, s, ts, N_HEADS, HEAD_V),
            tails[:, V7X_SUBLANES - keep:, :].reshape(1, s, keep, c))
```

```python
import functools
import math

import jax
import jax.numpy as jnp
from jax import lax
from jax.experimental import pallas as pl
from jax.experimental.pallas import tpu as pltpu

CHUNK = 64
N_HEADS = 4
HEAD_V = 128
HEAD_QK = HEAD_V // 2
ATTN_WIDTH = N_HEADS * HEAD_V
CONV_K = 3
N_BUCKETS = 32
MAX_DISTANCE = 128
LN_EPS = 1e-5
SUBLN_EPS = 1e-5

V7X_SUBLANES = 8
V7X_BF16_SUBLANES = 16
V7X_VMEM_LIMIT_BYTES = 56 * 1024 * 1024

ROW_TILE = 512
FINISH_CHUNKS = 2
Q_TILE = 256
K_BLOCK = 1024
HEADS_PER_STEP = 4

F32 = jnp.float32
BF16 = jnp.bfloat16


def _lambda_init(layer):
    return 0.8 - 0.6 * math.exp(-0.3 * layer)


def _layer_norm(x, g, b):
    mu = jnp.mean(x, axis=-1, keepdims=True)
    xc = x - mu
    var = jnp.mean(xc * xc, axis=-1, keepdims=True)
    return xc * lax.rsqrt(var + LN_EPS) * g + b


def _resident(shape):
    nd = len(shape)
    return pl.BlockSpec(shape, lambda *_: (0,) * nd, pipeline_mode=pl.Buffered(1))


def _in_proj_kernel(x_ref, g_ref, b_ref, w_ref, cw_ref, hist_ref, *rest, seg_len, tm, qv_tile, n_cast):
    cast_in, rest = rest[:n_cast], rest[n_cast:]
    q_ref, k_ref, v_ref, kb_ref, vb_ref, conv_ref, utail_ref = rest[:7]
    cast_out, (carry_ref,) = rest[7:7 + n_cast], rest[7 + n_cast:]
    a = ATTN_WIDTH
    c = (w_ref.shape[1] - 3 * a) // 3
    for src, dst in zip(cast_in, cast_out):
        dst[...] = src[...].astype(BF16)
    xn = _layer_norm(x_ref[...], g_ref[...], b_ref[...])
    z = jnp.dot(xn, w_ref[...], preferred_element_type=F32)
    gb = z[:, 3 * a:3 * a + c]
    u = z[:, 3 * a + c:3 * a + 2 * c] * z[:, 3 * a + 2 * c:]
    q = z[:, :a] * (HEAD_QK ** -0.5)
    k = z[:, a:2 * a]
    v = z[:, 2 * a:3 * a]
    for h in range(N_HEADS):
        k_ref[pl.ds(h, tm, stride=N_HEADS), :] = k[:, h * HEAD_V:(h + 1) * HEAD_V]
        v_ref[pl.ds(h, tm, stride=N_HEADS), :] = v[:, h * HEAD_V:(h + 1) * HEAD_V]
    kb_ref[...] = k.astype(BF16)
    if qv_tile is None:
        q_ref[...] = q.astype(BF16)
        vb_ref[...] = v.astype(BF16)
    else:
        for s in range(tm // qv_tile):
            rows_s = slice(s * qv_tile, (s + 1) * qv_tile)
            q_ref[s] = q[rows_s, :].T.astype(BF16)
            vb_ref[s] = v[rows_s, :].T.astype(BF16)
    rows = lax.broadcasted_iota(jnp.int32, (tm, 1), 0)
    if seg_len >= tm:
        tiles_per_seg = seg_len // tm
        first = (pl.program_id(0) % tiles_per_seg) == 0
        hist = hist_ref[0]
        prev = carry_ref[...]
        p1 = jnp.where(first, hist[1:2, :], prev[V7X_SUBLANES - 1:, :])
        p0 = jnp.where(first, hist[0:1, :], prev[V7X_SUBLANES - 2:V7X_SUBLANES - 1, :])
        pos = rows
        tail = u[tm - V7X_SUBLANES:, :]
        carry_ref[...] = tail
        utail_ref[0] = tail
    else:
        nseg = tm // seg_len
        hist = hist_ref[...]
        p1 = jnp.broadcast_to(hist[:, 1:2, :], (nseg, seg_len, c)).reshape(tm, c)
        p0 = jnp.broadcast_to(hist[:, 0:1, :], (nseg, seg_len, c)).reshape(tm, c)
        pos = rows & (seg_len - 1)
        utail_ref[...] = u.reshape(nseg, seg_len, c)[:, seg_len - V7X_SUBLANES:, :]
    u1 = jnp.where(pos == 0, p1, pltpu.roll(u, 1, 0))
    u2 = jnp.where(pos == 0, p0, jnp.where(pos == 1, p1, pltpu.roll(u, 2, 0)))
    cw = cw_ref[...]
    conv = gb * (u2 * cw[0:1, :] + u1 * cw[1:2, :] + u * cw[2:3, :])
    conv_ref[...] = conv.astype(BF16)


def _in_proj(x2d, seg_len, ln_g, ln_b, w_in, conv_w, hist, qv_tile=None, cast=()):
    n, d = x2d.shape
    e = w_in.shape[1]
    a = ATTN_WIDTH
    c = (e - 3 * a) // 3
    tm = min(ROW_TILE, n)
    assert n % tm == 0 and (seg_len % tm == 0 or tm % seg_len == 0)
    assert seg_len & (seg_len - 1) == 0 and seg_len % V7X_SUBLANES == 0
    nseq = n // seg_len
    if seg_len >= tm:
        tiles_per_seg = seg_len // tm
        hist_spec = pl.BlockSpec((1, CONV_K - 1, c), lambda i: (i // tiles_per_seg, 0, 0))
        tail_spec = pl.BlockSpec((1, V7X_SUBLANES, c), lambda i: (i // tiles_per_seg, 0, 0))
    else:
        nseg = tm // seg_len
        hist_spec = pl.BlockSpec((nseg, CONV_K - 1, c), lambda i: (i, 0, 0))
        tail_spec = pl.BlockSpec((nseg, V7X_SUBLANES, c), lambda i: (i, 0, 0))
    row = lambda w: pl.BlockSpec((tm, w), lambda i: (i, 0))
    headrow = pl.BlockSpec((tm * N_HEADS, HEAD_V), lambda i: (i, 0))
    if qv_tile is None:
        qv_shape = jax.ShapeDtypeStruct((n, a), BF16)
        qv_spec = row(a)
    else:
        assert tm % qv_tile == 0
        qv_shape = jax.ShapeDtypeStruct((n // qv_tile, a, qv_tile), BF16)
        qv_spec = pl.BlockSpec((tm // qv_tile, a, qv_tile), lambda i: (i, 0, 0))
    out_shape = (
        qv_shape,
        jax.ShapeDtypeStruct((n * N_HEADS, HEAD_V), F32),
        jax.ShapeDtypeStruct((n * N_HEADS, HEAD_V), F32),
        jax.ShapeDtypeStruct((n, a), BF16),
        qv_shape,
        jax.ShapeDtypeStruct((n, c), BF16),
        jax.ShapeDtypeStruct((nseq, V7X_SUBLANES, c), F32),
    ) + tuple(jax.ShapeDtypeStruct(m.shape, BF16) for m in cast)
    steps = n // tm
    slab_specs = []
    for m in cast:
        assert m.shape[0] % (steps * V7X_BF16_SUBLANES) == 0
        slab_specs.append(pl.BlockSpec((m.shape[0] // steps, m.shape[1]), lambda i: (i, 0)))
    return pl.pallas_call(
        functools.partial(_in_proj_kernel, seg_len=seg_len, tm=tm, qv_tile=qv_tile, n_cast=len(cast)),
        out_shape=out_shape,
        grid=(steps,),
        in_specs=[row(d), _resident((1, d)), _resident((1, d)), _resident((d, e)),
                  _resident((CONV_K, c)), hist_spec, *slab_specs],
        out_specs=(qv_spec, headrow, headrow, row(a), qv_spec, row(c), tail_spec, *slab_specs),
        scratch_shapes=[pltpu.VMEM((V7X_SUBLANES, c), F32)],
        compiler_params=pltpu.CompilerParams(
            dimension_semantics=("arbitrary",), vmem_limit_bytes=V7X_VMEM_LIMIT_BYTES),
        name="in_proj",
    )(x2d, ln_g.reshape(1, d), ln_b.reshape(1, d), w_in, conv_w, hist, *cast)


def _stack_maps(q):
    lane = lax.broadcasted_iota(jnp.int32, q.shape, 1)
    zero = jnp.zeros_like(q)
    return jnp.concatenate([jnp.where(lane < HEAD_QK, q, zero), jnp.where(lane < HEAD_QK, zero, q)], axis=0)


def _scores(qz, kblk):
    return lax.dot_general(qz, kblk, (((1,), (1,)), ((), ())), preferred_element_type=F32)


def _add_to_both_maps(s, bias):
    t, n = bias.shape
    return (s.reshape(2, t, n) + bias[None]).reshape(2 * t, n)


def _lambda(lamv_ref, lam_init):
    lv = lamv_ref[...]
    s1 = jnp.sum(lv[0:1, :] * lv[1:2, :], axis=-1, keepdims=True)
    s2 = jnp.sum(lv[2:3, :] * lv[3:4, :], axis=-1, keepdims=True)
    return jnp.exp(s1) - jnp.exp(s2) + lam_init


def _diff_combine(acc, l, lam, g, lam_init, t):
    o = acc[:t] / l[:t] - lam * (acc[t:] / l[t:])
    ms = jnp.mean(o * o, axis=-1, keepdims=True)
    return o * lax.rsqrt(ms + SUBLN_EPS) * g * (1.0 - lam_init)


def _rel_bucket(rel):
    half = N_BUCKETS // 2
    max_exact = half // 2
    ret = jnp.where(rel > 0, half, 0)
    n = jnp.abs(rel)
    nf = jnp.maximum(n, 1).astype(F32)
    large = max_exact + (jnp.log(nf / max_exact) / math.log(MAX_DISTANCE / max_exact)
                         * (half - max_exact)).astype(jnp.int32)
    large = jnp.minimum(large, half - 1)
    return ret + jnp.where(n < max_exact, n, large)


def _bias_band(rel_bias, n_q, n_k, rel00):
    length = n_q + n_k
    rel = rel00 - (n_q - 1) + jnp.arange(length, dtype=jnp.int32)
    far = rel_bias[N_BUCKETS // 2 - 1].astype(F32)
    w = (rel_bias[_rel_bucket(rel)].astype(F32) - far[None, :]).T
    skew = jnp.tile(w, (1, n_q))[:, :n_q * (length - 1)].reshape(-1, n_q, length - 1)
    return skew[:, :, n_q - 1:n_q - 1 + n_k]


def _chunk_mask(tile, q_pos, k_pos):
    visible = (k_pos[None, :] // CHUNK) <= (q_pos[:, None] // CHUNK)
    return jnp.where(visible[None], tile, -jnp.inf)


SUM_LO = 2.0 ** -60
SUM_HI = 2.0 ** 60


def _prompt_attn_kernel(lamv_ref, g_ref, q_ref, k_ref, v_ref, bd_ref, bp_ref, o_ref,
                        acc_ref, lp_ref, l_ref, m_ref, *, tq, tk, hp, lam_init):
    sub = tk // tq
    qt = pl.program_id(2)
    feats = [slice(h * HEAD_V, (h + 1) * HEAD_V) for h in range(hp)]

    def stack_maps(qT):
        row = lax.broadcasted_iota(jnp.int32, qT.shape, 0)
        zero = jnp.zeros_like(qT)
        return jnp.concatenate([jnp.where(row < HEAD_QK, qT, zero), jnp.where(row < HEAD_QK, zero, qT)], axis=1)

    qz = [stack_maps(q_ref[0, 0, ft, :]) for ft in feats]

    def kv(j, h):
        start = pl.multiple_of(j * tq, tq)
        return k_ref[0, pl.ds(start, tq), feats[h]], v_ref[0, j, feats[h], :]

    def logits(h, kj, bias):
        s = jnp.dot(kj, qz[h], preferred_element_type=F32)
        if bias is not None:
            s = s + jnp.concatenate([bias, bias], axis=1)
        return s

    def sublane_fold(p):
        return jnp.sum(p.reshape(p.shape[0] // V7X_SUBLANES, V7X_SUBLANES, p.shape[1]), axis=0)

    def numerators(chains, lag, first=False):
        totals = {}
        pending = []

        def values(h, vj, p):
            pv = jnp.dot(vj, p.astype(BF16), preferred_element_type=F32)
            lp = sublane_fold(p)
            totals[h] = (pv, lp) if h not in totals else (totals[h][0] + pv, totals[h][1] + lp)

        for h, j, bias in chains:
            kj, vj = kv(j, h)
            pending.append((h, vj, jnp.exp(logits(h, kj, bias))))
            if len(pending) > lag:
                values(*pending.pop(0))
        while pending:
            values(*pending.pop(0))
        for h, (pv, lp) in totals.items():
            if first:
                acc_ref[h] = pv
                lp_ref[h] = lp
            else:
                acc_ref[h] += pv
                lp_ref[h] += lp

    heads = range(hp)
    @pl.when(qt == 0)
    def _():
        numerators([(h, qt, bd_ref[h]) for h in heads], hp, first=True)

    @pl.when(qt > 0)
    def _():
        numerators([(h, qt - d, b_ref[h]) for h in heads for d, b_ref in ((0, bd_ref), (1, bp_ref))],
                   hp, first=True)

    n_far = jnp.maximum(qt - 1, 0)
    n_full = n_far // sub

    def single(j, carry):
        numerators([(h, j, None) for h in heads], hp)
        return carry

    lax.fori_loop(n_full * sub, n_far, single, 0)

    def full(jb, carry):
        numerators([(h, jb * sub + c, None) for h in heads for c in range(sub)], 2 * sub)
        return carry

    lax.fori_loop(0, n_full, full, 0)

    for h in range(hp):
        l_ref[h] = jnp.sum(lp_ref[h], axis=0, keepdims=True)
    l_all = l_ref[...]
    in_range = (l_all >= SUM_LO) & (l_all <= SUM_HI)

    @pl.when(jnp.min(jnp.where(in_range, 1.0, 0.0)) < 0.5)
    def _():
        for h in range(hp):
            kd, vd = kv(qt, h)
            s = logits(h, kd, bd_ref[h])
            m0 = jnp.max(s, axis=0, keepdims=True)
            p = jnp.exp(s - m0)
            m_ref[h] = m0
            l_ref[h] = jnp.sum(p, axis=0, keepdims=True)
            acc_ref[h] = jnp.dot(vd, p.astype(BF16), preferred_element_type=F32)

            def step(j, carry, h=h):
                kj, vj = kv(j, h)
                s = logits(h, kj, jnp.where(j == qt - 1, bp_ref[h], jnp.zeros_like(bp_ref[h])))
                m_prev = m_ref[h]
                m_new = jnp.maximum(m_prev, jnp.max(s, axis=0, keepdims=True))
                alpha = jnp.exp(m_prev - m_new)
                p = jnp.exp(s - m_new)
                l_ref[h] = alpha * l_ref[h] + jnp.sum(p, axis=0, keepdims=True)
                acc_ref[h] = alpha * acc_ref[h] + jnp.dot(vj, p.astype(BF16), preferred_element_type=F32)
                m_ref[h] = m_new
                return carry

            lax.fori_loop(0, qt, step, 0)

    lam = _lambda(lamv_ref, lam_init)
    for h in range(hp):
        o = acc_ref[h] / l_ref[h]
        o = o[:, :tq] - lam * o[:, tq:]
        ms = jnp.mean(o * o, axis=0, keepdims=True)
        o = o * lax.rsqrt(ms + SUBLN_EPS) * g_ref[...] * (1.0 - lam_init)
        o_ref[0, :, feats[h]] = o.T.astype(o_ref.dtype)


def _prompt_attention(qT, kb, vT, rel_bias, lamv, subln_g, lam_init):
    b, t, _ = kb.shape
    tq = qT.shape[-1]
    tk = min(K_BLOCK, t)
    sub = tk // tq
    assert t % tq == 0 and tk % tq == 0 and tq % CHUNK == 0 and tq >= MAX_DISTANCE
    pos = jnp.arange(tq, dtype=jnp.int32)
    band = _bias_band(rel_bias, tq, 2 * tq, -tq)
    prev = jnp.swapaxes(band[:, :, :tq], 1, 2)
    diag = jnp.swapaxes(_chunk_mask(band[:, :, tq:], pos, pos), 1, 2)
    gain = jnp.broadcast_to(subln_g.astype(F32)[:, None], (HEAD_V, tq))
    hp = HEADS_PER_STEP
    assert N_HEADS % hp == 0
    qspec = pl.BlockSpec((1, 1, hp * HEAD_V, tq), lambda bi, g, i: (bi, i, g, 0))
    kspec = pl.BlockSpec((1, t, hp * HEAD_V), lambda bi, g, i: (bi, 0, g))
    vspec = pl.BlockSpec((1, t // tq, hp * HEAD_V, tq), lambda bi, g, i: (bi, 0, g, 0))
    bspec = pl.BlockSpec((hp, tq, tq), lambda bi, g, i: (g, 0, 0))
    return pl.pallas_call(
        functools.partial(_prompt_attn_kernel, tq=tq, tk=tk, hp=hp, lam_init=lam_init),
        out_shape=jax.ShapeDtypeStruct(kb.shape, BF16),
        grid=(b, N_HEADS // hp, t // tq),
        in_specs=[_resident(lamv.shape), _resident((HEAD_V, tq)), qspec, kspec, vspec, bspec, bspec],
        out_specs=pl.BlockSpec((1, tq, hp * HEAD_V), lambda bi, g, i: (bi, i, g)),
        scratch_shapes=[pltpu.VMEM((hp, HEAD_V, 2 * tq), F32), pltpu.VMEM((hp, V7X_SUBLANES, 2 * tq), F32),
                        pltpu.VMEM((hp, 1, 2 * tq), F32), pltpu.VMEM((hp, 1, 2 * tq), F32)],
        compiler_params=pltpu.CompilerParams(
            dimension_semantics=("arbitrary", "arbitrary", "arbitrary"),
            vmem_limit_bytes=V7X_VMEM_LIMIT_BYTES),
        name="prompt_attention",
    )(lamv, gain, qT, kb, vT, diag, prev)


def _sample_attn_kernel(lamv_ref, g_ref, q_ref, ck_ref, cv_ref, kn_ref, vn_ref, bias_ref, o_ref,
                        *, ts, past, near, lam_init):
    far = past - near
    lam = _lambda(lamv_ref, lam_init)
    for h in range(N_HEADS):
        lanes = slice(h * HEAD_V, (h + 1) * HEAD_V)
        qz = _stack_maps(q_ref[0, :, lanes])
        ck = ck_ref[0, pl.ds(h, past, stride=N_HEADS), :].astype(BF16)
        cv = cv_ref[0, pl.ds(h, past, stride=N_HEADS), :].astype(BF16)
        bias = bias_ref[h]
        s_far = _scores(qz, ck[:far])
        s_near = _add_to_both_maps(_scores(qz, ck[far:]), bias[:, :near])
        s_new = _add_to_both_maps(_scores(qz, kn_ref[0, :, lanes]), bias[:, near:])
        m = jnp.maximum(jnp.max(s_far, axis=-1, keepdims=True),
                        jnp.maximum(jnp.max(s_near, axis=-1, keepdims=True),
                                    jnp.max(s_new, axis=-1, keepdims=True)))
        p_far = jnp.exp(s_far - m)
        p_near = jnp.exp(s_near - m)
        p_new = jnp.exp(s_new - m)
        l = (jnp.sum(p_far, axis=-1, keepdims=True) + jnp.sum(p_near, axis=-1, keepdims=True)
             + jnp.sum(p_new, axis=-1, keepdims=True))
        acc = (jnp.dot(p_far.astype(BF16), cv[:far], preferred_element_type=F32)
               + jnp.dot(p_near.astype(BF16), cv[far:], preferred_element_type=F32)
               + jnp.dot(p_new.astype(BF16), vn_ref[0, :, lanes], preferred_element_type=F32))
        o_ref[0, :, lanes] = _diff_combine(acc, l, lam, g_ref[...], lam_init, ts).astype(o_ref.dtype)


def _sample_attention(q, kb, vb, cache_k, cache_v, rel_bias, lamv, subln_g, lam_init):
    s, ts, _ = q.shape
    past = cache_k.shape[1] // N_HEADS
    near = MAX_DISTANCE
    assert past >= near and past % V7X_SUBLANES == 0
    q_pos = past + jnp.arange(ts, dtype=jnp.int32)
    k_pos = (past - near) + jnp.arange(near + ts, dtype=jnp.int32)
    bias = _chunk_mask(_bias_band(rel_bias, ts, near + ts, -near), q_pos, k_pos)
    new = pl.BlockSpec((1, ts, ATTN_WIDTH), lambda si: (si, 0, 0))
    cache = pl.BlockSpec((1, past * N_HEADS, HEAD_V), lambda si: (si, 0, 0))
    return pl.pallas_call(
        functools.partial(_sample_attn_kernel, ts=ts, past=past, near=near, lam_init=lam_init),
        out_shape=jax.ShapeDtypeStruct(q.shape, BF16),
        grid=(s,),
        in_specs=[_resident(lamv.shape), _resident((1, HEAD_V)), new, cache, cache, new, new,
                  _resident(bias.shape)],
        out_specs=new,
        compiler_params=pltpu.CompilerParams(
            dimension_semantics=("arbitrary",), vmem_limit_bytes=V7X_VMEM_LIMIT_BYTES),
        name="sample_attention",
    )(lamv, subln_g.reshape(1, HEAD_V), q, cache_k, cache_v, kb, vb, bias)


def _finish_kernel(x_ref, a_ref, c_ref, g0_ref, b0_ref, wo_ref, g1_ref, b1_ref, w1_ref, w2_ref,
                   g2_ref, b2_ref, o_ref, *, alpha, chunks):
    rows = x_ref.shape[0] // chunks
    rs = [pl.ds(c * rows, rows) for c in range(chunks)]
    xn, y, x1, h, f = [], [], [], [], []
    for c in range(chunks):
        xn.append(_layer_norm(x_ref[rs[c], :], g0_ref[...], b0_ref[...]))
        mix = jnp.concatenate([a_ref[rs[c], :], c_ref[rs[c], :]], axis=-1)
        y.append(jnp.dot(mix, wo_ref[...], preferred_element_type=F32))
    for c in range(chunks):
        x1.append(_layer_norm(alpha * xn[c] + y[c], g1_ref[...], b1_ref[...]))
        h.append(jnp.dot(x1[c].astype(BF16), w1_ref[...], preferred_element_type=F32))
    for c in range(chunks):
        hc = jnp.square(jnp.maximum(h[c], 0.0)).astype(BF16)
        f.append(jnp.dot(hc, w2_ref[...], preferred_element_type=F32))
    for c in range(chunks):
        o_ref[rs[c], :] = _layer_norm(alpha * x1[c] + f[c], g2_ref[...], b2_ref[...])


def _finish(x2d, attn, conv, ln0_g, ln0_b, w_out_bf, ln1_g, ln1_b, w1_bf, w2_bf, ln2_g, ln2_b, alpha):
    n, d = x2d.shape
    a = attn.shape[1]
    c = conv.shape[1]
    f = w1_bf.shape[1]
    tm = min(ROW_TILE, n)
    assert n % tm == 0
    row = lambda w: pl.BlockSpec((tm, w), lambda i: (i, 0))
    vec = lambda p: p.reshape(1, d)
    return pl.pallas_call(
        functools.partial(_finish_kernel, alpha=alpha, chunks=FINISH_CHUNKS),
        out_shape=jax.ShapeDtypeStruct((n, d), F32),
        grid=(n // tm,),
        in_specs=[row(d), row(a), row(c), _resident((1, d)), _resident((1, d)), _resident((a + c, d)),
                  _resident((1, d)), _resident((1, d)), _resident((d, f)), _resident((f, d)),
                  _resident((1, d)), _resident((1, d))],
        out_specs=row(d),
        compiler_params=pltpu.CompilerParams(
            dimension_semantics=("arbitrary",), vmem_limit_bytes=V7X_VMEM_LIMIT_BYTES),
        name="finish",
    )(x2d, attn, conv, vec(ln0_g), vec(ln0_b), w_out_bf, vec(ln1_g), vec(ln1_b), w1_bf, w2_bf,
      vec(ln2_g), vec(ln2_b))


def kernel(x_prompt, x_sample, cache_k, cache_v, cache_conv, ln0_g, ln0_b, rel_bias, w_in, conv_w,
           lambda_q1, lambda_k1, lambda_q2, lambda_k2, subln_g, w_out, ln1_g, ln1_b,
           w_ff1, w_ff2, ln2_g, ln2_b):
    depth = w_in.shape[0]
    assert depth == 1, "single-layer step"
    layer = 0
    b, t, d = x_prompt.shape
    s, ts, _ = x_sample.shape
    past = cache_k.shape[2]
    c = conv_w.shape[-1]
    alpha = (2.0 * depth) ** 0.25
    lam_init = _lambda_init(layer)

    w_in_f = w_in[layer].astype(F32)
    lamv = jnp.stack([lambda_q1[layer], lambda_k1[layer], lambda_q2[layer], lambda_k2[layer]]).astype(F32)

    xp2 = x_prompt.reshape(b * t, d)
    hist0 = jnp.zeros((b, CONV_K - 1, c), F32)
    tq = min(Q_TILE, t)
    qp, kp, vp, kpb, vpb, convp, tailp, w_out_bf, w1_bf, w2_bf = _in_proj(
        xp2, t, ln0_g, ln0_b, w_in_f, conv_w[layer], hist0, qv_tile=tq,
        cast=(w_out[layer].astype(F32), w_ff1[layer].astype(F32), w_ff2[layer].astype(F32)))
    ffn = (ln0_g, ln0_b, w_out_bf, ln1_g[layer], ln1_b[layer], w1_bf, w2_bf, ln2_g[layer], ln2_b[layer], alpha)
    tiles = (b, t // tq, ATTN_WIDTH, tq)
    attnp = _prompt_attention(qp.reshape(tiles), kpb.reshape(b, t, ATTN_WIDTH), vpb.reshape(tiles), rel_bias,
                              lamv, subln_g[layer], lam_init)
    yp = _finish(xp2, attnp.reshape(b * t, ATTN_WIDTH), convp, *ffn).reshape(b, t, d)

    xs2 = x_sample.reshape(s * ts, d)
    qs, ks, vs, ksb, vsb, convs, tails = _in_proj(xs2, ts, ln0_g, ln0_b, w_in_f, conv_w[layer],
                                                  cache_conv[layer].astype(F32))
    shs = (s, ts, ATTN_WIDTH)
    attns = _sample_attention(qs.reshape(shs), ksb.reshape(shs), vsb.reshape(shs),
                              cache_k[layer].astype(F32).reshape(s, past * N_HEADS, HEAD_V),
                              cache_v[layer].astype(F32).reshape(s, past * N_HEADS, HEAD_V),
                              rel_bias, lamv, subln_g[layer], lam_init)
    ys = _finish(xs2, attns.reshape(s * ts, ATTN_WIDTH), convs, *ffn).reshape(s, ts, d)

    keep = CONV_K - 1
    return (yp, ys,
            kp.reshape(1, b, t, N_HEADS, HEAD_V), vp.reshape(1, b, t, N_HEADS, HEAD_V),
            tailp[:, V7X_SUBLANES - keep:, :].reshape(1, b, keep, c),
            ks.reshape(1, s, ts, N_HEADS, HEAD_V), vs.reshape(1, s, ts, N_HEADS, HEAD_V),
            tails[:, V7X_SUBLANES - keep:, :].reshape(1, s, keep, c))
```

```python
import functools
import math

import jax
import jax.numpy as jnp
from jax import lax
from jax.experimental import pallas as pl
from jax.experimental.pallas import tpu as pltpu

CHUNK = 64
N_HEADS = 4
HEAD_V = 128
HEAD_QK = HEAD_V // 2
ATTN_WIDTH = N_HEADS * HEAD_V
CONV_K = 3
N_BUCKETS = 32
MAX_DISTANCE = 128
LN_EPS = 1e-5
SUBLN_EPS = 1e-5

V7X_SUBLANES = 8
V7X_BF16_SUBLANES = 16
V7X_VMEM_LIMIT_BYTES = 56 * 1024 * 1024

ROW_TILE = 512
FINISH_CHUNKS = 2
Q_TILE = 256
K_BLOCK = 1024
HEADS_PER_STEP = 4

F32 = jnp.float32
BF16 = jnp.bfloat16


def _lambda_init(layer):
    return 0.8 - 0.6 * math.exp(-0.3 * layer)


def _layer_norm(x, g, b):
    mu = jnp.mean(x, axis=-1, keepdims=True)
    xc = x - mu
    var = jnp.mean(xc * xc, axis=-1, keepdims=True)
    return xc * lax.rsqrt(var + LN_EPS) * g + b


def _resident(shape):
    nd = len(shape)
    return pl.BlockSpec(shape, lambda *_: (0,) * nd, pipeline_mode=pl.Buffered(1))


def _in_proj_kernel(x_ref, g_ref, b_ref, w_ref, cw_ref, hist_ref, *rest, seg_len, tm, qv_tile, n_cast):
    cast_in, rest = rest[:n_cast], rest[n_cast:]
    q_ref, k_ref, v_ref, kb_ref, vb_ref, conv_ref, utail_ref = rest[:7]
    cast_out, (carry_ref,) = rest[7:7 + n_cast], rest[7 + n_cast:]
    a = ATTN_WIDTH
    c = (w_ref.shape[1] - 3 * a) // 3
    for src, dst in zip(cast_in, cast_out):
        dst[...] = src[...].astype(BF16)
    xn = _layer_norm(x_ref[...], g_ref[...], b_ref[...])
    z = jnp.dot(xn, w_ref[...], preferred_element_type=F32)
    gb = z[:, 3 * a:3 * a + c]
    u = z[:, 3 * a + c:3 * a + 2 * c] * z[:, 3 * a + 2 * c:]
    q = z[:, :a] * (HEAD_QK ** -0.5)
    k = z[:, a:2 * a]
    v = z[:, 2 * a:3 * a]
    for h in range(N_HEADS):
        k_ref[pl.ds(h, tm, stride=N_HEADS), :] = k[:, h * HEAD_V:(h + 1) * HEAD_V]
        v_ref[pl.ds(h, tm, stride=N_HEADS), :] = v[:, h * HEAD_V:(h + 1) * HEAD_V]
    kb_ref[...] = k.astype(BF16)
    if qv_tile is None:
        q_ref[...] = q.astype(BF16)
        vb_ref[...] = v.astype(BF16)
    else:
        for s in range(tm // qv_tile):
            rows_s = slice(s * qv_tile, (s + 1) * qv_tile)
            q_ref[s] = q[rows_s, :].T.astype(BF16)
            vb_ref[s] = v[rows_s, :].T.astype(BF16)
    rows = lax.broadcasted_iota(jnp.int32, (tm, 1), 0)
    if seg_len >= tm:
        tiles_per_seg = seg_len // tm
        first = (pl.program_id(0) % tiles_per_seg) == 0
        hist = hist_ref[0]
        prev = carry_ref[...]
        p1 = jnp.where(first, hist[1:2, :], prev[V7X_SUBLANES - 1:, :])
        p0 = jnp.where(first, hist[0:1, :], prev[V7X_SUBLANES - 2:V7X_SUBLANES - 1, :])
        pos = rows
        tail = u[tm - V7X_SUBLANES:, :]
        carry_ref[...] = tail
        utail_ref[0] = tail
    else:
        nseg = tm // seg_len
        hist = hist_ref[...]
        p1 = jnp.broadcast_to(hist[:, 1:2, :], (nseg, seg_len, c)).reshape(tm, c)
        p0 = jnp.broadcast_to(hist[:, 0:1, :], (nseg, seg_len, c)).reshape(tm, c)
        pos = rows & (seg_len - 1)
        utail_ref[...] = u.reshape(nseg, seg_len, c)[:, seg_len - V7X_SUBLANES:, :]
    u1 = jnp.where(pos == 0, p1, pltpu.roll(u, 1, 0))
    u2 = jnp.where(pos == 0, p0, jnp.where(pos == 1, p1, pltpu.roll(u, 2, 0)))
    cw = cw_ref[...]
    conv = gb * (u2 * cw[0:1, :] + u1 * cw[1:2, :] + u * cw[2:3, :])
    conv_ref[...] = conv.astype(BF16)


def _in_proj(x2d, seg_len, ln_g, ln_b, w_in, conv_w, hist, qv_tile=None, cast=()):
    n, d = x2d.shape
    e = w_in.shape[1]
    a = ATTN_WIDTH
    c = (e - 3 * a) // 3
    tm = min(ROW_TILE, n)
    assert n % tm == 0 and (seg_len % tm == 0 or tm % seg_len == 0)
    assert seg_len & (seg_len - 1) == 0 and seg_len % V7X_SUBLANES == 0
    nseq = n // seg_len
    if seg_len >= tm:
        tiles_per_seg = seg_len // tm
        hist_spec = pl.BlockSpec((1, CONV_K - 1, c), lambda i: (i // tiles_per_seg, 0, 0))
        tail_spec = pl.BlockSpec((1, V7X_SUBLANES, c), lambda i: (i // tiles_per_seg, 0, 0))
    else:
        nseg = tm // seg_len
        hist_spec = pl.BlockSpec((nseg, CONV_K - 1, c), lambda i: (i, 0, 0))
        tail_spec = pl.BlockSpec((nseg, V7X_SUBLANES, c), lambda i: (i, 0, 0))
    row = lambda w: pl.BlockSpec((tm, w), lambda i: (i, 0))
    headrow = pl.BlockSpec((tm * N_HEADS, HEAD_V), lambda i: (i, 0))
    if qv_tile is None:
        qv_shape = jax.ShapeDtypeStruct((n, a), BF16)
        qv_spec = row(a)
    else:
        assert tm % qv_tile == 0
        qv_shape = jax.ShapeDtypeStruct((n // qv_tile, a, qv_tile), BF16)
        qv_spec = pl.BlockSpec((tm // qv_tile, a, qv_tile), lambda i: (i, 0, 0))
    out_shape = (
        qv_shape,
        jax.ShapeDtypeStruct((n * N_HEADS, HEAD_V), F32),
        jax.ShapeDtypeStruct((n * N_HEADS, HEAD_V), F32),
        jax.ShapeDtypeStruct((n, a), BF16),
        qv_shape,
        jax.ShapeDtypeStruct((n, c), BF16),
        jax.ShapeDtypeStruct((nseq, V7X_SUBLANES, c), F32),
    ) + tuple(jax.ShapeDtypeStruct(m.shape, BF16) for m in cast)
    steps = n // tm
    slab_specs = []
    for m in cast:
        assert m.shape[0] % (steps * V7X_BF16_SUBLANES) == 0
        slab_specs.append(pl.BlockSpec((m.shape[0] // steps, m.shape[1]), lambda i: (i, 0)))
    return pl.pallas_call(
        functools.partial(_in_proj_kernel, seg_len=seg_len, tm=tm, qv_tile=qv_tile, n_cast=len(cast)),
        out_shape=out_shape,
        grid=(steps,),
        in_specs=[row(d), _resident((1, d)), _resident((1, d)), _resident((d, e)),
                  _resident((CONV_K, c)), hist_spec, *slab_specs],
        out_specs=(qv_spec, headrow, headrow, row(a), qv_spec, row(c), tail_spec, *slab_specs),
        scratch_shapes=[pltpu.VMEM((V7X_SUBLANES, c), F32)],
        compiler_params=pltpu.CompilerParams(
            dimension_semantics=("arbitrary",), vmem_limit_bytes=V7X_VMEM_LIMIT_BYTES),
        name="in_proj",
    )(x2d, ln_g.reshape(1, d), ln_b.reshape(1, d), w_in, conv_w, hist, *cast)


def _stack_maps(q):
    lane = lax.broadcasted_iota(jnp.int32, q.shape, 1)
    zero = jnp.zeros_like(q)
    return jnp.concatenate([jnp.where(lane < HEAD_QK, q, zero), jnp.where(lane < HEAD_QK, zero, q)], axis=0)


def _scores(qz, kblk):
    return lax.dot_general(qz, kblk, (((1,), (1,)), ((), ())), preferred_element_type=F32)


def _add_to_both_maps(s, bias):
    t, n = bias.shape
    return (s.reshape(2, t, n) + bias[None]).reshape(2 * t, n)


def _lambda(lamv_ref, lam_init):
    lv = lamv_ref[...]
    s1 = jnp.sum(lv[0:1, :] * lv[1:2, :], axis=-1, keepdims=True)
    s2 = jnp.sum(lv[2:3, :] * lv[3:4, :], axis=-1, keepdims=True)
    return jnp.exp(s1) - jnp.exp(s2) + lam_init


def _diff_combine(acc, l, lam, g, lam_init, t):
    o = acc[:t] / l[:t] - lam * (acc[t:] / l[t:])
    ms = jnp.mean(o * o, axis=-1, keepdims=True)
    return o * lax.rsqrt(ms + SUBLN_EPS) * g * (1.0 - lam_init)


def _rel_bucket(rel):
    half = N_BUCKETS // 2
    max_exact = half // 2
    ret = jnp.where(rel > 0, half, 0)
    n = jnp.abs(rel)
    nf = jnp.maximum(n, 1).astype(F32)
    large = max_exact + (jnp.log(nf / max_exact) / math.log(MAX_DISTANCE / max_exact)
                         * (half - max_exact)).astype(jnp.int32)
    large = jnp.minimum(large, half - 1)
    return ret + jnp.where(n < max_exact, n, large)


def _bias_band(rel_bias, n_q, n_k, rel00):
    length = n_q + n_k
    rel = rel00 - (n_q - 1) + jnp.arange(length, dtype=jnp.int32)
    far = rel_bias[N_BUCKETS // 2 - 1].astype(F32)
    w = (rel_bias[_rel_bucket(rel)].astype(F32) - far[None, :]).T
    skew = jnp.tile(w, (1, n_q))[:, :n_q * (length - 1)].reshape(-1, n_q, length - 1)
    return skew[:, :, n_q - 1:n_q - 1 + n_k]


def _chunk_mask(tile, q_pos, k_pos):
    visible = (k_pos[None, :] // CHUNK) <= (q_pos[:, None] // CHUNK)
    return jnp.where(visible[None], tile, -jnp.inf)


SUM_LO = 2.0 ** -60
SUM_HI = 2.0 ** 60


def _prompt_attn_kernel(lamv_ref, g_ref, q_ref, k_ref, v_ref, bd_ref, bp_ref, o_ref,
                        acc_ref, lp_ref, l_ref, m_ref, *, tq, tk, hp, lam_init):
    sub = tk // tq
    qt = pl.program_id(2)
    feats = [slice(h * HEAD_V, (h + 1) * HEAD_V) for h in range(hp)]

    def stack_maps(qT):
        row = lax.broadcasted_iota(jnp.int32, qT.shape, 0)
        zero = jnp.zeros_like(qT)
        return jnp.concatenate([jnp.where(row < HEAD_QK, qT, zero), jnp.where(row < HEAD_QK, zero, qT)], axis=1)

    qz = [stack_maps(q_ref[0, 0, ft, :]) for ft in feats]

    def kv(j, h):
        start = pl.multiple_of(j * tq, tq)
        return k_ref[0, pl.ds(start, tq), feats[h]], v_ref[0, j, feats[h], :]

    def logits(h, kj, bias):
        s = jnp.dot(kj, qz[h], preferred_element_type=F32)
        if bias is not None:
            s = s + jnp.concatenate([bias, bias], axis=1)
        return s

    def sublane_fold(p):
        return jnp.sum(p.reshape(p.shape[0] // V7X_SUBLANES, V7X_SUBLANES, p.shape[1]), axis=0)

    def numerators(chains, lag, first=False):
        totals = {}
        pending = []

        def values(h, vj, p):
            pv = jnp.dot(vj, p.astype(BF16), preferred_element_type=F32)
            lp = sublane_fold(p)
            totals[h] = (pv, lp) if h not in totals else (totals[h][0] + pv, totals[h][1] + lp)

        for h, j, bias in chains:
            kj, vj = kv(j, h)
            pending.append((h, vj, jnp.exp(logits(h, kj, bias))))
            if len(pending) > lag:
                values(*pending.pop(0))
        while pending:
            values(*pending.pop(0))
        for h, (pv, lp) in totals.items():
            if first:
                acc_ref[h] = pv
                lp_ref[h] = lp
            else:
                acc_ref[h] += pv
                lp_ref[h] += lp

    heads = range(hp)
    n_far = jnp.maximum(qt - 1, 0)
    n_lead = n_far % sub
    n_blocks = n_far // sub

    @pl.when(qt == 0)
    def _():
        numerators([(h, qt, bd_ref[h]) for h in heads], hp, first=True)

    for lead in range(sub):
        @pl.when((qt > 0) & (n_lead == lead))
        def _(lead=lead):
            near = ((0, bd_ref), (1, bp_ref))
            numerators([(h, qt - d, None if b_ref is None else b_ref[h]) for h in heads
                        for d, b_ref in near + tuple((2 + f, None) for f in range(lead))], hp, first=True)

    def pair(jb, carry):
        numerators([(h, jb * 2 * sub + c, None) for h in heads for c in range(2 * sub)], sub)
        return carry

    lax.fori_loop(0, n_blocks // 2, pair, 0)

    @pl.when(n_blocks % 2 == 1)
    def _():
        numerators([(h, (n_blocks - 1) * sub + c, None) for h in heads for c in range(sub)], sub)

    for h in range(hp):
        l_ref[h] = jnp.sum(lp_ref[h], axis=0, keepdims=True)
    l_all = l_ref[...]
    in_range = (l_all >= SUM_LO) & (l_all <= SUM_HI)

    @pl.when(jnp.min(jnp.where(in_range, 1.0, 0.0)) < 0.5)
    def _():
        for h in range(hp):
            kd, vd = kv(qt, h)
            s = logits(h, kd, bd_ref[h])
            m0 = jnp.max(s, axis=0, keepdims=True)
            p = jnp.exp(s - m0)
            m_ref[h] = m0
            l_ref[h] = jnp.sum(p, axis=0, keepdims=True)
            acc_ref[h] = jnp.dot(vd, p.astype(BF16), preferred_element_type=F32)

            def step(j, carry, h=h):
                kj, vj = kv(j, h)
                s = logits(h, kj, jnp.where(j == qt - 1, bp_ref[h], jnp.zeros_like(bp_ref[h])))
                m_prev = m_ref[h]
                m_new = jnp.maximum(m_prev, jnp.max(s, axis=0, keepdims=True))
                alpha = jnp.exp(m_prev - m_new)
                p = jnp.exp(s - m_new)
                l_ref[h] = alpha * l_ref[h] + jnp.sum(p, axis=0, keepdims=True)
                acc_ref[h] = alpha * acc_ref[h] + jnp.dot(vj, p.astype(BF16), preferred_element_type=F32)
                m_ref[h] = m_new
                return carry

            lax.fori_loop(0, qt, step, 0)

    lam = _lambda(lamv_ref, lam_init)
    for h in range(hp):
        o = acc_ref[h] / l_ref[h]
        o = o[:, :tq] - lam * o[:, tq:]
        ms = jnp.mean(o * o, axis=0, keepdims=True)
        o = o * lax.rsqrt(ms + SUBLN_EPS) * g_ref[...] * (1.0 - lam_init)
        o_ref[0, :, feats[h]] = o.T.astype(o_ref.dtype)


def _prompt_attention(qT, kb, vT, rel_bias, lamv, subln_g, lam_init):
    b, t, _ = kb.shape
    tq = qT.shape[-1]
    tk = min(K_BLOCK, t)
    sub = tk // tq
    assert t % tq == 0 and tk % tq == 0 and tq % CHUNK == 0 and tq >= MAX_DISTANCE
    pos = jnp.arange(tq, dtype=jnp.int32)
    band = _bias_band(rel_bias, tq, 2 * tq, -tq)
    prev = jnp.swapaxes(band[:, :, :tq], 1, 2)
    diag = jnp.swapaxes(_chunk_mask(band[:, :, tq:], pos, pos), 1, 2)
    gain = jnp.broadcast_to(subln_g.astype(F32)[:, None], (HEAD_V, tq))
    hp = HEADS_PER_STEP
    assert N_HEADS % hp == 0
    qspec = pl.BlockSpec((1, 1, hp * HEAD_V, tq), lambda bi, g, i: (bi, i, g, 0))
    kspec = pl.BlockSpec((1, t, hp * HEAD_V), lambda bi, g, i: (bi, 0, g))
    vspec = pl.BlockSpec((1, t // tq, hp * HEAD_V, tq), lambda bi, g, i: (bi, 0, g, 0))
    bspec = pl.BlockSpec((hp, tq, tq), lambda bi, g, i: (g, 0, 0))
    return pl.pallas_call(
        functools.partial(_prompt_attn_kernel, tq=tq, tk=tk, hp=hp, lam_init=lam_init),
        out_shape=jax.ShapeDtypeStruct(kb.shape, BF16),
        grid=(b, N_HEADS // hp, t // tq),
        in_specs=[_resident(lamv.shape), _resident((HEAD_V, tq)), qspec, kspec, vspec, bspec, bspec],
        out_specs=pl.BlockSpec((1, tq, hp * HEAD_V), lambda bi, g, i: (bi, i, g)),
        scratch_shapes=[pltpu.VMEM((hp, HEAD_V, 2 * tq), F32), pltpu.VMEM((hp, V7X_SUBLANES, 2 * tq), F32),
                        pltpu.VMEM((hp, 1, 2 * tq), F32), pltpu.VMEM((hp, 1, 2 * tq), F32)],
        compiler_params=pltpu.CompilerParams(
            dimension_semantics=("arbitrary", "arbitrary", "arbitrary"),
            vmem_limit_bytes=V7X_VMEM_LIMIT_BYTES),
        name="prompt_attention",
    )(lamv, gain, qT, kb, vT, diag, prev)


def _sample_attn_kernel(lamv_ref, g_ref, q_ref, ck_ref, cv_ref, kn_ref, vn_ref, bias_ref, o_ref,
                        *, ts, past, near, lam_init):
    far = past - near
    lam = _lambda(lamv_ref, lam_init)
    for h in range(N_HEADS):
        lanes = slice(h * HEAD_V, (h + 1) * HEAD_V)
        qz = _stack_maps(q_ref[0, :, lanes])
        ck = ck_ref[0, pl.ds(h, past, stride=N_HEADS), :].astype(BF16)
        cv = cv_ref[0, pl.ds(h, past, stride=N_HEADS), :].astype(BF16)
        bias = bias_ref[h]
        s_far = _scores(qz, ck[:far])
        s_near = _add_to_both_maps(_scores(qz, ck[far:]), bias[:, :near])
        s_new = _add_to_both_maps(_scores(qz, kn_ref[0, :, lanes]), bias[:, near:])
        m = jnp.maximum(jnp.max(s_far, axis=-1, keepdims=True),
                        jnp.maximum(jnp.max(s_near, axis=-1, keepdims=True),
                                    jnp.max(s_new, axis=-1, keepdims=True)))
        p_far = jnp.exp(s_far - m)
        p_near = jnp.exp(s_near - m)
        p_new = jnp.exp(s_new - m)
        l = (jnp.sum(p_far, axis=-1, keepdims=True) + jnp.sum(p_near, axis=-1, keepdims=True)
             + jnp.sum(p_new, axis=-1, keepdims=True))
        acc = (jnp.dot(p_far.astype(BF16), cv[:far], preferred_element_type=F32)
               + jnp.dot(p_near.astype(BF16), cv[far:], preferred_element_type=F32)
               + jnp.dot(p_new.astype(BF16), vn_ref[0, :, lanes], preferred_element_type=F32))
        o_ref[0, :, lanes] = _diff_combine(acc, l, lam, g_ref[...], lam_init, ts).astype(o_ref.dtype)


def _sample_attention(q, kb, vb, cache_k, cache_v, rel_bias, lamv, subln_g, lam_init):
    s, ts, _ = q.shape
    past = cache_k.shape[1] // N_HEADS
    near = MAX_DISTANCE
    assert past >= near and past % V7X_SUBLANES == 0
    q_pos = past + jnp.arange(ts, dtype=jnp.int32)
    k_pos = (past - near) + jnp.arange(near + ts, dtype=jnp.int32)
    bias = _chunk_mask(_bias_band(rel_bias, ts, near + ts, -near), q_pos, k_pos)
    new = pl.BlockSpec((1, ts, ATTN_WIDTH), lambda si: (si, 0, 0))
    cache = pl.BlockSpec((1, past * N_HEADS, HEAD_V), lambda si: (si, 0, 0))
    return pl.pallas_call(
        functools.partial(_sample_attn_kernel, ts=ts, past=past, near=near, lam_init=lam_init),
        out_shape=jax.ShapeDtypeStruct(q.shape, BF16),
        grid=(s,),
        in_specs=[_resident(lamv.shape), _resident((1, HEAD_V)), new, cache, cache, new, new,
                  _resident(bias.shape)],
        out_specs=new,
        compiler_params=pltpu.CompilerParams(
            dimension_semantics=("arbitrary",), vmem_limit_bytes=V7X_VMEM_LIMIT_BYTES),
        name="sample_attention",
    )(lamv, subln_g.reshape(1, HEAD_V), q, cache_k, cache_v, kb, vb, bias)


def _finish_kernel(x_ref, a_ref, c_ref, g0_ref, b0_ref, wo_ref, g1_ref, b1_ref, w1_ref, w2_ref,
                   g2_ref, b2_ref, o_ref, *, alpha, chunks):
    rows = x_ref.shape[0] // chunks
    rs = [pl.ds(c * rows, rows) for c in range(chunks)]
    xn, y, x1, h, f = [], [], [], [], []
    for c in range(chunks):
        xn.append(_layer_norm(x_ref[rs[c], :], g0_ref[...], b0_ref[...]))
        mix = jnp.concatenate([a_ref[rs[c], :], c_ref[rs[c], :]], axis=-1)
        y.append(jnp.dot(mix, wo_ref[...], preferred_element_type=F32))
    for c in range(chunks):
        x1.append(_layer_norm(alpha * xn[c] + y[c], g1_ref[...], b1_ref[...]))
        h.append(jnp.dot(x1[c].astype(BF16), w1_ref[...], preferred_element_type=F32))
    for c in range(chunks):
        hc = jnp.square(jnp.maximum(h[c], 0.0)).astype(BF16)
        f.append(jnp.dot(hc, w2_ref[...], preferred_element_type=F32))
    for c in range(chunks):
        o_ref[rs[c], :] = _layer_norm(alpha * x1[c] + f[c], g2_ref[...], b2_ref[...])


def _finish(x2d, attn, conv, ln0_g, ln0_b, w_out_bf, ln1_g, ln1_b, w1_bf, w2_bf, ln2_g, ln2_b, alpha):
    n, d = x2d.shape
    a = attn.shape[1]
    c = conv.shape[1]
    f = w1_bf.shape[1]
    tm = min(ROW_TILE, n)
    assert n % tm == 0
    row = lambda w: pl.BlockSpec((tm, w), lambda i: (i, 0))
    vec = lambda p: p.reshape(1, d)
    return pl.pallas_call(
        functools.partial(_finish_kernel, alpha=alpha, chunks=FINISH_CHUNKS),
        out_shape=jax.ShapeDtypeStruct((n, d), F32),
        grid=(n // tm,),
        in_specs=[row(d), row(a), row(c), _resident((1, d)), _resident((1, d)), _resident((a + c, d)),
                  _resident((1, d)), _resident((1, d)), _resident((d, f)), _resident((f, d)),
                  _resident((1, d)), _resident((1, d))],
        out_specs=row(d),
        compiler_params=pltpu.CompilerParams(
            dimension_semantics=("arbitrary",), vmem_limit_bytes=V7X_VMEM_LIMIT_BYTES),
        name="finish",
    )(x2d, attn, conv, vec(ln0_g), vec(ln0_b), w_out_bf, vec(ln1_g), vec(ln1_b), w1_bf, w2_bf,
      vec(ln2_g), vec(ln2_b))


def kernel(x_prompt, x_sample, cache_k, cache_v, cache_conv, ln0_g, ln0_b, rel_bias, w_in, conv_w,
           lambda_q1, lambda_k1, lambda_q2, lambda_k2, subln_g, w_out, ln1_g, ln1_b,
           w_ff1, w_ff2, ln2_g, ln2_b):
    depth = w_in.shape[0]
    assert depth == 1, "single-layer step"
    layer = 0
    b, t, d = x_prompt.shape
    s, ts, _ = x_sample.shape
    past = cache_k.shape[2]
    c = conv_w.shape[-1]
    alpha = (2.0 * depth) ** 0.25
    lam_init = _lambda_init(layer)

    w_in_f = w_in[layer].astype(F32)
    lamv = jnp.stack([lambda_q1[layer], lambda_k1[layer], lambda_q2[layer], lambda_k2[layer]]).astype(F32)

    xp2 = x_prompt.reshape(b * t, d)
    hist0 = jnp.zeros((b, CONV_K - 1, c), F32)
    tq = min(Q_TILE, t)
    qp, kp, vp, kpb, vpb, convp, tailp, w_out_bf, w1_bf, w2_bf = _in_proj(
        xp2, t, ln0_g, ln0_b, w_in_f, conv_w[layer], hist0, qv_tile=tq,
        cast=(w_out[layer].astype(F32), w_ff1[layer].astype(F32), w_ff2[layer].astype(F32)))
    ffn = (ln0_g, ln0_b, w_out_bf, ln1_g[layer], ln1_b[layer], w1_bf, w2_bf, ln2_g[layer], ln2_b[layer], alpha)
    tiles = (b, t // tq, ATTN_WIDTH, tq)
    attnp = _prompt_attention(qp.reshape(tiles), kpb.reshape(b, t, ATTN_WIDTH), vpb.reshape(tiles), rel_bias,
                              lamv, subln_g[layer], lam_init)
    yp = _finish(xp2, attnp.reshape(b * t, ATTN_WIDTH), convp, *ffn).reshape(b, t, d)

    xs2 = x_sample.reshape(s * ts, d)
    qs, ks, vs, ksb, vsb, convs, tails = _in_proj(xs2, ts, ln0_g, ln0_b, w_in_f, conv_w[layer],
                                                  cache_conv[layer].astype(F32))
    shs = (s, ts, ATTN_WIDTH)
    attns = _sample_attention(qs.reshape(shs), ksb.reshape(shs), vsb.reshape(shs),
                              cache_k[layer].astype(F32).reshape(s, past * N_HEADS, HEAD_V),
                              cache_v[layer].astype(F32).reshape(s, past * N_HEADS, HEAD_V),
                              rel_bias, lamv, subln_g[layer], lam_init)
    ys = _finish(xs2, attns.reshape(s * ts, ATTN_WIDTH), convs, *ffn).reshape(s, ts, d)

    keep = CONV_K - 1
    return (yp, ys,
            kp.reshape(1, b, t, N_HEADS, HEAD_V), vp.reshape(1, b, t, N_HEADS, HEAD_V),
            tailp[:, V7X_SUBLANES - keep:, :].reshape(1, b, keep, c),
            ks.reshape(1, s, ts, N_HEADS, HEAD_V), vs.reshape(1, s, ts, N_HEADS, HEAD_V),
            tails[:, V7X_SUBLANES - keep:, :].reshape(1, s, keep, c))
```

```python
import functools
import math

import jax
import jax.numpy as jnp
from jax import lax
from jax.experimental import pallas as pl
from jax.experimental.pallas import tpu as pltpu

CHUNK = 64
N_HEADS = 4
HEAD_V = 128
HEAD_QK = HEAD_V // 2
ATTN_WIDTH = N_HEADS * HEAD_V
CONV_K = 3
N_BUCKETS = 32
MAX_DISTANCE = 128
LN_EPS = 1e-5
SUBLN_EPS = 1e-5

V7X_SUBLANES = 8
V7X_BF16_SUBLANES = 16
V7X_VMEM_LIMIT_BYTES = 56 * 1024 * 1024

ROW_TILE = 512
FINISH_CHUNKS = 2
Q_TILE = 256
K_BLOCK = 1024
HEADS_PER_STEP = 4
PAIR_LOOP_LAG = 2

F32 = jnp.float32
BF16 = jnp.bfloat16


def _lambda_init(layer):
    return 0.8 - 0.6 * math.exp(-0.3 * layer)


def _layer_norm(x, g, b):
    mu = jnp.mean(x, axis=-1, keepdims=True)
    xc = x - mu
    var = jnp.mean(xc * xc, axis=-1, keepdims=True)
    return xc * lax.rsqrt(var + LN_EPS) * g + b


def _resident(shape):
    nd = len(shape)
    return pl.BlockSpec(shape, lambda *_: (0,) * nd, pipeline_mode=pl.Buffered(1))


def _in_proj_kernel(x_ref, g_ref, b_ref, w_ref, cw_ref, hist_ref, *rest, seg_len, tm, qv_tile, n_cast):
    cast_in, rest = rest[:n_cast], rest[n_cast:]
    q_ref, k_ref, v_ref, kb_ref, vb_ref, conv_ref, utail_ref = rest[:7]
    cast_out, (carry_ref,) = rest[7:7 + n_cast], rest[7 + n_cast:]
    a = ATTN_WIDTH
    c = (w_ref.shape[1] - 3 * a) // 3
    for src, dst in zip(cast_in, cast_out):
        dst[...] = src[...].astype(BF16)
    xn = _layer_norm(x_ref[...], g_ref[...], b_ref[...])
    z = jnp.dot(xn, w_ref[...], preferred_element_type=F32)
    gb = z[:, 3 * a:3 * a + c]
    u = z[:, 3 * a + c:3 * a + 2 * c] * z[:, 3 * a + 2 * c:]
    q = z[:, :a] * (HEAD_QK ** -0.5)
    k = z[:, a:2 * a]
    v = z[:, 2 * a:3 * a]
    for h in range(N_HEADS):
        k_ref[pl.ds(h, tm, stride=N_HEADS), :] = k[:, h * HEAD_V:(h + 1) * HEAD_V]
        v_ref[pl.ds(h, tm, stride=N_HEADS), :] = v[:, h * HEAD_V:(h + 1) * HEAD_V]
    kb_ref[...] = k.astype(BF16)
    if qv_tile is None:
        q_ref[...] = q.astype(BF16)
        vb_ref[...] = v.astype(BF16)
    else:
        for s in range(tm // qv_tile):
            rows_s = slice(s * qv_tile, (s + 1) * qv_tile)
            q_ref[s] = q[rows_s, :].T.astype(BF16)
            vb_ref[s] = v[rows_s, :].T.astype(BF16)
    rows = lax.broadcasted_iota(jnp.int32, (tm, 1), 0)
    if seg_len >= tm:
        tiles_per_seg = seg_len // tm
        first = (pl.program_id(0) % tiles_per_seg) == 0
        hist = hist_ref[0]
        prev = carry_ref[...]
        p1 = jnp.where(first, hist[1:2, :], prev[V7X_SUBLANES - 1:, :])
        p0 = jnp.where(first, hist[0:1, :], prev[V7X_SUBLANES - 2:V7X_SUBLANES - 1, :])
        pos = rows
        tail = u[tm - V7X_SUBLANES:, :]
        carry_ref[...] = tail
        utail_ref[0] = tail
    else:
        nseg = tm // seg_len
        hist = hist_ref[...]
        p1 = jnp.broadcast_to(hist[:, 1:2, :], (nseg, seg_len, c)).reshape(tm, c)
        p0 = jnp.broadcast_to(hist[:, 0:1, :], (nseg, seg_len, c)).reshape(tm, c)
        pos = rows & (seg_len - 1)
        utail_ref[...] = u.reshape(nseg, seg_len, c)[:, seg_len - V7X_SUBLANES:, :]
    u1 = jnp.where(pos == 0, p1, pltpu.roll(u, 1, 0))
    u2 = jnp.where(pos == 0, p0, jnp.where(pos == 1, p1, pltpu.roll(u, 2, 0)))
    cw = cw_ref[...]
    conv = gb * (u2 * cw[0:1, :] + u1 * cw[1:2, :] + u * cw[2:3, :])
    conv_ref[...] = conv.astype(BF16)


def _in_proj(x2d, seg_len, ln_g, ln_b, w_in, conv_w, hist, qv_tile=None, cast=()):
    n, d = x2d.shape
    e = w_in.shape[1]
    a = ATTN_WIDTH
    c = (e - 3 * a) // 3
    tm = min(ROW_TILE, n)
    assert n % tm == 0 and (seg_len % tm == 0 or tm % seg_len == 0)
    assert seg_len & (seg_len - 1) == 0 and seg_len % V7X_SUBLANES == 0
    nseq = n // seg_len
    if seg_len >= tm:
        tiles_per_seg = seg_len // tm
        hist_spec = pl.BlockSpec((1, CONV_K - 1, c), lambda i: (i // tiles_per_seg, 0, 0))
        tail_spec = pl.BlockSpec((1, V7X_SUBLANES, c), lambda i: (i // tiles_per_seg, 0, 0))
    else:
        nseg = tm // seg_len
        hist_spec = pl.BlockSpec((nseg, CONV_K - 1, c), lambda i: (i, 0, 0))
        tail_spec = pl.BlockSpec((nseg, V7X_SUBLANES, c), lambda i: (i, 0, 0))
    row = lambda w: pl.BlockSpec((tm, w), lambda i: (i, 0))
    headrow = pl.BlockSpec((tm * N_HEADS, HEAD_V), lambda i: (i, 0))
    if qv_tile is None:
        qv_shape = jax.ShapeDtypeStruct((n, a), BF16)
        qv_spec = row(a)
    else:
        assert tm % qv_tile == 0
        qv_shape = jax.ShapeDtypeStruct((n // qv_tile, a, qv_tile), BF16)
        qv_spec = pl.BlockSpec((tm // qv_tile, a, qv_tile), lambda i: (i, 0, 0))
    out_shape = (
        qv_shape,
        jax.ShapeDtypeStruct((n * N_HEADS, HEAD_V), F32),
        jax.ShapeDtypeStruct((n * N_HEADS, HEAD_V), F32),
        jax.ShapeDtypeStruct((n, a), BF16),
        qv_shape,
        jax.ShapeDtypeStruct((n, c), BF16),
        jax.ShapeDtypeStruct((nseq, V7X_SUBLANES, c), F32),
    ) + tuple(jax.ShapeDtypeStruct(m.shape, BF16) for m in cast)
    steps = n // tm
    slab_specs = []
    for m in cast:
        assert m.shape[0] % (steps * V7X_BF16_SUBLANES) == 0
        slab_specs.append(pl.BlockSpec((m.shape[0] // steps, m.shape[1]), lambda i: (i, 0)))
    return pl.pallas_call(
        functools.partial(_in_proj_kernel, seg_len=seg_len, tm=tm, qv_tile=qv_tile, n_cast=len(cast)),
        out_shape=out_shape,
        grid=(steps,),
        in_specs=[row(d), _resident((1, d)), _resident((1, d)), _resident((d, e)),
                  _resident((CONV_K, c)), hist_spec, *slab_specs],
        out_specs=(qv_spec, headrow, headrow, row(a), qv_spec, row(c), tail_spec, *slab_specs),
        scratch_shapes=[pltpu.VMEM((V7X_SUBLANES, c), F32)],
        compiler_params=pltpu.CompilerParams(
            dimension_semantics=("arbitrary",), vmem_limit_bytes=V7X_VMEM_LIMIT_BYTES),
        name="in_proj",
    )(x2d, ln_g.reshape(1, d), ln_b.reshape(1, d), w_in, conv_w, hist, *cast)


def _stack_maps(q):
    lane = lax.broadcasted_iota(jnp.int32, q.shape, 1)
    zero = jnp.zeros_like(q)
    return jnp.concatenate([jnp.where(lane < HEAD_QK, q, zero), jnp.where(lane < HEAD_QK, zero, q)], axis=0)


def _scores(qz, kblk):
    return lax.dot_general(qz, kblk, (((1,), (1,)), ((), ())), preferred_element_type=F32)


def _add_to_both_maps(s, bias):
    t, n = bias.shape
    return (s.reshape(2, t, n) + bias[None]).reshape(2 * t, n)


def _lambda(lamv_ref, lam_init):
    lv = lamv_ref[...]
    s1 = jnp.sum(lv[0:1, :] * lv[1:2, :], axis=-1, keepdims=True)
    s2 = jnp.sum(lv[2:3, :] * lv[3:4, :], axis=-1, keepdims=True)
    return jnp.exp(s1) - jnp.exp(s2) + lam_init


def _diff_combine(acc, l, lam, g, lam_init, t):
    o = acc[:t] / l[:t] - lam * (acc[t:] / l[t:])
    ms = jnp.mean(o * o, axis=-1, keepdims=True)
    return o * lax.rsqrt(ms + SUBLN_EPS) * g * (1.0 - lam_init)


def _rel_bucket(rel):
    half = N_BUCKETS // 2
    max_exact = half // 2
    ret = jnp.where(rel > 0, half, 0)
    n = jnp.abs(rel)
    nf = jnp.maximum(n, 1).astype(F32)
    large = max_exact + (jnp.log(nf / max_exact) / math.log(MAX_DISTANCE / max_exact)
                         * (half - max_exact)).astype(jnp.int32)
    large = jnp.minimum(large, half - 1)
    return ret + jnp.where(n < max_exact, n, large)


def _bias_band(rel_bias, n_q, n_k, rel00):
    length = n_q + n_k
    rel = rel00 - (n_q - 1) + jnp.arange(length, dtype=jnp.int32)
    far = rel_bias[N_BUCKETS // 2 - 1].astype(F32)
    w = (rel_bias[_rel_bucket(rel)].astype(F32) - far[None, :]).T
    skew = jnp.tile(w, (1, n_q))[:, :n_q * (length - 1)].reshape(-1, n_q, length - 1)
    return skew[:, :, n_q - 1:n_q - 1 + n_k]


def _chunk_mask(tile, q_pos, k_pos):
    visible = (k_pos[None, :] // CHUNK) <= (q_pos[:, None] // CHUNK)
    return jnp.where(visible[None], tile, -jnp.inf)


SUM_LO = 2.0 ** -60
SUM_HI = 2.0 ** 60


def _prompt_attn_kernel(lamv_ref, g_ref, q_ref, k_ref, v_ref, bd_ref, bp_ref, o_ref,
                        acc_ref, lp_ref, l_ref, m_ref, *, tq, tk, hp, lam_init):
    sub = tk // tq
    qt = pl.program_id(2)
    feats = [slice(h * HEAD_V, (h + 1) * HEAD_V) for h in range(hp)]

    def stack_maps(qT):
        row = lax.broadcasted_iota(jnp.int32, qT.shape, 0)
        zero = jnp.zeros_like(qT)
        return jnp.concatenate([jnp.where(row < HEAD_QK, qT, zero), jnp.where(row < HEAD_QK, zero, qT)], axis=1)

    qz = [stack_maps(q_ref[0, 0, ft, :]) for ft in feats]

    def kv(j, h):
        start = pl.multiple_of(j * tq, tq)
        return k_ref[0, pl.ds(start, tq), feats[h]], v_ref[0, j, feats[h], :]

    def logits(h, kj, bias):
        s = jnp.dot(kj, qz[h], preferred_element_type=F32)
        if bias is not None:
            s = s + jnp.concatenate([bias, bias], axis=1)
        return s

    def sublane_fold(p):
        return jnp.sum(p.reshape(p.shape[0] // V7X_SUBLANES, V7X_SUBLANES, p.shape[1]), axis=0)

    def numerators(chains, lag, first=False):
        totals = {}
        pending = []

        def values(h, vj, p):
            pv = jnp.dot(vj, p.astype(BF16), preferred_element_type=F32)
            lp = sublane_fold(p)
            totals[h] = (pv, lp) if h not in totals else (totals[h][0] + pv, totals[h][1] + lp)

        for h, j, bias in chains:
            kj, vj = kv(j, h)
            pending.append((h, vj, jnp.exp(logits(h, kj, bias))))
            if len(pending) > lag:
                values(*pending.pop(0))
        while pending:
            values(*pending.pop(0))
        for h, (pv, lp) in totals.items():
            if first:
                acc_ref[h] = pv
                lp_ref[h] = lp
            else:
                acc_ref[h] += pv
                lp_ref[h] += lp

    heads = range(hp)
    n_far = jnp.maximum(qt - 1, 0)
    n_lead = n_far % sub
    n_blocks = n_far // sub

    @pl.when(qt == 0)
    def _():
        numerators([(h, qt, bd_ref[h]) for h in heads], hp, first=True)

    for lead in range(sub):
        @pl.when((qt > 0) & (n_lead == lead))
        def _(lead=lead):
            near = ((0, bd_ref), (1, bp_ref))
            numerators([(h, qt - d, None if b_ref is None else b_ref[h]) for h in heads
                        for d, b_ref in near + tuple((2 + f, None) for f in range(lead))], hp, first=True)

    def pair(jb, carry):
        numerators([(h, jb * 2 * sub + c, None) for c in range(2 * sub) for h in heads], PAIR_LOOP_LAG)
        return carry

    lax.fori_loop(0, n_blocks // 2, pair, 0)

    @pl.when(n_blocks % 2 == 1)
    def _():
        numerators([(h, (n_blocks - 1) * sub + c, None) for h in heads for c in range(sub)], sub)

    for h in range(hp):
        l_ref[h] = jnp.sum(lp_ref[h], axis=0, keepdims=True)
    l_all = l_ref[...]
    in_range = (l_all >= SUM_LO) & (l_all <= SUM_HI)

    @pl.when(jnp.min(jnp.where(in_range, 1.0, 0.0)) < 0.5)
    def _():
        for h in range(hp):
            kd, vd = kv(qt, h)
            s = logits(h, kd, bd_ref[h])
            m0 = jnp.max(s, axis=0, keepdims=True)
            p = jnp.exp(s - m0)
            m_ref[h] = m0
            l_ref[h] = jnp.sum(p, axis=0, keepdims=True)
            acc_ref[h] = jnp.dot(vd, p.astype(BF16), preferred_element_type=F32)

            def step(j, carry, h=h):
                kj, vj = kv(j, h)
                s = logits(h, kj, jnp.where(j == qt - 1, bp_ref[h], jnp.zeros_like(bp_ref[h])))
                m_prev = m_ref[h]
                m_new = jnp.maximum(m_prev, jnp.max(s, axis=0, keepdims=True))
                alpha = jnp.exp(m_prev - m_new)
                p = jnp.exp(s - m_new)
                l_ref[h] = alpha * l_ref[h] + jnp.sum(p, axis=0, keepdims=True)
                acc_ref[h] = alpha * acc_ref[h] + jnp.dot(vj, p.astype(BF16), preferred_element_type=F32)
                m_ref[h] = m_new
                return carry

            lax.fori_loop(0, qt, step, 0)

    lam = _lambda(lamv_ref, lam_init)
    for h in range(hp):
        o = acc_ref[h] / l_ref[h]
        o = o[:, :tq] - lam * o[:, tq:]
        ms = jnp.mean(o * o, axis=0, keepdims=True)
        o = o * lax.rsqrt(ms + SUBLN_EPS) * g_ref[...] * (1.0 - lam_init)
        o_ref[0, :, feats[h]] = o.T.astype(o_ref.dtype)


def _prompt_attention(qT, kb, vT, rel_bias, lamv, subln_g, lam_init):
    b, t, _ = kb.shape
    tq = qT.shape[-1]
    tk = min(K_BLOCK, t)
    sub = tk // tq
    assert t % tq == 0 and tk % tq == 0 and tq % CHUNK == 0 and tq >= MAX_DISTANCE
    pos = jnp.arange(tq, dtype=jnp.int32)
    band = _bias_band(rel_bias, tq, 2 * tq, -tq)
    prev = jnp.swapaxes(band[:, :, :tq], 1, 2)
    diag = jnp.swapaxes(_chunk_mask(band[:, :, tq:], pos, pos), 1, 2)
    gain = jnp.broadcast_to(subln_g.astype(F32)[:, None], (HEAD_V, tq))
    hp = HEADS_PER_STEP
    assert N_HEADS % hp == 0
    qspec = pl.BlockSpec((1, 1, hp * HEAD_V, tq), lambda bi, g, i: (bi, i, g, 0))
    kspec = pl.BlockSpec((1, t, hp * HEAD_V), lambda bi, g, i: (bi, 0, g))
    vspec = pl.BlockSpec((1, t // tq, hp * HEAD_V, tq), lambda bi, g, i: (bi, 0, g, 0))
    bspec = pl.BlockSpec((hp, tq, tq), lambda bi, g, i: (g, 0, 0))
    return pl.pallas_call(
        functools.partial(_prompt_attn_kernel, tq=tq, tk=tk, hp=hp, lam_init=lam_init),
        out_shape=jax.ShapeDtypeStruct(kb.shape, BF16),
        grid=(b, N_HEADS // hp, t // tq),
        in_specs=[_resident(lamv.shape), _resident((HEAD_V, tq)), qspec, kspec, vspec, bspec, bspec],
        out_specs=pl.BlockSpec((1, tq, hp * HEAD_V), lambda bi, g, i: (bi, i, g)),
        scratch_shapes=[pltpu.VMEM((hp, HEAD_V, 2 * tq), F32), pltpu.VMEM((hp, V7X_SUBLANES, 2 * tq), F32),
                        pltpu.VMEM((hp, 1, 2 * tq), F32), pltpu.VMEM((hp, 1, 2 * tq), F32)],
        compiler_params=pltpu.CompilerParams(
            dimension_semantics=("arbitrary", "arbitrary", "arbitrary"),
            vmem_limit_bytes=V7X_VMEM_LIMIT_BYTES),
        name="prompt_attention",
    )(lamv, gain, qT, kb, vT, diag, prev)


def _sample_attn_kernel(lamv_ref, g_ref, q_ref, ck_ref, cv_ref, kn_ref, vn_ref, bias_ref, o_ref,
                        *, ts, past, near, lam_init):
    far = past - near
    lam = _lambda(lamv_ref, lam_init)
    lanes = [slice(h * HEAD_V, (h + 1) * HEAD_V) for h in range(N_HEADS)]

    def logits(h):
        qz = _stack_maps(q_ref[0, :, lanes[h]])
        ck = ck_ref[0, pl.ds(h, past, stride=N_HEADS), :].astype(BF16)
        bias = bias_ref[h]
        return (_scores(qz, ck[:far]),
                _add_to_both_maps(_scores(qz, ck[far:]), bias[:, :near]),
                _add_to_both_maps(_scores(qz, kn_ref[0, :, lanes[h]]), bias[:, near:]))

    def attend(h, s_far, s_near, s_new):
        cv = cv_ref[0, pl.ds(h, past, stride=N_HEADS), :].astype(BF16)
        m = jnp.maximum(jnp.max(s_far, axis=-1, keepdims=True),
                        jnp.maximum(jnp.max(s_near, axis=-1, keepdims=True),
                                    jnp.max(s_new, axis=-1, keepdims=True)))
        p_far = jnp.exp(s_far - m)
        p_near = jnp.exp(s_near - m)
        p_new = jnp.exp(s_new - m)
        l = (jnp.sum(p_far, axis=-1, keepdims=True) + jnp.sum(p_near, axis=-1, keepdims=True)
             + jnp.sum(p_new, axis=-1, keepdims=True))
        acc = (jnp.dot(p_far.astype(BF16), cv[:far], preferred_element_type=F32)
               + jnp.dot(p_near.astype(BF16), cv[far:], preferred_element_type=F32)
               + jnp.dot(p_new.astype(BF16), vn_ref[0, :, lanes[h]], preferred_element_type=F32))
        o_ref[0, :, lanes[h]] = _diff_combine(acc, l, lam, g_ref[...], lam_init, ts).astype(o_ref.dtype)

    s = logits(0)
    for h in range(1, N_HEADS):
        s_next = logits(h)
        attend(h - 1, *s)
        s = s_next
    attend(N_HEADS - 1, *s)


def _sample_attention(q, kb, vb, cache_k, cache_v, rel_bias, lamv, subln_g, lam_init):
    s, ts, _ = q.shape
    past = cache_k.shape[1] // N_HEADS
    near = MAX_DISTANCE
    assert past >= near and past % V7X_SUBLANES == 0
    q_pos = past + jnp.arange(ts, dtype=jnp.int32)
    k_pos = (past - near) + jnp.arange(near + ts, dtype=jnp.int32)
    bias = _chunk_mask(_bias_band(rel_bias, ts, near + ts, -near), q_pos, k_pos)
    new = pl.BlockSpec((1, ts, ATTN_WIDTH), lambda si: (si, 0, 0))
    cache = pl.BlockSpec((1, past * N_HEADS, HEAD_V), lambda si: (si, 0, 0))
    return pl.pallas_call(
        functools.partial(_sample_attn_kernel, ts=ts, past=past, near=near, lam_init=lam_init),
        out_shape=jax.ShapeDtypeStruct(q.shape, BF16),
        grid=(s,),
        in_specs=[_resident(lamv.shape), _resident((1, HEAD_V)), new, cache, cache, new, new,
                  _resident(bias.shape)],
        out_specs=new,
        compiler_params=pltpu.CompilerParams(
            dimension_semantics=("arbitrary",), vmem_limit_bytes=V7X_VMEM_LIMIT_BYTES),
        name="sample_attention",
    )(lamv, subln_g.reshape(1, HEAD_V), q, cache_k, cache_v, kb, vb, bias)


def _finish_kernel(x_ref, a_ref, c_ref, g0_ref, b0_ref, wo_ref, g1_ref, b1_ref, w1_ref, w2_ref,
                   g2_ref, b2_ref, o_ref, *, alpha, chunks):
    rows = x_ref.shape[0] // chunks
    rs = [pl.ds(c * rows, rows) for c in range(chunks)]
    xn, y, x1, h, f = [], [], [], [], []
    for c in range(chunks):
        xn.append(_layer_norm(x_ref[rs[c], :], g0_ref[...], b0_ref[...]))
        mix = jnp.concatenate([a_ref[rs[c], :], c_ref[rs[c], :]], axis=-1)
        y.append(jnp.dot(mix, wo_ref[...], preferred_element_type=F32))
    for c in range(chunks):
        x1.append(_layer_norm(alpha * xn[c] + y[c], g1_ref[...], b1_ref[...]))
        h.append(jnp.dot(x1[c].astype(BF16), w1_ref[...], preferred_element_type=F32))
    for c in range(chunks):
        hc = jnp.square(jnp.maximum(h[c], 0.0)).astype(BF16)
        f.append(jnp.dot(hc, w2_ref[...], preferred_element_type=F32))
    for c in range(chunks):
        o_ref[rs[c], :] = _layer_norm(alpha * x1[c] + f[c], g2_ref[...], b2_ref[...])


def _finish(x2d, attn, conv, ln0_g, ln0_b, w_out_bf, ln1_g, ln1_b, w1_bf, w2_bf, ln2_g, ln2_b, alpha):
    n, d = x2d.shape
    a = attn.shape[1]
    c = conv.shape[1]
    f = w1_bf.shape[1]
    tm = min(ROW_TILE, n)
    assert n % tm == 0
    row = lambda w: pl.BlockSpec((tm, w), lambda i: (i, 0))
    vec = lambda p: p.reshape(1, d)
    return pl.pallas_call(
        functools.partial(_finish_kernel, alpha=alpha, chunks=FINISH_CHUNKS),
        out_shape=jax.ShapeDtypeStruct((n, d), F32),
        grid=(n // tm,),
        in_specs=[row(d), row(a), row(c), _resident((1, d)), _resident((1, d)), _resident((a + c, d)),
                  _resident((1, d)), _resident((1, d)), _resident((d, f)), _resident((f, d)),
                  _resident((1, d)), _resident((1, d))],
        out_specs=row(d),
        compiler_params=pltpu.CompilerParams(
            dimension_semantics=("arbitrary",), vmem_limit_bytes=V7X_VMEM_LIMIT_BYTES),
        name="finish",
    )(x2d, attn, conv, vec(ln0_g), vec(ln0_b), w_out_bf, vec(ln1_g), vec(ln1_b), w1_bf, w2_bf,
      vec(ln2_g), vec(ln2_b))


def kernel(x_prompt, x_sample, cache_k, cache_v, cache_conv, ln0_g, ln0_b, rel_bias, w_in, conv_w,
           lambda_q1, lambda_k1, lambda_q2, lambda_k2, subln_g, w_out, ln1_g, ln1_b,
           w_ff1, w_ff2, ln2_g, ln2_b):
    depth = w_in.shape[0]
    assert depth == 1, "single-layer step"
    layer = 0
    b, t, d = x_prompt.shape
    s, ts, _ = x_sample.shape
    past = cache_k.shape[2]
    c = conv_w.shape[-1]
    alpha = (2.0 * depth) ** 0.25
    lam_init = _lambda_init(layer)

    w_in_f = w_in[layer].astype(F32)
    lamv = jnp.stack([lambda_q1[layer], lambda_k1[layer], lambda_q2[layer], lambda_k2[layer]]).astype(F32)

    xp2 = x_prompt.reshape(b * t, d)
    hist0 = jnp.zeros((b, CONV_K - 1, c), F32)
    tq = min(Q_TILE, t)
    qp, kp, vp, kpb, vpb, convp, tailp, w_out_bf, w1_bf, w2_bf = _in_proj(
        xp2, t, ln0_g, ln0_b, w_in_f, conv_w[layer], hist0, qv_tile=tq,
        cast=(w_out[layer].astype(F32), w_ff1[layer].astype(F32), w_ff2[layer].astype(F32)))
    ffn = (ln0_g, ln0_b, w_out_bf, ln1_g[layer], ln1_b[layer], w1_bf, w2_bf, ln2_g[layer], ln2_b[layer], alpha)
    tiles = (b, t // tq, ATTN_WIDTH, tq)
    attnp = _prompt_attention(qp.reshape(tiles), kpb.reshape(b, t, ATTN_WIDTH), vpb.reshape(tiles), rel_bias,
                              lamv, subln_g[layer], lam_init)
    yp = _finish(xp2, attnp.reshape(b * t, ATTN_WIDTH), convp, *ffn).reshape(b, t, d)

    xs2 = x_sample.reshape(s * ts, d)
    qs, ks, vs, ksb, vsb, convs, tails = _in_proj(xs2, ts, ln0_g, ln0_b, w_in_f, conv_w[layer],
                                                  cache_conv[layer].astype(F32))
    shs = (s, ts, ATTN_WIDTH)
    attns = _sample_attention(qs.reshape(shs), ksb.reshape(shs), vsb.reshape(shs),
                              cache_k[layer].astype(F32).reshape(s, past * N_HEADS, HEAD_V),
                              cache_v[layer].astype(F32).reshape(s, past * N_HEADS, HEAD_V),
                              rel_bias, lamv, subln_g[layer], lam_init)
    ys = _finish(xs2, attns.reshape(s * ts, ATTN_WIDTH), convs, *ffn).reshape(s, ts, d)

    keep = CONV_K - 1
    return (yp, ys,
            kp.reshape(1, b, t, N_HEADS, HEAD_V), vp.reshape(1, b, t, N_HEADS, HEAD_V),
            tailp[:, V7X_SUBLANES - keep:, :].reshape(1, b, keep, c),
            ks.reshape(1, s, ts, N_HEADS, HEAD_V), vs.reshape(1, s, ts, N_HEADS, HEAD_V),
            tails[:, V7X_SUBLANES - keep:, :].reshape(1, s, keep, c))
```

```python
import functools
import math

import jax
import jax.numpy as jnp
from jax import lax
from jax.experimental import pallas as pl
from jax.experimental.pallas import tpu as pltpu

CHUNK = 64
N_HEADS = 4
HEAD_V = 128
HEAD_QK = HEAD_V // 2
ATTN_WIDTH = N_HEADS * HEAD_V
CONV_K = 3
N_BUCKETS = 32
MAX_DISTANCE = 128
LN_EPS = 1e-5
SUBLN_EPS = 1e-5

V7X_SUBLANES = 8
V7X_BF16_SUBLANES = 16
V7X_VMEM_LIMIT_BYTES = 56 * 1024 * 1024

ROW_TILE = 512
FINISH_CHUNKS = 2
Q_TILE = 256
K_BLOCK = 1024
HEADS_PER_STEP = 4
FAR_BLOCKS = 4
FAR_LAG = 2

F32 = jnp.float32
BF16 = jnp.bfloat16


def _lambda_init(layer):
    return 0.8 - 0.6 * math.exp(-0.3 * layer)


def _layer_norm(x, g, b):
    mu = jnp.mean(x, axis=-1, keepdims=True)
    xc = x - mu
    var = jnp.mean(xc * xc, axis=-1, keepdims=True)
    return xc * lax.rsqrt(var + LN_EPS) * g + b


def _resident(shape):
    nd = len(shape)
    return pl.BlockSpec(shape, lambda *_: (0,) * nd, pipeline_mode=pl.Buffered(1))


def _in_proj_kernel(x_ref, g_ref, b_ref, w_ref, cw_ref, hist_ref, *rest, seg_len, tm, qv_tile, n_cast):
    cast_in, rest = rest[:n_cast], rest[n_cast:]
    q_ref, k_ref, v_ref, kb_ref, vb_ref, conv_ref, utail_ref = rest[:7]
    cast_out, (carry_ref,) = rest[7:7 + n_cast], rest[7 + n_cast:]
    a = ATTN_WIDTH
    c = (w_ref.shape[1] - 3 * a) // 3
    for src, dst in zip(cast_in, cast_out):
        dst[...] = src[...].astype(BF16)
    xn = _layer_norm(x_ref[...], g_ref[...], b_ref[...])
    z = jnp.dot(xn, w_ref[...], preferred_element_type=F32)
    gb = z[:, 3 * a:3 * a + c]
    u = z[:, 3 * a + c:3 * a + 2 * c] * z[:, 3 * a + 2 * c:]
    q = z[:, :a] * (HEAD_QK ** -0.5)
    k = z[:, a:2 * a]
    v = z[:, 2 * a:3 * a]
    for h in range(N_HEADS):
        k_ref[pl.ds(h, tm, stride=N_HEADS), :] = k[:, h * HEAD_V:(h + 1) * HEAD_V]
        v_ref[pl.ds(h, tm, stride=N_HEADS), :] = v[:, h * HEAD_V:(h + 1) * HEAD_V]
    kb_ref[...] = k.astype(BF16)
    if qv_tile is None:
        q_ref[...] = q.astype(BF16)
        vb_ref[...] = v.astype(BF16)
    else:
        for s in range(tm // qv_tile):
            rows_s = slice(s * qv_tile, (s + 1) * qv_tile)
            q_ref[s] = q[rows_s, :].T.astype(BF16)
            vb_ref[s] = v[rows_s, :].T.astype(BF16)
    rows = lax.broadcasted_iota(jnp.int32, (tm, 1), 0)
    if seg_len >= tm:
        tiles_per_seg = seg_len // tm
        first = (pl.program_id(0) % tiles_per_seg) == 0
        hist = hist_ref[0]
        prev = carry_ref[...]
        p1 = jnp.where(first, hist[1:2, :], prev[V7X_SUBLANES - 1:, :])
        p0 = jnp.where(first, hist[0:1, :], prev[V7X_SUBLANES - 2:V7X_SUBLANES - 1, :])
        pos = rows
        tail = u[tm - V7X_SUBLANES:, :]
        carry_ref[...] = tail
        utail_ref[0] = tail
    else:
        nseg = tm // seg_len
        hist = hist_ref[...]
        p1 = jnp.broadcast_to(hist[:, 1:2, :], (nseg, seg_len, c)).reshape(tm, c)
        p0 = jnp.broadcast_to(hist[:, 0:1, :], (nseg, seg_len, c)).reshape(tm, c)
        pos = rows & (seg_len - 1)
        utail_ref[...] = u.reshape(nseg, seg_len, c)[:, seg_len - V7X_SUBLANES:, :]
    u1 = jnp.where(pos == 0, p1, pltpu.roll(u, 1, 0))
    u2 = jnp.where(pos == 0, p0, jnp.where(pos == 1, p1, pltpu.roll(u, 2, 0)))
    cw = cw_ref[...]
    conv = gb * (u2 * cw[0:1, :] + u1 * cw[1:2, :] + u * cw[2:3, :])
    conv_ref[...] = conv.astype(BF16)


def _in_proj(x2d, seg_len, ln_g, ln_b, w_in, conv_w, hist, qv_tile=None, cast=()):
    n, d = x2d.shape
    e = w_in.shape[1]
    a = ATTN_WIDTH
    c = (e - 3 * a) // 3
    tm = min(ROW_TILE, n)
    assert n % tm == 0 and (seg_len % tm == 0 or tm % seg_len == 0)
    assert seg_len & (seg_len - 1) == 0 and seg_len % V7X_SUBLANES == 0
    nseq = n // seg_len
    if seg_len >= tm:
        tiles_per_seg = seg_len // tm
        hist_spec = pl.BlockSpec((1, CONV_K - 1, c), lambda i: (i // tiles_per_seg, 0, 0))
        tail_spec = pl.BlockSpec((1, V7X_SUBLANES, c), lambda i: (i // tiles_per_seg, 0, 0))
    else:
        nseg = tm // seg_len
        hist_spec = pl.BlockSpec((nseg, CONV_K - 1, c), lambda i: (i, 0, 0))
        tail_spec = pl.BlockSpec((nseg, V7X_SUBLANES, c), lambda i: (i, 0, 0))
    row = lambda w: pl.BlockSpec((tm, w), lambda i: (i, 0))
    headrow = pl.BlockSpec((tm * N_HEADS, HEAD_V), lambda i: (i, 0))
    if qv_tile is None:
        qv_shape = jax.ShapeDtypeStruct((n, a), BF16)
        qv_spec = row(a)
    else:
        assert tm % qv_tile == 0
        qv_shape = jax.ShapeDtypeStruct((n // qv_tile, a, qv_tile), BF16)
        qv_spec = pl.BlockSpec((tm // qv_tile, a, qv_tile), lambda i: (i, 0, 0))
    out_shape = (
        qv_shape,
        jax.ShapeDtypeStruct((n * N_HEADS, HEAD_V), F32),
        jax.ShapeDtypeStruct((n * N_HEADS, HEAD_V), F32),
        jax.ShapeDtypeStruct((n, a), BF16),
        qv_shape,
        jax.ShapeDtypeStruct((n, c), BF16),
        jax.ShapeDtypeStruct((nseq, V7X_SUBLANES, c), F32),
    ) + tuple(jax.ShapeDtypeStruct(m.shape, BF16) for m in cast)
    steps = n // tm
    slab_specs = []
    for m in cast:
        assert m.shape[0] % (steps * V7X_BF16_SUBLANES) == 0
        slab_specs.append(pl.BlockSpec((m.shape[0] // steps, m.shape[1]), lambda i: (i, 0)))
    return pl.pallas_call(
        functools.partial(_in_proj_kernel, seg_len=seg_len, tm=tm, qv_tile=qv_tile, n_cast=len(cast)),
        out_shape=out_shape,
        grid=(steps,),
        in_specs=[row(d), _resident((1, d)), _resident((1, d)), _resident((d, e)),
                  _resident((CONV_K, c)), hist_spec, *slab_specs],
        out_specs=(qv_spec, headrow, headrow, row(a), qv_spec, row(c), tail_spec, *slab_specs),
        scratch_shapes=[pltpu.VMEM((V7X_SUBLANES, c), F32)],
        compiler_params=pltpu.CompilerParams(
            dimension_semantics=("arbitrary",), vmem_limit_bytes=V7X_VMEM_LIMIT_BYTES),
        name="in_proj",
    )(x2d, ln_g.reshape(1, d), ln_b.reshape(1, d), w_in, conv_w, hist, *cast)


def _stack_maps(q):
    lane = lax.broadcasted_iota(jnp.int32, q.shape, 1)
    zero = jnp.zeros_like(q)
    return jnp.concatenate([jnp.where(lane < HEAD_QK, q, zero), jnp.where(lane < HEAD_QK, zero, q)], axis=0)


def _scores(qz, kblk):
    return lax.dot_general(qz, kblk, (((1,), (1,)), ((), ())), preferred_element_type=F32)


def _add_to_both_maps(s, bias):
    t, n = bias.shape
    return (s.reshape(2, t, n) + bias[None]).reshape(2 * t, n)


def _lambda(lamv_ref, lam_init):
    lv = lamv_ref[...]
    s1 = jnp.sum(lv[0:1, :] * lv[1:2, :], axis=-1, keepdims=True)
    s2 = jnp.sum(lv[2:3, :] * lv[3:4, :], axis=-1, keepdims=True)
    return jnp.exp(s1) - jnp.exp(s2) + lam_init


def _diff_combine(acc, l, lam, g, lam_init, t):
    o = acc[:t] / l[:t] - lam * (acc[t:] / l[t:])
    ms = jnp.mean(o * o, axis=-1, keepdims=True)
    return o * lax.rsqrt(ms + SUBLN_EPS) * g * (1.0 - lam_init)


def _rel_bucket(rel):
    half = N_BUCKETS // 2
    max_exact = half // 2
    ret = jnp.where(rel > 0, half, 0)
    n = jnp.abs(rel)
    nf = jnp.maximum(n, 1).astype(F32)
    large = max_exact + (jnp.log(nf / max_exact) / math.log(MAX_DISTANCE / max_exact)
                         * (half - max_exact)).astype(jnp.int32)
    large = jnp.minimum(large, half - 1)
    return ret + jnp.where(n < max_exact, n, large)


def _bias_band(rel_bias, n_q, n_k, rel00):
    length = n_q + n_k
    rel = rel00 - (n_q - 1) + jnp.arange(length, dtype=jnp.int32)
    far = rel_bias[N_BUCKETS // 2 - 1].astype(F32)
    w = (rel_bias[_rel_bucket(rel)].astype(F32) - far[None, :]).T
    skew = jnp.tile(w, (1, n_q))[:, :n_q * (length - 1)].reshape(-1, n_q, length - 1)
    return skew[:, :, n_q - 1:n_q - 1 + n_k]


def _chunk_mask(tile, q_pos, k_pos):
    visible = (k_pos[None, :] // CHUNK) <= (q_pos[:, None] // CHUNK)
    return jnp.where(visible[None], tile, -jnp.inf)


SUM_LO = 2.0 ** -60
SUM_HI = 2.0 ** 60


def _prompt_attn_kernel(lamv_ref, g_ref, q_ref, k_ref, v_ref, bd_ref, bp_ref, o_ref,
                        acc_ref, lp_ref, l_ref, m_ref, *, tq, tk, hp, lam_init):
    sub = tk // tq
    qt = pl.program_id(2)
    feats = [slice(h * HEAD_V, (h + 1) * HEAD_V) for h in range(hp)]

    def stack_maps(qT):
        row = lax.broadcasted_iota(jnp.int32, qT.shape, 0)
        zero = jnp.zeros_like(qT)
        return jnp.concatenate([jnp.where(row < HEAD_QK, qT, zero), jnp.where(row < HEAD_QK, zero, qT)], axis=1)

    qz = [stack_maps(q_ref[0, 0, ft, :]) for ft in feats]

    def kv(j, h):
        start = pl.multiple_of(j * tq, tq)
        return k_ref[0, pl.ds(start, tq), feats[h]], v_ref[0, j, feats[h], :]

    def logits(h, kj, bias):
        s = jnp.dot(kj, qz[h], preferred_element_type=F32)
        if bias is not None:
            s = s + jnp.concatenate([bias, bias], axis=1)
        return s

    def sublane_fold(p):
        return jnp.sum(p.reshape(p.shape[0] // V7X_SUBLANES, V7X_SUBLANES, p.shape[1]), axis=0)

    def numerators(chains, lag, first=False):
        totals = {}
        pending = []

        def values(h, vj, p):
            pv = jnp.dot(vj, p.astype(BF16), preferred_element_type=F32)
            lp = sublane_fold(p)
            totals[h] = (pv, lp) if h not in totals else (totals[h][0] + pv, totals[h][1] + lp)

        for h, j, bias in chains:
            kj, vj = kv(j, h)
            pending.append((h, vj, jnp.exp(logits(h, kj, bias))))
            if len(pending) > lag:
                values(*pending.pop(0))
        while pending:
            values(*pending.pop(0))
        for h, (pv, lp) in totals.items():
            if first:
                acc_ref[h] = pv
                lp_ref[h] = lp
            else:
                acc_ref[h] += pv
                lp_ref[h] += lp

    heads = range(hp)
    n_far = jnp.maximum(qt - 1, 0)
    n_lead = n_far % sub
    n_blocks = n_far // sub

    @pl.when(qt == 0)
    def _():
        numerators([(h, qt, bd_ref[h]) for h in heads], hp, first=True)

    for lead in range(sub):
        @pl.when((qt > 0) & (n_lead == lead))
        def _(lead=lead):
            near = ((0, bd_ref), (1, bp_ref))
            numerators([(h, qt - d, None if b_ref is None else b_ref[h]) for h in heads
                        for d, b_ref in near + tuple((2 + f, None) for f in range(lead))], hp, first=True)

    def far_region(first_block, blocks):
        numerators([(h, first_block * sub + c, None) for c in range(blocks * sub) for h in heads], FAR_LAG)

    def quad(jb, carry):
        far_region(jb * FAR_BLOCKS, FAR_BLOCKS)
        return carry

    lax.fori_loop(0, n_blocks // FAR_BLOCKS, quad, 0)

    for rem in range(1, FAR_BLOCKS):
        @pl.when(n_blocks % FAR_BLOCKS == rem)
        def _(rem=rem):
            far_region(n_blocks - rem, rem)

    for h in range(hp):
        l_ref[h] = jnp.sum(lp_ref[h], axis=0, keepdims=True)
    l_all = l_ref[...]
    in_range = (l_all >= SUM_LO) & (l_all <= SUM_HI)

    @pl.when(jnp.min(jnp.where(in_range, 1.0, 0.0)) < 0.5)
    def _():
        for h in range(hp):
            kd, vd = kv(qt, h)
            s = logits(h, kd, bd_ref[h])
            m0 = jnp.max(s, axis=0, keepdims=True)
            p = jnp.exp(s - m0)
            m_ref[h] = m0
            l_ref[h] = jnp.sum(p, axis=0, keepdims=True)
            acc_ref[h] = jnp.dot(vd, p.astype(BF16), preferred_element_type=F32)

            def step(j, carry, h=h):
                kj, vj = kv(j, h)
                s = logits(h, kj, jnp.where(j == qt - 1, bp_ref[h], jnp.zeros_like(bp_ref[h])))
                m_prev = m_ref[h]
                m_new = jnp.maximum(m_prev, jnp.max(s, axis=0, keepdims=True))
                alpha = jnp.exp(m_prev - m_new)
                p = jnp.exp(s - m_new)
                l_ref[h] = alpha * l_ref[h] + jnp.sum(p, axis=0, keepdims=True)
                acc_ref[h] = alpha * acc_ref[h] + jnp.dot(vj, p.astype(BF16), preferred_element_type=F32)
                m_ref[h] = m_new
                return carry

            lax.fori_loop(0, qt, step, 0)

    lam = _lambda(lamv_ref, lam_init)
    for h in range(hp):
        o = acc_ref[h] / l_ref[h]
        o = o[:, :tq] - lam * o[:, tq:]
        ms = jnp.mean(o * o, axis=0, keepdims=True)
        o = o * lax.rsqrt(ms + SUBLN_EPS) * g_ref[...] * (1.0 - lam_init)
        o_ref[0, :, feats[h]] = o.T.astype(o_ref.dtype)


def _prompt_attention(qT, kb, vT, rel_bias, lamv, subln_g, lam_init):
    b, t, _ = kb.shape
    tq = qT.shape[-1]
    tk = min(K_BLOCK, t)
    sub = tk // tq
    assert t % tq == 0 and tk % tq == 0 and tq % CHUNK == 0 and tq >= MAX_DISTANCE
    pos = jnp.arange(tq, dtype=jnp.int32)
    band = _bias_band(rel_bias, tq, 2 * tq, -tq)
    prev = jnp.swapaxes(band[:, :, :tq], 1, 2)
    diag = jnp.swapaxes(_chunk_mask(band[:, :, tq:], pos, pos), 1, 2)
    gain = jnp.broadcast_to(subln_g.astype(F32)[:, None], (HEAD_V, tq))
    hp = HEADS_PER_STEP
    assert N_HEADS % hp == 0
    qspec = pl.BlockSpec((1, 1, hp * HEAD_V, tq), lambda bi, g, i: (bi, i, g, 0))
    kspec = pl.BlockSpec((1, t, hp * HEAD_V), lambda bi, g, i: (bi, 0, g))
    vspec = pl.BlockSpec((1, t // tq, hp * HEAD_V, tq), lambda bi, g, i: (bi, 0, g, 0))
    bspec = pl.BlockSpec((hp, tq, tq), lambda bi, g, i: (g, 0, 0))
    return pl.pallas_call(
        functools.partial(_prompt_attn_kernel, tq=tq, tk=tk, hp=hp, lam_init=lam_init),
        out_shape=jax.ShapeDtypeStruct(kb.shape, BF16),
        grid=(b, N_HEADS // hp, t // tq),
        in_specs=[_resident(lamv.shape), _resident((HEAD_V, tq)), qspec, kspec, vspec, bspec, bspec],
        out_specs=pl.BlockSpec((1, tq, hp * HEAD_V), lambda bi, g, i: (bi, i, g)),
        scratch_shapes=[pltpu.VMEM((hp, HEAD_V, 2 * tq), F32), pltpu.VMEM((hp, V7X_SUBLANES, 2 * tq), F32),
                        pltpu.VMEM((hp, 1, 2 * tq), F32), pltpu.VMEM((hp, 1, 2 * tq), F32)],
        compiler_params=pltpu.CompilerParams(
            dimension_semantics=("arbitrary", "arbitrary", "arbitrary"),
            vmem_limit_bytes=V7X_VMEM_LIMIT_BYTES),
        name="prompt_attention",
    )(lamv, gain, qT, kb, vT, diag, prev)


def _sample_attn_kernel(lamv_ref, g_ref, q_ref, ck_ref, cv_ref, kn_ref, vn_ref, bias_ref, o_ref,
                        *, ts, past, near, lam_init):
    far = past - near
    lam = _lambda(lamv_ref, lam_init)
    lanes = [slice(h * HEAD_V, (h + 1) * HEAD_V) for h in range(N_HEADS)]

    def logits(h):
        qz = _stack_maps(q_ref[0, :, lanes[h]])
        ck = ck_ref[0, pl.ds(h, past, stride=N_HEADS), :].astype(BF16)
        bias = bias_ref[h]
        return (_scores(qz, ck[:far]),
                _add_to_both_maps(_scores(qz, ck[far:]), bias[:, :near]),
                _add_to_both_maps(_scores(qz, kn_ref[0, :, lanes[h]]), bias[:, near:]))

    def attend(h, s_far, s_near, s_new):
        cv = cv_ref[0, pl.ds(h, past, stride=N_HEADS), :].astype(BF16)
        m = jnp.maximum(jnp.max(s_far, axis=-1, keepdims=True),
                        jnp.maximum(jnp.max(s_near, axis=-1, keepdims=True),
                                    jnp.max(s_new, axis=-1, keepdims=True)))
        p_far = jnp.exp(s_far - m)
        p_near = jnp.exp(s_near - m)
        p_new = jnp.exp(s_new - m)
        l = (jnp.sum(p_far, axis=-1, keepdims=True) + jnp.sum(p_near, axis=-1, keepdims=True)
             + jnp.sum(p_new, axis=-1, keepdims=True))
        acc = (jnp.dot(p_far.astype(BF16), cv[:far], preferred_element_type=F32)
               + jnp.dot(p_near.astype(BF16), cv[far:], preferred_element_type=F32)
               + jnp.dot(p_new.astype(BF16), vn_ref[0, :, lanes[h]], preferred_element_type=F32))
        o_ref[0, :, lanes[h]] = _diff_combine(acc, l, lam, g_ref[...], lam_init, ts).astype(o_ref.dtype)

    s = logits(0)
    for h in range(1, N_HEADS):
        s_next = logits(h)
        attend(h - 1, *s)
        s = s_next
    attend(N_HEADS - 1, *s)


def _sample_attention(q, kb, vb, cache_k, cache_v, rel_bias, lamv, subln_g, lam_init):
    s, ts, _ = q.shape
    past = cache_k.shape[1] // N_HEADS
    near = MAX_DISTANCE
    assert past >= near and past % V7X_SUBLANES == 0
    q_pos = past + jnp.arange(ts, dtype=jnp.int32)
    k_pos = (past - near) + jnp.arange(near + ts, dtype=jnp.int32)
    bias = _chunk_mask(_bias_band(rel_bias, ts, near + ts, -near), q_pos, k_pos)
    new = pl.BlockSpec((1, ts, ATTN_WIDTH), lambda si: (si, 0, 0))
    cache = pl.BlockSpec((1, past * N_HEADS, HEAD_V), lambda si: (si, 0, 0))
    return pl.pallas_call(
        functools.partial(_sample_attn_kernel, ts=ts, past=past, near=near, lam_init=lam_init),
        out_shape=jax.ShapeDtypeStruct(q.shape, BF16),
        grid=(s,),
        in_specs=[_resident(lamv.shape), _resident((1, HEAD_V)), new, cache, cache, new, new,
                  _resident(bias.shape)],
        out_specs=new,
        compiler_params=pltpu.CompilerParams(
            dimension_semantics=("arbitrary",), vmem_limit_bytes=V7X_VMEM_LIMIT_BYTES),
        name="sample_attention",
    )(lamv, subln_g.reshape(1, HEAD_V), q, cache_k, cache_v, kb, vb, bias)


def _finish_kernel(x_ref, a_ref, c_ref, g0_ref, b0_ref, wo_ref, g1_ref, b1_ref, w1_ref, w2_ref,
                   g2_ref, b2_ref, o_ref, *, alpha, chunks):
    rows = x_ref.shape[0] // chunks
    rs = [pl.ds(c * rows, rows) for c in range(chunks)]
    xn, y, x1, h, f = [], [], [], [], []
    for c in range(chunks):
        xn.append(_layer_norm(x_ref[rs[c], :], g0_ref[...], b0_ref[...]))
        mix = jnp.concatenate([a_ref[rs[c], :], c_ref[rs[c], :]], axis=-1)
        y.append(jnp.dot(mix, wo_ref[...], preferred_element_type=F32))
    for c in range(chunks):
        x1.append(_layer_norm(alpha * xn[c] + y[c], g1_ref[...], b1_ref[...]))
        h.append(jnp.dot(x1[c].astype(BF16), w1_ref[...], preferred_element_type=F32))
    for c in range(chunks):
        hc = jnp.square(jnp.maximum(h[c], 0.0)).astype(BF16)
        f.append(jnp.dot(hc, w2_ref[...], preferred_element_type=F32))
    for c in range(chunks):
        o_ref[rs[c], :] = _layer_norm(alpha * x1[c] + f[c], g2_ref[...], b2_ref[...])


def _finish(x2d, attn, conv, ln0_g, ln0_b, w_out_bf, ln1_g, ln1_b, w1_bf, w2_bf, ln2_g, ln2_b, alpha):
    n, d = x2d.shape
    a = attn.shape[1]
    c = conv.shape[1]
    f = w1_bf.shape[1]
    tm = min(ROW_TILE, n)
    assert n % tm == 0
    row = lambda w: pl.BlockSpec((tm, w), lambda i: (i, 0))
    vec = lambda p: p.reshape(1, d)
    return pl.pallas_call(
        functools.partial(_finish_kernel, alpha=alpha, chunks=FINISH_CHUNKS),
        out_shape=jax.ShapeDtypeStruct((n, d), F32),
        grid=(n // tm,),
        in_specs=[row(d), row(a), row(c), _resident((1, d)), _resident((1, d)), _resident((a + c, d)),
                  _resident((1, d)), _resident((1, d)), _resident((d, f)), _resident((f, d)),
                  _resident((1, d)), _resident((1, d))],
        out_specs=row(d),
        compiler_params=pltpu.CompilerParams(
            dimension_semantics=("arbitrary",), vmem_limit_bytes=V7X_VMEM_LIMIT_BYTES),
        name="finish",
    )(x2d, attn, conv, vec(ln0_g), vec(ln0_b), w_out_bf, vec(ln1_g), vec(ln1_b), w1_bf, w2_bf,
      vec(ln2_g), vec(ln2_b))


def kernel(x_prompt, x_sample, cache_k, cache_v, cache_conv, ln0_g, ln0_b, rel_bias, w_in, conv_w,
           lambda_q1, lambda_k1, lambda_q2, lambda_k2, subln_g, w_out, ln1_g, ln1_b,
           w_ff1, w_ff2, ln2_g, ln2_b):
    depth = w_in.shape[0]
    assert depth == 1, "single-layer step"
    layer = 0
    b, t, d = x_prompt.shape
    s, ts, _ = x_sample.shape
    past = cache_k.shape[2]
    c = conv_w.shape[-1]
    alpha = (2.0 * depth) ** 0.25
    lam_init = _lambda_init(layer)

    w_in_f = w_in[layer].astype(F32)
    lamv = jnp.stack([lambda_q1[layer], lambda_k1[layer], lambda_q2[layer], lambda_k2[layer]]).astype(F32)

    xp2 = x_prompt.reshape(b * t, d)
    hist0 = jnp.zeros((b, CONV_K - 1, c), F32)
    tq = min(Q_TILE, t)
    qp, kp, vp, kpb, vpb, convp, tailp, w_out_bf, w1_bf, w2_bf = _in_proj(
        xp2, t, ln0_g, ln0_b, w_in_f, conv_w[layer], hist0, qv_tile=tq,
        cast=(w_out[layer].astype(F32), w_ff1[layer].astype(F32), w_ff2[layer].astype(F32)))
    ffn = (ln0_g, ln0_b, w_out_bf, ln1_g[layer], ln1_b[layer], w1_bf, w2_bf, ln2_g[layer], ln2_b[layer], alpha)
    tiles = (b, t // tq, ATTN_WIDTH, tq)
    attnp = _prompt_attention(qp.reshape(tiles), kpb.reshape(b, t, ATTN_WIDTH), vpb.reshape(tiles), rel_bias,
                              lamv, subln_g[layer], lam_init)
    yp = _finish(xp2, attnp.reshape(b * t, ATTN_WIDTH), convp, *ffn).reshape(b, t, d)

    xs2 = x_sample.reshape(s * ts, d)
    qs, ks, vs, ksb, vsb, convs, tails = _in_proj(xs2, ts, ln0_g, ln0_b, w_in_f, conv_w[layer],
                                                  cache_conv[layer].astype(F32))
    shs = (s, ts, ATTN_WIDTH)
    attns = _sample_attention(qs.reshape(shs), ksb.reshape(shs), vsb.reshape(shs),
                              cache_k[layer].astype(F32).reshape(s, past * N_HEADS, HEAD_V),
                              cache_v[layer].astype(F32).reshape(s, past * N_HEADS, HEAD_V),
                              rel_bias, lamv, subln_g[layer], lam_init)
    ys = _finish(xs2, attns.reshape(s * ts, ATTN_WIDTH), convs, *ffn).reshape(s, ts, d)

    keep = CONV_K - 1
    return (yp, ys,
            kp.reshape(1, b, t, N_HEADS, HEAD_V), vp.reshape(1, b, t, N_HEADS, HEAD_V),
            tailp[:, V7X_SUBLANES - keep:, :].reshape(1, b, keep, c),
            ks.reshape(1, s, ts, N_HEADS, HEAD_V), vs.reshape(1, s, ts, N_HEADS, HEAD_V),
            tails[:, V7X_SUBLANES - keep:, :].reshape(1, s, keep, c))
```

```python
import functools
import math

import jax
import jax.numpy as jnp
from jax import lax
from jax.experimental import pallas as pl
from jax.experimental.pallas import tpu as pltpu

CHUNK = 64
N_HEADS = 4
HEAD_V = 128
HEAD_QK = HEAD_V // 2
ATTN_WIDTH = N_HEADS * HEAD_V
CONV_K = 3
N_BUCKETS = 32
MAX_DISTANCE = 128
LN_EPS = 1e-5
SUBLN_EPS = 1e-5

V7X_SUBLANES = 8
V7X_BF16_SUBLANES = 16
V7X_VMEM_LIMIT_BYTES = 56 * 1024 * 1024

ROW_TILE = 512
FINISH_CHUNKS = 2
Q_TILE = 256
K_BLOCK = 1024
HEADS_PER_STEP = 4
TILES_PER_STEP = 4
FAR_BLOCKS = 4
FAR_LAG = 2

F32 = jnp.float32
BF16 = jnp.bfloat16


def _lambda_init(layer):
    return 0.8 - 0.6 * math.exp(-0.3 * layer)


def _layer_norm(x, g, b):
    mu = jnp.mean(x, axis=-1, keepdims=True)
    xc = x - mu
    var = jnp.mean(xc * xc, axis=-1, keepdims=True)
    return xc * lax.rsqrt(var + LN_EPS) * g + b


def _resident(shape):
    nd = len(shape)
    return pl.BlockSpec(shape, lambda *_: (0,) * nd, pipeline_mode=pl.Buffered(1))


def _in_proj_kernel(x_ref, g_ref, b_ref, w_ref, cw_ref, hist_ref, *rest, seg_len, tm, qv_tile, n_cast):
    cast_in, rest = rest[:n_cast], rest[n_cast:]
    q_ref, k_ref, v_ref, kb_ref, vb_ref, conv_ref, utail_ref = rest[:7]
    cast_out, (carry_ref,) = rest[7:7 + n_cast], rest[7 + n_cast:]
    a = ATTN_WIDTH
    c = (w_ref.shape[1] - 3 * a) // 3
    for src, dst in zip(cast_in, cast_out):
        dst[...] = src[...].astype(BF16)
    xn = _layer_norm(x_ref[...], g_ref[...], b_ref[...])
    z = jnp.dot(xn, w_ref[...], preferred_element_type=F32)
    gb = z[:, 3 * a:3 * a + c]
    u = z[:, 3 * a + c:3 * a + 2 * c] * z[:, 3 * a + 2 * c:]
    q = z[:, :a] * (HEAD_QK ** -0.5)
    k = z[:, a:2 * a]
    v = z[:, 2 * a:3 * a]
    for h in range(N_HEADS):
        k_ref[pl.ds(h, tm, stride=N_HEADS), :] = k[:, h * HEAD_V:(h + 1) * HEAD_V]
        v_ref[pl.ds(h, tm, stride=N_HEADS), :] = v[:, h * HEAD_V:(h + 1) * HEAD_V]
    kb_ref[...] = k.astype(BF16)
    if qv_tile is None:
        q_ref[...] = q.astype(BF16)
        vb_ref[...] = v.astype(BF16)
    else:
        for s in range(tm // qv_tile):
            rows_s = slice(s * qv_tile, (s + 1) * qv_tile)
            q_ref[s] = q[rows_s, :].T.astype(BF16)
            vb_ref[s] = v[rows_s, :].T.astype(BF16)
    rows = lax.broadcasted_iota(jnp.int32, (tm, 1), 0)
    if seg_len >= tm:
        tiles_per_seg = seg_len // tm
        first = (pl.program_id(0) % tiles_per_seg) == 0
        hist = hist_ref[0]
        prev = carry_ref[...]
        p1 = jnp.where(first, hist[1:2, :], prev[V7X_SUBLANES - 1:, :])
        p0 = jnp.where(first, hist[0:1, :], prev[V7X_SUBLANES - 2:V7X_SUBLANES - 1, :])
        pos = rows
        tail = u[tm - V7X_SUBLANES:, :]
        carry_ref[...] = tail
        utail_ref[0] = tail
    else:
        nseg = tm // seg_len
        hist = hist_ref[...]
        p1 = jnp.broadcast_to(hist[:, 1:2, :], (nseg, seg_len, c)).reshape(tm, c)
        p0 = jnp.broadcast_to(hist[:, 0:1, :], (nseg, seg_len, c)).reshape(tm, c)
        pos = rows & (seg_len - 1)
        utail_ref[...] = u.reshape(nseg, seg_len, c)[:, seg_len - V7X_SUBLANES:, :]
    u1 = jnp.where(pos == 0, p1, pltpu.roll(u, 1, 0))
    u2 = jnp.where(pos == 0, p0, jnp.where(pos == 1, p1, pltpu.roll(u, 2, 0)))
    cw = cw_ref[...]
    conv = gb * (u2 * cw[0:1, :] + u1 * cw[1:2, :] + u * cw[2:3, :])
    conv_ref[...] = conv.astype(BF16)


def _in_proj(x2d, seg_len, ln_g, ln_b, w_in, conv_w, hist, qv_tile=None, cast=()):
    n, d = x2d.shape
    e = w_in.shape[1]
    a = ATTN_WIDTH
    c = (e - 3 * a) // 3
    tm = min(ROW_TILE, n)
    assert n % tm == 0 and (seg_len % tm == 0 or tm % seg_len == 0)
    assert seg_len & (seg_len - 1) == 0 and seg_len % V7X_SUBLANES == 0
    nseq = n // seg_len
    if seg_len >= tm:
        tiles_per_seg = seg_len // tm
        hist_spec = pl.BlockSpec((1, CONV_K - 1, c), lambda i: (i // tiles_per_seg, 0, 0))
        tail_spec = pl.BlockSpec((1, V7X_SUBLANES, c), lambda i: (i // tiles_per_seg, 0, 0))
    else:
        nseg = tm // seg_len
        hist_spec = pl.BlockSpec((nseg, CONV_K - 1, c), lambda i: (i, 0, 0))
        tail_spec = pl.BlockSpec((nseg, V7X_SUBLANES, c), lambda i: (i, 0, 0))
    row = lambda w: pl.BlockSpec((tm, w), lambda i: (i, 0))
    headrow = pl.BlockSpec((tm * N_HEADS, HEAD_V), lambda i: (i, 0))
    if qv_tile is None:
        qv_shape = jax.ShapeDtypeStruct((n, a), BF16)
        qv_spec = row(a)
    else:
        assert tm % qv_tile == 0
        qv_shape = jax.ShapeDtypeStruct((n // qv_tile, a, qv_tile), BF16)
        qv_spec = pl.BlockSpec((tm // qv_tile, a, qv_tile), lambda i: (i, 0, 0))
    out_shape = (
        qv_shape,
        jax.ShapeDtypeStruct((n * N_HEADS, HEAD_V), F32),
        jax.ShapeDtypeStruct((n * N_HEADS, HEAD_V), F32),
        jax.ShapeDtypeStruct((n, a), BF16),
        qv_shape,
        jax.ShapeDtypeStruct((n, c), BF16),
        jax.ShapeDtypeStruct((nseq, V7X_SUBLANES, c), F32),
    ) + tuple(jax.ShapeDtypeStruct(m.shape, BF16) for m in cast)
    steps = n // tm
    slab_specs = []
    for m in cast:
        assert m.shape[0] % (steps * V7X_BF16_SUBLANES) == 0
        slab_specs.append(pl.BlockSpec((m.shape[0] // steps, m.shape[1]), lambda i: (i, 0)))
    return pl.pallas_call(
        functools.partial(_in_proj_kernel, seg_len=seg_len, tm=tm, qv_tile=qv_tile, n_cast=len(cast)),
        out_shape=out_shape,
        grid=(steps,),
        in_specs=[row(d), _resident((1, d)), _resident((1, d)), _resident((d, e)),
                  _resident((CONV_K, c)), hist_spec, *slab_specs],
        out_specs=(qv_spec, headrow, headrow, row(a), qv_spec, row(c), tail_spec, *slab_specs),
        scratch_shapes=[pltpu.VMEM((V7X_SUBLANES, c), F32)],
        compiler_params=pltpu.CompilerParams(
            dimension_semantics=("arbitrary",), vmem_limit_bytes=V7X_VMEM_LIMIT_BYTES),
        name="in_proj",
    )(x2d, ln_g.reshape(1, d), ln_b.reshape(1, d), w_in, conv_w, hist, *cast)


def _stack_maps(q):
    lane = lax.broadcasted_iota(jnp.int32, q.shape, 1)
    zero = jnp.zeros_like(q)
    return jnp.concatenate([jnp.where(lane < HEAD_QK, q, zero), jnp.where(lane < HEAD_QK, zero, q)], axis=0)


def _scores(qz, kblk):
    return lax.dot_general(qz, kblk, (((1,), (1,)), ((), ())), preferred_element_type=F32)


def _add_to_both_maps(s, bias):
    t, n = bias.shape
    return (s.reshape(2, t, n) + bias[None]).reshape(2 * t, n)


def _lambda(lamv_ref, lam_init):
    lv = lamv_ref[...]
    s1 = jnp.sum(lv[0:1, :] * lv[1:2, :], axis=-1, keepdims=True)
    s2 = jnp.sum(lv[2:3, :] * lv[3:4, :], axis=-1, keepdims=True)
    return jnp.exp(s1) - jnp.exp(s2) + lam_init


def _diff_combine(acc, l, lam, g, lam_init, t):
    o = acc[:t] / l[:t] - lam * (acc[t:] / l[t:])
    ms = jnp.mean(o * o, axis=-1, keepdims=True)
    return o * lax.rsqrt(ms + SUBLN_EPS) * g * (1.0 - lam_init)


def _rel_bucket(rel):
    half = N_BUCKETS // 2
    max_exact = half // 2
    ret = jnp.where(rel > 0, half, 0)
    n = jnp.abs(rel)
    nf = jnp.maximum(n, 1).astype(F32)
    large = max_exact + (jnp.log(nf / max_exact) / math.log(MAX_DISTANCE / max_exact)
                         * (half - max_exact)).astype(jnp.int32)
    large = jnp.minimum(large, half - 1)
    return ret + jnp.where(n < max_exact, n, large)


def _bias_band(rel_bias, n_q, n_k, rel00):
    length = n_q + n_k
    rel = rel00 - (n_q - 1) + jnp.arange(length, dtype=jnp.int32)
    far = rel_bias[N_BUCKETS // 2 - 1].astype(F32)
    w = (rel_bias[_rel_bucket(rel)].astype(F32) - far[None, :]).T
    skew = jnp.tile(w, (1, n_q))[:, :n_q * (length - 1)].reshape(-1, n_q, length - 1)
    return skew[:, :, n_q - 1:n_q - 1 + n_k]


def _chunk_mask(tile, q_pos, k_pos):
    visible = (k_pos[None, :] // CHUNK) <= (q_pos[:, None] // CHUNK)
    return jnp.where(visible[None], tile, -jnp.inf)


SUM_LO = 2.0 ** -60
SUM_HI = 2.0 ** 60


def _prompt_attn_kernel(lamv_ref, g_ref, q_ref, k_ref, v_ref, bd_ref, bp_ref, o_ref,
                        acc_ref, lp_ref, l_ref, m_ref, *, tq, tk, hp, tps, lam_init):
    def tile(t, carry):
        rows = pl.ds(pl.multiple_of(t * tq, tq), tq)
        _prompt_attn_tile(pl.program_id(2) * tps + t, q_ref.at[0, t], o_ref.at[0, rows], lamv_ref, g_ref,
                          k_ref, v_ref, bd_ref, bp_ref, acc_ref, lp_ref, l_ref, m_ref,
                          tq=tq, tk=tk, hp=hp, lam_init=lam_init)
        return carry

    lax.fori_loop(0, tps, tile, 0)


def _prompt_attn_tile(qt, q_ref, o_ref, lamv_ref, g_ref, k_ref, v_ref, bd_ref, bp_ref,
                      acc_ref, lp_ref, l_ref, m_ref, *, tq, tk, hp, lam_init):
    sub = tk // tq
    feats = [slice(h * HEAD_V, (h + 1) * HEAD_V) for h in range(hp)]

    def stack_maps(qT):
        row = lax.broadcasted_iota(jnp.int32, qT.shape, 0)
        zero = jnp.zeros_like(qT)
        return jnp.concatenate([jnp.where(row < HEAD_QK, qT, zero), jnp.where(row < HEAD_QK, zero, qT)], axis=1)

    qz = [stack_maps(q_ref[ft, :]) for ft in feats]

    def kv(j, h):
        start = pl.multiple_of(j * tq, tq)
        return k_ref[0, pl.ds(start, tq), feats[h]], v_ref[0, j, feats[h], :]

    def logits(h, kj, bias):
        s = jnp.dot(kj, qz[h], preferred_element_type=F32)
        if bias is not None:
            s = s + jnp.concatenate([bias, bias], axis=1)
        return s

    def sublane_fold(p):
        return jnp.sum(p.reshape(p.shape[0] // V7X_SUBLANES, V7X_SUBLANES, p.shape[1]), axis=0)

    def numerators(chains, lag, first=False):
        totals = {}
        pending = []

        def values(h, vj, p):
            pv = jnp.dot(vj, p.astype(BF16), preferred_element_type=F32)
            lp = sublane_fold(p)
            totals[h] = (pv, lp) if h not in totals else (totals[h][0] + pv, totals[h][1] + lp)

        for h, j, bias in chains:
            kj, vj = kv(j, h)
            pending.append((h, vj, jnp.exp(logits(h, kj, bias))))
            if len(pending) > lag:
                values(*pending.pop(0))
        while pending:
            values(*pending.pop(0))
        for h, (pv, lp) in totals.items():
            if first:
                acc_ref[h] = pv
                lp_ref[h] = lp
            else:
                acc_ref[h] += pv
                lp_ref[h] += lp

    heads = range(hp)
    n_far = jnp.maximum(qt - 1, 0)
    n_lead = n_far % sub
    n_blocks = n_far // sub

    @pl.when(qt == 0)
    def _():
        numerators([(h, qt, bd_ref[h]) for h in heads], hp, first=True)

    for lead in range(sub):
        @pl.when((qt > 0) & (n_lead == lead))
        def _(lead=lead):
            near = ((0, bd_ref), (1, bp_ref))
            numerators([(h, qt - d, None if b_ref is None else b_ref[h]) for h in heads
                        for d, b_ref in near + tuple((2 + f, None) for f in range(lead))], hp, first=True)

    def far_region(first_block, blocks):
        numerators([(h, first_block * sub + c, None) for c in range(blocks * sub) for h in heads], FAR_LAG)

    def quad(jb, carry):
        far_region(jb * FAR_BLOCKS, FAR_BLOCKS)
        return carry

    lax.fori_loop(0, n_blocks // FAR_BLOCKS, quad, 0)

    for rem in range(1, FAR_BLOCKS):
        @pl.when(n_blocks % FAR_BLOCKS == rem)
        def _(rem=rem):
            far_region(n_blocks - rem, rem)

    lam = _lambda(lamv_ref, lam_init)

    def write_output():
        for h in range(hp):
            o = acc_ref[h] / l_ref[h]
            o = o[:, :tq] - lam * o[:, tq:]
            ms = jnp.mean(o * o, axis=0, keepdims=True)
            o = o * lax.rsqrt(ms + SUBLN_EPS) * g_ref[...] * (1.0 - lam_init)
            o_ref[:, feats[h]] = o.T.astype(o_ref.dtype)

    for h in range(hp):
        l_ref[h] = jnp.sum(lp_ref[h], axis=0, keepdims=True)
    l_all = l_ref[...]
    in_range = (l_all >= SUM_LO) & (l_all <= SUM_HI)
    write_output()

    @pl.when(jnp.min(jnp.where(in_range, 1.0, 0.0)) < 0.5)
    def _():
        for h in range(hp):
            kd, vd = kv(qt, h)
            s = logits(h, kd, bd_ref[h])
            m0 = jnp.max(s, axis=0, keepdims=True)
            p = jnp.exp(s - m0)
            m_ref[h] = m0
            l_ref[h] = jnp.sum(p, axis=0, keepdims=True)
            acc_ref[h] = jnp.dot(vd, p.astype(BF16), preferred_element_type=F32)

            def step(j, carry, h=h):
                kj, vj = kv(j, h)
                s = logits(h, kj, jnp.where(j == qt - 1, bp_ref[h], jnp.zeros_like(bp_ref[h])))
                m_prev = m_ref[h]
                m_new = jnp.maximum(m_prev, jnp.max(s, axis=0, keepdims=True))
                alpha = jnp.exp(m_prev - m_new)
                p = jnp.exp(s - m_new)
                l_ref[h] = alpha * l_ref[h] + jnp.sum(p, axis=0, keepdims=True)
                acc_ref[h] = alpha * acc_ref[h] + jnp.dot(vj, p.astype(BF16), preferred_element_type=F32)
                m_ref[h] = m_new
                return carry

            lax.fori_loop(0, qt, step, 0)
        write_output()


def _prompt_attention(qT, kb, vT, rel_bias, lamv, subln_g, lam_init):
    b, t, _ = kb.shape
    tq = qT.shape[-1]
    tk = min(K_BLOCK, t)
    sub = tk // tq
    assert t % tq == 0 and tk % tq == 0 and tq % CHUNK == 0 and tq >= MAX_DISTANCE
    pos = jnp.arange(tq, dtype=jnp.int32)
    band = _bias_band(rel_bias, tq, 2 * tq, -tq)
    prev = jnp.swapaxes(band[:, :, :tq], 1, 2)
    diag = jnp.swapaxes(_chunk_mask(band[:, :, tq:], pos, pos), 1, 2)
    gain = jnp.broadcast_to(subln_g.astype(F32)[:, None], (HEAD_V, tq))
    hp = HEADS_PER_STEP
    tps = min(TILES_PER_STEP, t // tq)
    assert N_HEADS % hp == 0 and (t // tq) % tps == 0
    qspec = pl.BlockSpec((1, tps, hp * HEAD_V, tq), lambda bi, g, i: (bi, i, g, 0))
    kspec = pl.BlockSpec((1, t, hp * HEAD_V), lambda bi, g, i: (bi, 0, g))
    vspec = pl.BlockSpec((1, t // tq, hp * HEAD_V, tq), lambda bi, g, i: (bi, 0, g, 0))
    bspec = pl.BlockSpec((hp, tq, tq), lambda bi, g, i: (g, 0, 0))
    return pl.pallas_call(
        functools.partial(_prompt_attn_kernel, tq=tq, tk=tk, hp=hp, tps=tps, lam_init=lam_init),
        out_shape=jax.ShapeDtypeStruct(kb.shape, BF16),
        grid=(b, N_HEADS // hp, t // (tq * tps)),
        in_specs=[_resident(lamv.shape), _resident((HEAD_V, tq)), qspec, kspec, vspec, bspec, bspec],
        out_specs=pl.BlockSpec((1, tq * tps, hp * HEAD_V), lambda bi, g, i: (bi, i, g)),
        scratch_shapes=[pltpu.VMEM((hp, HEAD_V, 2 * tq), F32), pltpu.VMEM((hp, V7X_SUBLANES, 2 * tq), F32),
                        pltpu.VMEM((hp, 1, 2 * tq), F32), pltpu.VMEM((hp, 1, 2 * tq), F32)],
        compiler_params=pltpu.CompilerParams(
            dimension_semantics=("arbitrary", "arbitrary", "arbitrary"),
            vmem_limit_bytes=V7X_VMEM_LIMIT_BYTES),
        name="prompt_attention",
    )(lamv, gain, qT, kb, vT, diag, prev)


def _sample_attn_kernel(lamv_ref, g_ref, q_ref, ck_ref, cv_ref, kn_ref, vn_ref, bias_ref, o_ref,
                        *, ts, past, near, lam_init):
    far = past - near
    lam = _lambda(lamv_ref, lam_init)
    lanes = [slice(h * HEAD_V, (h + 1) * HEAD_V) for h in range(N_HEADS)]

    def logits(h):
        qz = _stack_maps(q_ref[0, :, lanes[h]])
        ck = ck_ref[0, pl.ds(h, past, stride=N_HEADS), :].astype(BF16)
        bias = bias_ref[h]
        return (_scores(qz, ck[:far]),
                _add_to_both_maps(_scores(qz, ck[far:]), bias[:, :near]),
                _add_to_both_maps(_scores(qz, kn_ref[0, :, lanes[h]]), bias[:, near:]))

    def attend(h, s_far, s_near, s_new):
        cv = cv_ref[0, pl.ds(h, past, stride=N_HEADS), :].astype(BF16)
        m = jnp.maximum(jnp.max(s_far, axis=-1, keepdims=True),
                        jnp.maximum(jnp.max(s_near, axis=-1, keepdims=True),
                                    jnp.max(s_new, axis=-1, keepdims=True)))
        p_far = jnp.exp(s_far - m)
        p_near = jnp.exp(s_near - m)
        p_new = jnp.exp(s_new - m)
        l = (jnp.sum(p_far, axis=-1, keepdims=True) + jnp.sum(p_near, axis=-1, keepdims=True)
             + jnp.sum(p_new, axis=-1, keepdims=True))
        acc = (jnp.dot(p_far.astype(BF16), cv[:far], preferred_element_type=F32)
               + jnp.dot(p_near.astype(BF16), cv[far:], preferred_element_type=F32)
               + jnp.dot(p_new.astype(BF16), vn_ref[0, :, lanes[h]], preferred_element_type=F32))
        o_ref[0, :, lanes[h]] = _diff_combine(acc, l, lam, g_ref[...], lam_init, ts).astype(o_ref.dtype)

    s = logits(0)
    for h in range(1, N_HEADS):
        s_next = logits(h)
        attend(h - 1, *s)
        s = s_next
    attend(N_HEADS - 1, *s)


def _sample_attention(q, kb, vb, cache_k, cache_v, rel_bias, lamv, subln_g, lam_init):
    s, ts, _ = q.shape
    past = cache_k.shape[1] // N_HEADS
    near = MAX_DISTANCE
    assert past >= near and past % V7X_SUBLANES == 0
    q_pos = past + jnp.arange(ts, dtype=jnp.int32)
    k_pos = (past - near) + jnp.arange(near + ts, dtype=jnp.int32)
    bias = _chunk_mask(_bias_band(rel_bias, ts, near + ts, -near), q_pos, k_pos)
    new = pl.BlockSpec((1, ts, ATTN_WIDTH), lambda si: (si, 0, 0))
    cache = pl.BlockSpec((1, past * N_HEADS, HEAD_V), lambda si: (si, 0, 0))
    return pl.pallas_call(
        functools.partial(_sample_attn_kernel, ts=ts, past=past, near=near, lam_init=lam_init),
        out_shape=jax.ShapeDtypeStruct(q.shape, BF16),
        grid=(s,),
        in_specs=[_resident(lamv.shape), _resident((1, HEAD_V)), new, cache, cache, new, new,
                  _resident(bias.shape)],
        out_specs=new,
        compiler_params=pltpu.CompilerParams(
            dimension_semantics=("arbitrary",), vmem_limit_bytes=V7X_VMEM_LIMIT_BYTES),
        name="sample_attention",
    )(lamv, subln_g.reshape(1, HEAD_V), q, cache_k, cache_v, kb, vb, bias)


def _finish_kernel(x_ref, a_ref, c_ref, g0_ref, b0_ref, wo_ref, g1_ref, b1_ref, w1_ref, w2_ref,
                   g2_ref, b2_ref, o_ref, *, alpha, chunks):
    rows = x_ref.shape[0] // chunks
    rs = [pl.ds(c * rows, rows) for c in range(chunks)]
    xn, y, x1, h, f = [], [], [], [], []
    for c in range(chunks):
        xn.append(_layer_norm(x_ref[rs[c], :], g0_ref[...], b0_ref[...]))
        mix = jnp.concatenate([a_ref[rs[c], :], c_ref[rs[c], :]], axis=-1)
        y.append(jnp.dot(mix, wo_ref[...], preferred_element_type=F32))
    for c in range(chunks):
        x1.append(_layer_norm(alpha * xn[c] + y[c], g1_ref[...], b1_ref[...]))
        h.append(jnp.dot(x1[c].astype(BF16), w1_ref[...], preferred_element_type=F32))
    for c in range(chunks):
        hc = jnp.square(jnp.maximum(h[c], 0.0)).astype(BF16)
        f.append(jnp.dot(hc, w2_ref[...], preferred_element_type=F32))
    for c in range(chunks):
        o_ref[rs[c], :] = _layer_norm(alpha * x1[c] + f[c], g2_ref[...], b2_ref[...])


def _finish(x2d, attn, conv, ln0_g, ln0_b, w_out_bf, ln1_g, ln1_b, w1_bf, w2_bf, ln2_g, ln2_b, alpha):
    n, d = x2d.shape
    a = attn.shape[1]
    c = conv.shape[1]
    f = w1_bf.shape[1]
    tm = min(ROW_TILE, n)
    assert n % tm == 0
    row = lambda w: pl.BlockSpec((tm, w), lambda i: (i, 0))
    vec = lambda p: p.reshape(1, d)
    return pl.pallas_call(
        functools.partial(_finish_kernel, alpha=alpha, chunks=FINISH_CHUNKS),
        out_shape=jax.ShapeDtypeStruct((n, d), F32),
        grid=(n // tm,),
        in_specs=[row(d), row(a), row(c), _resident((1, d)), _resident((1, d)), _resident((a + c, d)),
                  _resident((1, d)), _resident((1, d)), _resident((d, f)), _resident((f, d)),
                  _resident((1, d)), _resident((1, d))],
        out_specs=row(d),
        compiler_params=pltpu.CompilerParams(
            dimension_semantics=("arbitrary",), vmem_limit_bytes=V7X_VMEM_LIMIT_BYTES),
        name="finish",
    )(x2d, attn, conv, vec(ln0_g), vec(ln0_b), w_out_bf, vec(ln1_g), vec(ln1_b), w1_bf, w2_bf,
      vec(ln2_g), vec(ln2_b))


def kernel(x_prompt, x_sample, cache_k, cache_v, cache_conv, ln0_g, ln0_b, rel_bias, w_in, conv_w,
           lambda_q1, lambda_k1, lambda_q2, lambda_k2, subln_g, w_out, ln1_g, ln1_b,
           w_ff1, w_ff2, ln2_g, ln2_b):
    depth = w_in.shape[0]
    assert depth == 1, "single-layer step"
    layer = 0
    b, t, d = x_prompt.shape
    s, ts, _ = x_sample.shape
    past = cache_k.shape[2]
    c = conv_w.shape[-1]
    alpha = (2.0 * depth) ** 0.25
    lam_init = _lambda_init(layer)

    w_in_f = w_in[layer].astype(F32)
    lamv = jnp.stack([lambda_q1[layer], lambda_k1[layer], lambda_q2[layer], lambda_k2[layer]]).astype(F32)

    xp2 = x_prompt.reshape(b * t, d)
    hist0 = jnp.zeros((b, CONV_K - 1, c), F32)
    tq = min(Q_TILE, t)
    qp, kp, vp, kpb, vpb, convp, tailp, w_out_bf, w1_bf, w2_bf = _in_proj(
        xp2, t, ln0_g, ln0_b, w_in_f, conv_w[layer], hist0, qv_tile=tq,
        cast=(w_out[layer].astype(F32), w_ff1[layer].astype(F32), w_ff2[layer].astype(F32)))
    ffn = (ln0_g, ln0_b, w_out_bf, ln1_g[layer], ln1_b[layer], w1_bf, w2_bf, ln2_g[layer], ln2_b[layer], alpha)
    tiles = (b, t // tq, ATTN_WIDTH, tq)
    attnp = _prompt_attention(qp.reshape(tiles), kpb.reshape(b, t, ATTN_WIDTH), vpb.reshape(tiles), rel_bias,
                              lamv, subln_g[layer], lam_init)
    yp = _finish(xp2, attnp.reshape(b * t, ATTN_WIDTH), convp, *ffn).reshape(b, t, d)

    xs2 = x_sample.reshape(s * ts, d)
    qs, ks, vs, ksb, vsb, convs, tails = _in_proj(xs2, ts, ln0_g, ln0_b, w_in_f, conv_w[layer],
                                                  cache_conv[layer].astype(F32))
    shs = (s, ts, ATTN_WIDTH)
    attns = _sample_attention(qs.reshape(shs), ksb.reshape(shs), vsb.reshape(shs),
                              cache_k[layer].astype(F32).reshape(s, past * N_HEADS, HEAD_V),
                              cache_v[layer].astype(F32).reshape(s, past * N_HEADS, HEAD_V),
                              rel_bias, lamv, subln_g[layer], lam_init)
    ys = _finish(xs2, attns.reshape(s * ts, ATTN_WIDTH), convs, *ffn).reshape(s, ts, d)

    keep = CONV_K - 1
    return (yp, ys,
            kp.reshape(1, b, t, N_HEADS, HEAD_V), vp.reshape(1, b, t, N_HEADS, HEAD_V),
            tailp[:, V7X_SUBLANES - keep:, :].reshape(1, b, keep, c),
            ks.reshape(1, s, ts, N_HEADS, HEAD_V), vs.reshape(1, s, ts, N_HEADS, HEAD_V),
            tails[:, V7X_SUBLANES - keep:, :].reshape(1, s, keep, c))
```

```python
import functools
import math

import jax
import jax.numpy as jnp
from jax import lax
from jax.experimental import pallas as pl
from jax.experimental.pallas import tpu as pltpu

CHUNK = 64
N_HEADS = 4
HEAD_V = 128
HEAD_QK = HEAD_V // 2
ATTN_WIDTH = N_HEADS * HEAD_V
CONV_K = 3
N_BUCKETS = 32
MAX_DISTANCE = 128
LN_EPS = 1e-5
SUBLN_EPS = 1e-5
LOG2_E = math.log2(math.e)
Q_SCALE = LOG2_E * HEAD_QK ** -0.5

V7X_SUBLANES = 8
V7X_BF16_SUBLANES = 16
V7X_VMEM_LIMIT_BYTES = 56 * 1024 * 1024

ROW_TILE = 512
FINISH_CHUNKS = 2
Q_TILE = 256
K_BLOCK = 1024
HEADS_PER_STEP = 4
TILES_PER_STEP = 4
FAR_BLOCKS = 4
FAR_LAG = 2

F32 = jnp.float32
BF16 = jnp.bfloat16


def _lambda_init(layer):
    return 0.8 - 0.6 * math.exp(-0.3 * layer)


def _layer_norm(x, g, b):
    mu = jnp.mean(x, axis=-1, keepdims=True)
    xc = x - mu
    var = jnp.mean(xc * xc, axis=-1, keepdims=True)
    return xc * lax.rsqrt(var + LN_EPS) * g + b


def _resident(shape):
    nd = len(shape)
    return pl.BlockSpec(shape, lambda *_: (0,) * nd, pipeline_mode=pl.Buffered(1))


def _in_proj_kernel(x_ref, g_ref, b_ref, w_ref, cw_ref, hist_ref, *rest, seg_len, tm, qv_tile, n_cast):
    cast_in, rest = rest[:n_cast], rest[n_cast:]
    q_ref, k_ref, v_ref, kb_ref, vb_ref, conv_ref, utail_ref = rest[:7]
    cast_out, (carry_ref,) = rest[7:7 + n_cast], rest[7 + n_cast:]
    a = ATTN_WIDTH
    c = (w_ref.shape[1] - 3 * a) // 3
    for src, dst in zip(cast_in, cast_out):
        dst[...] = src[...].astype(BF16)
    xn = _layer_norm(x_ref[...], g_ref[...], b_ref[...])
    z = jnp.dot(xn, w_ref[...], preferred_element_type=F32)
    gb = z[:, 3 * a:3 * a + c]
    u = z[:, 3 * a + c:3 * a + 2 * c] * z[:, 3 * a + 2 * c:]
    q = z[:, :a] * Q_SCALE
    k = z[:, a:2 * a]
    v = z[:, 2 * a:3 * a]
    for h in range(N_HEADS):
        k_ref[pl.ds(h, tm, stride=N_HEADS), :] = k[:, h * HEAD_V:(h + 1) * HEAD_V]
        v_ref[pl.ds(h, tm, stride=N_HEADS), :] = v[:, h * HEAD_V:(h + 1) * HEAD_V]
    kb_ref[...] = k.astype(BF16)
    if qv_tile is None:
        q_ref[...] = q.astype(BF16)
        vb_ref[...] = v.astype(BF16)
    else:
        for s in range(tm // qv_tile):
            rows_s = slice(s * qv_tile, (s + 1) * qv_tile)
            q_ref[s] = q[rows_s, :].T.astype(BF16)
            vb_ref[s] = v[rows_s, :].T.astype(BF16)
    rows = lax.broadcasted_iota(jnp.int32, (tm, 1), 0)
    if seg_len >= tm:
        tiles_per_seg = seg_len // tm
        first = (pl.program_id(0) % tiles_per_seg) == 0
        hist = hist_ref[0]
        prev = carry_ref[...]
        p1 = jnp.where(first, hist[1:2, :], prev[V7X_SUBLANES - 1:, :])
        p0 = jnp.where(first, hist[0:1, :], prev[V7X_SUBLANES - 2:V7X_SUBLANES - 1, :])
        pos = rows
        tail = u[tm - V7X_SUBLANES:, :]
        carry_ref[...] = tail
        utail_ref[0] = tail
    else:
        nseg = tm // seg_len
        hist = hist_ref[...]
        p1 = jnp.broadcast_to(hist[:, 1:2, :], (nseg, seg_len, c)).reshape(tm, c)
        p0 = jnp.broadcast_to(hist[:, 0:1, :], (nseg, seg_len, c)).reshape(tm, c)
        pos = rows & (seg_len - 1)
        utail_ref[...] = u.reshape(nseg, seg_len, c)[:, seg_len - V7X_SUBLANES:, :]
    u1 = jnp.where(pos == 0, p1, pltpu.roll(u, 1, 0))
    u2 = jnp.where(pos == 0, p0, jnp.where(pos == 1, p1, pltpu.roll(u, 2, 0)))
    cw = cw_ref[...]
    conv = gb * (u2 * cw[0:1, :] + u1 * cw[1:2, :] + u * cw[2:3, :])
    conv_ref[...] = conv.astype(BF16)


def _in_proj(x2d, seg_len, ln_g, ln_b, w_in, conv_w, hist, qv_tile=None, cast=()):
    n, d = x2d.shape
    e = w_in.shape[1]
    a = ATTN_WIDTH
    c = (e - 3 * a) // 3
    tm = min(ROW_TILE, n)
    assert n % tm == 0 and (seg_len % tm == 0 or tm % seg_len == 0)
    assert seg_len & (seg_len - 1) == 0 and seg_len % V7X_SUBLANES == 0
    nseq = n // seg_len
    if seg_len >= tm:
        tiles_per_seg = seg_len // tm
        hist_spec = pl.BlockSpec((1, CONV_K - 1, c), lambda i: (i // tiles_per_seg, 0, 0))
        tail_spec = pl.BlockSpec((1, V7X_SUBLANES, c), lambda i: (i // tiles_per_seg, 0, 0))
    else:
        nseg = tm // seg_len
        hist_spec = pl.BlockSpec((nseg, CONV_K - 1, c), lambda i: (i, 0, 0))
        tail_spec = pl.BlockSpec((nseg, V7X_SUBLANES, c), lambda i: (i, 0, 0))
    row = lambda w: pl.BlockSpec((tm, w), lambda i: (i, 0))
    headrow = pl.BlockSpec((tm * N_HEADS, HEAD_V), lambda i: (i, 0))
    if qv_tile is None:
        qv_shape = jax.ShapeDtypeStruct((n, a), BF16)
        qv_spec = row(a)
    else:
        assert tm % qv_tile == 0
        qv_shape = jax.ShapeDtypeStruct((n // qv_tile, a, qv_tile), BF16)
        qv_spec = pl.BlockSpec((tm // qv_tile, a, qv_tile), lambda i: (i, 0, 0))
    out_shape = (
        qv_shape,
        jax.ShapeDtypeStruct((n * N_HEADS, HEAD_V), F32),
        jax.ShapeDtypeStruct((n * N_HEADS, HEAD_V), F32),
        jax.ShapeDtypeStruct((n, a), BF16),
        qv_shape,
        jax.ShapeDtypeStruct((n, c), BF16),
        jax.ShapeDtypeStruct((nseq, V7X_SUBLANES, c), F32),
    ) + tuple(jax.ShapeDtypeStruct(m.shape, BF16) for m in cast)
    steps = n // tm
    slab_specs = []
    for m in cast:
        assert m.shape[0] % (steps * V7X_BF16_SUBLANES) == 0
        slab_specs.append(pl.BlockSpec((m.shape[0] // steps, m.shape[1]), lambda i: (i, 0)))
    return pl.pallas_call(
        functools.partial(_in_proj_kernel, seg_len=seg_len, tm=tm, qv_tile=qv_tile, n_cast=len(cast)),
        out_shape=out_shape,
        grid=(steps,),
        in_specs=[row(d), _resident((1, d)), _resident((1, d)), _resident((d, e)),
                  _resident((CONV_K, c)), hist_spec, *slab_specs],
        out_specs=(qv_spec, headrow, headrow, row(a), qv_spec, row(c), tail_spec, *slab_specs),
        scratch_shapes=[pltpu.VMEM((V7X_SUBLANES, c), F32)],
        compiler_params=pltpu.CompilerParams(
            dimension_semantics=("arbitrary",), vmem_limit_bytes=V7X_VMEM_LIMIT_BYTES),
        name="in_proj",
    )(x2d, ln_g.reshape(1, d), ln_b.reshape(1, d), w_in, conv_w, hist, *cast)


def _stack_maps(q):
    lane = lax.broadcasted_iota(jnp.int32, q.shape, 1)
    zero = jnp.zeros_like(q)
    return jnp.concatenate([jnp.where(lane < HEAD_QK, q, zero), jnp.where(lane < HEAD_QK, zero, q)], axis=0)


def _scores(qz, kblk):
    return lax.dot_general(qz, kblk, (((1,), (1,)), ((), ())), preferred_element_type=F32)


def _add_to_both_maps(s, bias):
    t, n = bias.shape
    return (s.reshape(2, t, n) + bias[None]).reshape(2 * t, n)


def _lambda(lamv_ref, lam_init):
    lv = lamv_ref[...]
    s1 = jnp.sum(lv[0:1, :] * lv[1:2, :], axis=-1, keepdims=True)
    s2 = jnp.sum(lv[2:3, :] * lv[3:4, :], axis=-1, keepdims=True)
    return jnp.exp(s1) - jnp.exp(s2) + lam_init


def _diff_combine(acc, l, lam, g, lam_init, t):
    o = acc[:t] / l[:t] - lam * (acc[t:] / l[t:])
    ms = jnp.mean(o * o, axis=-1, keepdims=True)
    return o * lax.rsqrt(ms + SUBLN_EPS) * g * (1.0 - lam_init)


def _rel_bucket(rel):
    half = N_BUCKETS // 2
    max_exact = half // 2
    ret = jnp.where(rel > 0, half, 0)
    n = jnp.abs(rel)
    nf = jnp.maximum(n, 1).astype(F32)
    large = max_exact + (jnp.log(nf / max_exact) / math.log(MAX_DISTANCE / max_exact)
                         * (half - max_exact)).astype(jnp.int32)
    large = jnp.minimum(large, half - 1)
    return ret + jnp.where(n < max_exact, n, large)


def _bias_band(rel_bias, n_q, n_k, rel00):
    length = n_q + n_k
    rel = rel00 - (n_q - 1) + jnp.arange(length, dtype=jnp.int32)
    far = rel_bias[N_BUCKETS // 2 - 1].astype(F32)
    w = ((rel_bias[_rel_bucket(rel)].astype(F32) - far[None, :]) * LOG2_E).T
    skew = jnp.tile(w, (1, n_q))[:, :n_q * (length - 1)].reshape(-1, n_q, length - 1)
    return skew[:, :, n_q - 1:n_q - 1 + n_k]


def _chunk_mask(tile, q_pos, k_pos):
    visible = (k_pos[None, :] // CHUNK) <= (q_pos[:, None] // CHUNK)
    return jnp.where(visible[None], tile, -jnp.inf)


SUM_LO = 2.0 ** -60
SUM_HI = 2.0 ** 60


def _prompt_attn_kernel(lamv_ref, g_ref, q_ref, k_ref, v_ref, bd_ref, bp_ref, o_ref,
                        acc_ref, lp_ref, l_ref, m_ref, *, tq, tk, hp, tps, lam_init):
    def tile(t, carry):
        rows = pl.ds(pl.multiple_of(t * tq, tq), tq)
        _prompt_attn_tile(pl.program_id(2) * tps + t, q_ref.at[0, t], o_ref.at[0, rows], lamv_ref, g_ref,
                          k_ref, v_ref, bd_ref, bp_ref, acc_ref, lp_ref, l_ref, m_ref,
                          tq=tq, tk=tk, hp=hp, lam_init=lam_init)
        return carry

    lax.fori_loop(0, tps, tile, 0)


def _prompt_attn_tile(qt, q_ref, o_ref, lamv_ref, g_ref, k_ref, v_ref, bd_ref, bp_ref,
                      acc_ref, lp_ref, l_ref, m_ref, *, tq, tk, hp, lam_init):
    sub = tk // tq
    feats = [slice(h * HEAD_V, (h + 1) * HEAD_V) for h in range(hp)]

    def stack_maps(qT):
        row = lax.broadcasted_iota(jnp.int32, qT.shape, 0)
        zero = jnp.zeros_like(qT)
        return jnp.concatenate([jnp.where(row < HEAD_QK, qT, zero), jnp.where(row < HEAD_QK, zero, qT)], axis=1)

    qz = [stack_maps(q_ref[ft, :]) for ft in feats]

    def kv(j, h):
        start = pl.multiple_of(j * tq, tq)
        return k_ref[0, pl.ds(start, tq), feats[h]], v_ref[0, j, feats[h], :]

    def logits(h, kj, bias):
        s = jnp.dot(kj, qz[h], preferred_element_type=F32)
        if bias is not None:
            s = s + jnp.concatenate([bias, bias], axis=1)
        return s

    def sublane_fold(p):
        return jnp.sum(p.reshape(p.shape[0] // V7X_SUBLANES, V7X_SUBLANES, p.shape[1]), axis=0)

    def numerators(chains, lag, first=False):
        totals = {}
        pending = []

        def values(h, vj, p):
            pv = jnp.dot(vj, p.astype(BF16), preferred_element_type=F32)
            lp = sublane_fold(p)
            totals[h] = (pv, lp) if h not in totals else (totals[h][0] + pv, totals[h][1] + lp)

        for h, j, bias in chains:
            kj, vj = kv(j, h)
            pending.append((h, vj, jnp.exp2(logits(h, kj, bias))))
            if len(pending) > lag:
                values(*pending.pop(0))
        while pending:
            values(*pending.pop(0))
        for h, (pv, lp) in totals.items():
            if first:
                acc_ref[h] = pv
                lp_ref[h] = lp
            else:
                acc_ref[h] += pv
                lp_ref[h] += lp

    heads = range(hp)
    n_far = jnp.maximum(qt - 1, 0)
    n_lead = n_far % sub
    n_blocks = n_far // sub

    @pl.when(qt == 0)
    def _():
        numerators([(h, qt, bd_ref[h]) for h in heads], hp, first=True)

    for lead in range(sub):
        @pl.when((qt > 0) & (n_lead == lead))
        def _(lead=lead):
            near = ((0, bd_ref), (1, bp_ref))
            numerators([(h, qt - d, None if b_ref is None else b_ref[h]) for h in heads
                        for d, b_ref in near + tuple((2 + f, None) for f in range(lead))], hp, first=True)

    def far_region(first_block, blocks):
        numerators([(h, first_block * sub + c, None) for c in range(blocks * sub) for h in heads], FAR_LAG)

    def quad(jb, carry):
        far_region(jb * FAR_BLOCKS, FAR_BLOCKS)
        return carry

    lax.fori_loop(0, n_blocks // FAR_BLOCKS, quad, 0)

    for rem in range(1, FAR_BLOCKS):
        @pl.when(n_blocks % FAR_BLOCKS == rem)
        def _(rem=rem):
            far_region(n_blocks - rem, rem)

    lam = _lambda(lamv_ref, lam_init)

    def write_output():
        for h in range(hp):
            o = acc_ref[h] / l_ref[h]
            o = o[:, :tq] - lam * o[:, tq:]
            ms = jnp.mean(o * o, axis=0, keepdims=True)
            o = o * lax.rsqrt(ms + SUBLN_EPS) * g_ref[...] * (1.0 - lam_init)
            o_ref[:, feats[h]] = o.T.astype(o_ref.dtype)

    for h in range(hp):
        l_ref[h] = jnp.sum(lp_ref[h], axis=0, keepdims=True)
    l_all = l_ref[...]
    in_range = (l_all >= SUM_LO) & (l_all <= SUM_HI)
    write_output()

    @pl.when(jnp.min(jnp.where(in_range, 1.0, 0.0)) < 0.5)
    def _():
        for h in range(hp):
            kd, vd = kv(qt, h)
            s = logits(h, kd, bd_ref[h])
            m0 = jnp.max(s, axis=0, keepdims=True)
            p = jnp.exp2(s - m0)
            m_ref[h] = m0
            l_ref[h] = jnp.sum(p, axis=0, keepdims=True)
            acc_ref[h] = jnp.dot(vd, p.astype(BF16), preferred_element_type=F32)

            def step(j, carry, h=h):
                kj, vj = kv(j, h)
                s = logits(h, kj, jnp.where(j == qt - 1, bp_ref[h], jnp.zeros_like(bp_ref[h])))
                m_prev = m_ref[h]
                m_new = jnp.maximum(m_prev, jnp.max(s, axis=0, keepdims=True))
                alpha = jnp.exp2(m_prev - m_new)
                p = jnp.exp2(s - m_new)
                l_ref[h] = alpha * l_ref[h] + jnp.sum(p, axis=0, keepdims=True)
                acc_ref[h] = alpha * acc_ref[h] + jnp.dot(vj, p.astype(BF16), preferred_element_type=F32)
                m_ref[h] = m_new
                return carry

            lax.fori_loop(0, qt, step, 0)
        write_output()


def _prompt_attention(qT, kb, vT, rel_bias, lamv, subln_g, lam_init):
    b, t, _ = kb.shape
    tq = qT.shape[-1]
    tk = min(K_BLOCK, t)
    sub = tk // tq
    assert t % tq == 0 and tk % tq == 0 and tq % CHUNK == 0 and tq >= MAX_DISTANCE
    pos = jnp.arange(tq, dtype=jnp.int32)
    band = _bias_band(rel_bias, tq, 2 * tq, -tq)
    prev = jnp.swapaxes(band[:, :, :tq], 1, 2)
    diag = jnp.swapaxes(_chunk_mask(band[:, :, tq:], pos, pos), 1, 2)
    gain = jnp.broadcast_to(subln_g.astype(F32)[:, None], (HEAD_V, tq))
    hp = HEADS_PER_STEP
    tps = min(TILES_PER_STEP, t // tq)
    assert N_HEADS % hp == 0 and (t // tq) % tps == 0
    qspec = pl.BlockSpec((1, tps, hp * HEAD_V, tq), lambda bi, g, i: (bi, i, g, 0))
    kspec = pl.BlockSpec((1, t, hp * HEAD_V), lambda bi, g, i: (bi, 0, g))
    vspec = pl.BlockSpec((1, t // tq, hp * HEAD_V, tq), lambda bi, g, i: (bi, 0, g, 0))
    bspec = pl.BlockSpec((hp, tq, tq), lambda bi, g, i: (g, 0, 0))
    return pl.pallas_call(
        functools.partial(_prompt_attn_kernel, tq=tq, tk=tk, hp=hp, tps=tps, lam_init=lam_init),
        out_shape=jax.ShapeDtypeStruct(kb.shape, BF16),
        grid=(b, N_HEADS // hp, t // (tq * tps)),
        in_specs=[_resident(lamv.shape), _resident((HEAD_V, tq)), qspec, kspec, vspec, bspec, bspec],
        out_specs=pl.BlockSpec((1, tq * tps, hp * HEAD_V), lambda bi, g, i: (bi, i, g)),
        scratch_shapes=[pltpu.VMEM((hp, HEAD_V, 2 * tq), F32), pltpu.VMEM((hp, V7X_SUBLANES, 2 * tq), F32),
                        pltpu.VMEM((hp, 1, 2 * tq), F32), pltpu.VMEM((hp, 1, 2 * tq), F32)],
        compiler_params=pltpu.CompilerParams(
            dimension_semantics=("arbitrary", "arbitrary", "arbitrary"),
            vmem_limit_bytes=V7X_VMEM_LIMIT_BYTES),
        name="prompt_attention",
    )(lamv, gain, qT, kb, vT, diag, prev)


def _sample_attn_kernel(lamv_ref, g_ref, q_ref, ck_ref, cv_ref, kn_ref, vn_ref, bias_ref, o_ref,
                        *, ts, past, near, lam_init):
    far = past - near
    lam = _lambda(lamv_ref, lam_init)
    lanes = [slice(h * HEAD_V, (h + 1) * HEAD_V) for h in range(N_HEADS)]

    def logits(h):
        qz = _stack_maps(q_ref[0, :, lanes[h]])
        ck = ck_ref[0, pl.ds(h, past, stride=N_HEADS), :].astype(BF16)
        bias = bias_ref[h]
        return (_scores(qz, ck[:far]),
                _add_to_both_maps(_scores(qz, ck[far:]), bias[:, :near]),
                _add_to_both_maps(_scores(qz, kn_ref[0, :, lanes[h]]), bias[:, near:]))

    def attend(h, s_far, s_near, s_new):
        cv = cv_ref[0, pl.ds(h, past, stride=N_HEADS), :].astype(BF16)
        m = jnp.maximum(jnp.max(s_far, axis=-1, keepdims=True),
                        jnp.maximum(jnp.max(s_near, axis=-1, keepdims=True),
                                    jnp.max(s_new, axis=-1, keepdims=True)))
        p_far = jnp.exp2(s_far - m)
        p_near = jnp.exp2(s_near - m)
        p_new = jnp.exp2(s_new - m)
        l = (jnp.sum(p_far, axis=-1, keepdims=True) + jnp.sum(p_near, axis=-1, keepdims=True)
             + jnp.sum(p_new, axis=-1, keepdims=True))
        acc = (jnp.dot(p_far.astype(BF16), cv[:far], preferred_element_type=F32)
               + jnp.dot(p_near.astype(BF16), cv[far:], preferred_element_type=F32)
               + jnp.dot(p_new.astype(BF16), vn_ref[0, :, lanes[h]], preferred_element_type=F32))
        o_ref[0, :, lanes[h]] = _diff_combine(acc, l, lam, g_ref[...], lam_init, ts).astype(o_ref.dtype)

    s = logits(0)
    for h in range(1, N_HEADS):
        s_next = logits(h)
        attend(h - 1, *s)
        s = s_next
    attend(N_HEADS - 1, *s)


def _sample_attention(q, kb, vb, cache_k, cache_v, rel_bias, lamv, subln_g, lam_init):
    s, ts, _ = q.shape
    past = cache_k.shape[1] // N_HEADS
    near = MAX_DISTANCE
    assert past >= near and past % V7X_SUBLANES == 0
    q_pos = past + jnp.arange(ts, dtype=jnp.int32)
    k_pos = (past - near) + jnp.arange(near + ts, dtype=jnp.int32)
    bias = _chunk_mask(_bias_band(rel_bias, ts, near + ts, -near), q_pos, k_pos)
    new = pl.BlockSpec((1, ts, ATTN_WIDTH), lambda si: (si, 0, 0))
    cache = pl.BlockSpec((1, past * N_HEADS, HEAD_V), lambda si: (si, 0, 0))
    return pl.pallas_call(
        functools.partial(_sample_attn_kernel, ts=ts, past=past, near=near, lam_init=lam_init),
        out_shape=jax.ShapeDtypeStruct(q.shape, BF16),
        grid=(s,),
        in_specs=[_resident(lamv.shape), _resident((1, HEAD_V)), new, cache, cache, new, new,
                  _resident(bias.shape)],
        out_specs=new,
        compiler_params=pltpu.CompilerParams(
            dimension_semantics=("arbitrary",), vmem_limit_bytes=V7X_VMEM_LIMIT_BYTES),
        name="sample_attention",
    )(lamv, subln_g.reshape(1, HEAD_V), q, cache_k, cache_v, kb, vb, bias)


def _finish_kernel(x_ref, a_ref, c_ref, g0_ref, b0_ref, wo_ref, g1_ref, b1_ref, w1_ref, w2_ref,
                   g2_ref, b2_ref, o_ref, *, alpha, chunks):
    rows = x_ref.shape[0] // chunks
    rs = [pl.ds(c * rows, rows) for c in range(chunks)]
    xn, y, x1, h, f = [], [], [], [], []
    for c in range(chunks):
        xn.append(_layer_norm(x_ref[rs[c], :], g0_ref[...], b0_ref[...]))
        mix = jnp.concatenate([a_ref[rs[c], :], c_ref[rs[c], :]], axis=-1)
        y.append(jnp.dot(mix, wo_ref[...], preferred_element_type=F32))
    for c in range(chunks):
        x1.append(_layer_norm(alpha * xn[c] + y[c], g1_ref[...], b1_ref[...]))
        h.append(jnp.dot(x1[c].astype(BF16), w1_ref[...], preferred_element_type=F32))
    for c in range(chunks):
        hc = jnp.square(jnp.maximum(h[c], 0.0)).astype(BF16)
        f.append(jnp.dot(hc, w2_ref[...], preferred_element_type=F32))
    for c in range(chunks):
        o_ref[rs[c], :] = _layer_norm(alpha * x1[c] + f[c], g2_ref[...], b2_ref[...])


def _finish(x2d, attn, conv, ln0_g, ln0_b, w_out_bf, ln1_g, ln1_b, w1_bf, w2_bf, ln2_g, ln2_b, alpha):
    n, d = x2d.shape
    a = attn.shape[1]
    c = conv.shape[1]
    f = w1_bf.shape[1]
    tm = min(ROW_TILE, n)
    assert n % tm == 0
    row = lambda w: pl.BlockSpec((tm, w), lambda i: (i, 0))
    vec = lambda p: p.reshape(1, d)
    return pl.pallas_call(
        functools.partial(_finish_kernel, alpha=alpha, chunks=FINISH_CHUNKS),
        out_shape=jax.ShapeDtypeStruct((n, d), F32),
        grid=(n // tm,),
        in_specs=[row(d), row(a), row(c), _resident((1, d)), _resident((1, d)), _resident((a + c, d)),
                  _resident((1, d)), _resident((1, d)), _resident((d, f)), _resident((f, d)),
                  _resident((1, d)), _resident((1, d))],
        out_specs=row(d),
        compiler_params=pltpu.CompilerParams(
            dimension_semantics=("arbitrary",), vmem_limit_bytes=V7X_VMEM_LIMIT_BYTES),
        name="finish",
    )(x2d, attn, conv, vec(ln0_g), vec(ln0_b), w_out_bf, vec(ln1_g), vec(ln1_b), w1_bf, w2_bf,
      vec(ln2_g), vec(ln2_b))


def kernel(x_prompt, x_sample, cache_k, cache_v, cache_conv, ln0_g, ln0_b, rel_bias, w_in, conv_w,
           lambda_q1, lambda_k1, lambda_q2, lambda_k2, subln_g, w_out, ln1_g, ln1_b,
           w_ff1, w_ff2, ln2_g, ln2_b):
    depth = w_in.shape[0]
    assert depth == 1, "single-layer step"
    layer = 0
    b, t, d = x_prompt.shape
    s, ts, _ = x_sample.shape
    past = cache_k.shape[2]
    c = conv_w.shape[-1]
    alpha = (2.0 * depth) ** 0.25
    lam_init = _lambda_init(layer)

    w_in_f = w_in[layer].astype(F32)
    lamv = jnp.stack([lambda_q1[layer], lambda_k1[layer], lambda_q2[layer], lambda_k2[layer]]).astype(F32)

    xp2 = x_prompt.reshape(b * t, d)
    hist0 = jnp.zeros((b, CONV_K - 1, c), F32)
    tq = min(Q_TILE, t)
    qp, kp, vp, kpb, vpb, convp, tailp, w_out_bf, w1_bf, w2_bf = _in_proj(
        xp2, t, ln0_g, ln0_b, w_in_f, conv_w[layer], hist0, qv_tile=tq,
        cast=(w_out[layer].astype(F32), w_ff1[layer].astype(F32), w_ff2[layer].astype(F32)))
    ffn = (ln0_g, ln0_b, w_out_bf, ln1_g[layer], ln1_b[layer], w1_bf, w2_bf, ln2_g[layer], ln2_b[layer], alpha)
    tiles = (b, t // tq, ATTN_WIDTH, tq)
    attnp = _prompt_attention(qp.reshape(tiles), kpb.reshape(b, t, ATTN_WIDTH), vpb.reshape(tiles), rel_bias,
                              lamv, subln_g[layer], lam_init)
    yp = _finish(xp2, attnp.reshape(b * t, ATTN_WIDTH), convp, *ffn).reshape(b, t, d)

    xs2 = x_sample.reshape(s * ts, d)
    qs, ks, vs, ksb, vsb, convs, tails = _in_proj(xs2, ts, ln0_g, ln0_b, w_in_f, conv_w[layer],
                                                  cache_conv[layer].astype(F32))
    shs = (s, ts, ATTN_WIDTH)
    attns = _sample_attention(qs.reshape(shs), ksb.reshape(shs), vsb.reshape(shs),
                              cache_k[layer].astype(F32).reshape(s, past * N_HEADS, HEAD_V),
                              cache_v[layer].astype(F32).reshape(s, past * N_HEADS, HEAD_V),
                              rel_bias, lamv, subln_g[layer], lam_init)
    ys = _finish(xs2, attns.reshape(s * ts, ATTN_WIDTH), convs, *ffn).reshape(s, ts, d)

    keep = CONV_K - 1
    return (yp, ys,
            kp.reshape(1, b, t, N_HEADS, HEAD_V), vp.reshape(1, b, t, N_HEADS, HEAD_V),
            tailp[:, V7X_SUBLANES - keep:, :].reshape(1, b, keep, c),
            ks.reshape(1, s, ts, N_HEADS, HEAD_V), vs.reshape(1, s, ts, N_HEADS, HEAD_V),
            tails[:, V7X_SUBLANES - keep:, :].reshape(1, s, keep, c))
```

```python
import functools
import math

import jax
import jax.numpy as jnp
from jax import lax
from jax.experimental import pallas as pl
from jax.experimental.pallas import tpu as pltpu

CHUNK = 64
N_HEADS = 4
HEAD_V = 128
HEAD_QK = HEAD_V // 2
ATTN_WIDTH = N_HEADS * HEAD_V
CONV_K = 3
N_BUCKETS = 32
MAX_DISTANCE = 128
LN_EPS = 1e-5
SUBLN_EPS = 1e-5
LOG2_E = math.log2(math.e)
Q_SCALE = LOG2_E * HEAD_QK ** -0.5

V7X_SUBLANES = 8
V7X_BF16_SUBLANES = 16
V7X_VMEM_LIMIT_BYTES = 56 * 1024 * 1024

ROW_TILE = 512
IN_PROJ_ROW_TILE = 1024
FINISH_CHUNKS = 2
Q_TILE = 256
K_BLOCK = 1024
HEADS_PER_STEP = 4
TILES_PER_STEP = 4
FAR_BLOCKS = 4
FAR_LAG = 2

F32 = jnp.float32
BF16 = jnp.bfloat16


def _lambda_init(layer):
    return 0.8 - 0.6 * math.exp(-0.3 * layer)


def _layer_norm(x, g, b):
    mu = jnp.mean(x, axis=-1, keepdims=True)
    xc = x - mu
    var = jnp.mean(xc * xc, axis=-1, keepdims=True)
    return xc * lax.rsqrt(var + LN_EPS) * g + b


def _resident(shape):
    nd = len(shape)
    return pl.BlockSpec(shape, lambda *_: (0,) * nd, pipeline_mode=pl.Buffered(1))


def _in_proj_kernel(x_ref, g_ref, b_ref, w_ref, cw_ref, hist_ref, *rest, seg_len, tm, qv_tile, n_cast):
    cast_in, rest = rest[:n_cast], rest[n_cast:]
    q_ref, k_ref, v_ref, kb_ref, vb_ref, conv_ref, utail_ref = rest[:7]
    cast_out, (carry_ref,) = rest[7:7 + n_cast], rest[7 + n_cast:]
    a = ATTN_WIDTH
    c = (w_ref.shape[1] - 3 * a) // 3
    for src, dst in zip(cast_in, cast_out):
        dst[...] = src[...].astype(BF16)
    xn = _layer_norm(x_ref[...], g_ref[...], b_ref[...])
    z = jnp.dot(xn, w_ref[...], preferred_element_type=F32)
    gb = z[:, 3 * a:3 * a + c]
    u = z[:, 3 * a + c:3 * a + 2 * c] * z[:, 3 * a + 2 * c:]
    q = z[:, :a] * Q_SCALE
    k = z[:, a:2 * a]
    v = z[:, 2 * a:3 * a]
    for h in range(N_HEADS):
        k_ref[pl.ds(h, tm, stride=N_HEADS), :] = k[:, h * HEAD_V:(h + 1) * HEAD_V]
        v_ref[pl.ds(h, tm, stride=N_HEADS), :] = v[:, h * HEAD_V:(h + 1) * HEAD_V]
    kb_ref[...] = k.astype(BF16)
    if qv_tile is None:
        q_ref[...] = q.astype(BF16)
        vb_ref[...] = v.astype(BF16)
    else:
        for s in range(tm // qv_tile):
            rows_s = slice(s * qv_tile, (s + 1) * qv_tile)
            q_ref[s] = q[rows_s, :].T.astype(BF16)
            vb_ref[s] = v[rows_s, :].T.astype(BF16)
    rows = lax.broadcasted_iota(jnp.int32, (tm, 1), 0)
    if seg_len >= tm:
        tiles_per_seg = seg_len // tm
        first = (pl.program_id(0) % tiles_per_seg) == 0
        hist = hist_ref[0]
        prev = carry_ref[...]
        p1 = jnp.where(first, hist[1:2, :], prev[V7X_SUBLANES - 1:, :])
        p0 = jnp.where(first, hist[0:1, :], prev[V7X_SUBLANES - 2:V7X_SUBLANES - 1, :])
        pos = rows
        tail = u[tm - V7X_SUBLANES:, :]
        carry_ref[...] = tail
        utail_ref[0] = tail
    else:
        nseg = tm // seg_len
        hist = hist_ref[...]
        p1 = jnp.broadcast_to(hist[:, 1:2, :], (nseg, seg_len, c)).reshape(tm, c)
        p0 = jnp.broadcast_to(hist[:, 0:1, :], (nseg, seg_len, c)).reshape(tm, c)
        pos = rows & (seg_len - 1)
        utail_ref[...] = u.reshape(nseg, seg_len, c)[:, seg_len - V7X_SUBLANES:, :]
    u1 = jnp.where(pos == 0, p1, pltpu.roll(u, 1, 0))
    u2 = jnp.where(pos == 0, p0, jnp.where(pos == 1, p1, pltpu.roll(u, 2, 0)))
    cw = cw_ref[...]
    conv = gb * (u2 * cw[0:1, :] + u1 * cw[1:2, :] + u * cw[2:3, :])
    conv_ref[...] = conv.astype(BF16)


def _in_proj(x2d, seg_len, ln_g, ln_b, w_in, conv_w, hist, qv_tile=None, cast=()):
    n, d = x2d.shape
    e = w_in.shape[1]
    a = ATTN_WIDTH
    c = (e - 3 * a) // 3
    tm = min(IN_PROJ_ROW_TILE, n)
    assert n % tm == 0 and (seg_len % tm == 0 or tm % seg_len == 0)
    assert seg_len & (seg_len - 1) == 0 and seg_len % V7X_SUBLANES == 0
    nseq = n // seg_len
    if seg_len >= tm:
        tiles_per_seg = seg_len // tm
        hist_spec = pl.BlockSpec((1, CONV_K - 1, c), lambda i: (i // tiles_per_seg, 0, 0))
        tail_spec = pl.BlockSpec((1, V7X_SUBLANES, c), lambda i: (i // tiles_per_seg, 0, 0))
    else:
        nseg = tm // seg_len
        hist_spec = pl.BlockSpec((nseg, CONV_K - 1, c), lambda i: (i, 0, 0))
        tail_spec = pl.BlockSpec((nseg, V7X_SUBLANES, c), lambda i: (i, 0, 0))
    row = lambda w: pl.BlockSpec((tm, w), lambda i: (i, 0))
    headrow = pl.BlockSpec((tm * N_HEADS, HEAD_V), lambda i: (i, 0))
    if qv_tile is None:
        qv_shape = jax.ShapeDtypeStruct((n, a), BF16)
        qv_spec = row(a)
    else:
        assert tm % qv_tile == 0
        qv_shape = jax.ShapeDtypeStruct((n // qv_tile, a, qv_tile), BF16)
        qv_spec = pl.BlockSpec((tm // qv_tile, a, qv_tile), lambda i: (i, 0, 0))
    out_shape = (
        qv_shape,
        jax.ShapeDtypeStruct((n * N_HEADS, HEAD_V), F32),
        jax.ShapeDtypeStruct((n * N_HEADS, HEAD_V), F32),
        jax.ShapeDtypeStruct((n, a), BF16),
        qv_shape,
        jax.ShapeDtypeStruct((n, c), BF16),
        jax.ShapeDtypeStruct((nseq, V7X_SUBLANES, c), F32),
    ) + tuple(jax.ShapeDtypeStruct(m.shape, BF16) for m in cast)
    steps = n // tm
    slab_specs = []
    for m in cast:
        assert m.shape[0] % (steps * V7X_BF16_SUBLANES) == 0
        slab_specs.append(pl.BlockSpec((m.shape[0] // steps, m.shape[1]), lambda i: (i, 0)))
    return pl.pallas_call(
        functools.partial(_in_proj_kernel, seg_len=seg_len, tm=tm, qv_tile=qv_tile, n_cast=len(cast)),
        out_shape=out_shape,
        grid=(steps,),
        in_specs=[row(d), _resident((1, d)), _resident((1, d)), _resident((d, e)),
                  _resident((CONV_K, c)), hist_spec, *slab_specs],
        out_specs=(qv_spec, headrow, headrow, row(a), qv_spec, row(c), tail_spec, *slab_specs),
        scratch_shapes=[pltpu.VMEM((V7X_SUBLANES, c), F32)],
        compiler_params=pltpu.CompilerParams(
            dimension_semantics=("arbitrary",), vmem_limit_bytes=V7X_VMEM_LIMIT_BYTES),
        name="in_proj",
    )(x2d, ln_g.reshape(1, d), ln_b.reshape(1, d), w_in, conv_w, hist, *cast)


def _stack_maps(q):
    lane = lax.broadcasted_iota(jnp.int32, q.shape, 1)
    zero = jnp.zeros_like(q)
    return jnp.concatenate([jnp.where(lane < HEAD_QK, q, zero), jnp.where(lane < HEAD_QK, zero, q)], axis=0)


def _scores(qz, kblk):
    return lax.dot_general(qz, kblk, (((1,), (1,)), ((), ())), preferred_element_type=F32)


def _add_to_both_maps(s, bias):
    t, n = bias.shape
    return (s.reshape(2, t, n) + bias[None]).reshape(2 * t, n)


def _lambda(lamv_ref, lam_init):
    lv = lamv_ref[...]
    s1 = jnp.sum(lv[0:1, :] * lv[1:2, :], axis=-1, keepdims=True)
    s2 = jnp.sum(lv[2:3, :] * lv[3:4, :], axis=-1, keepdims=True)
    return jnp.exp(s1) - jnp.exp(s2) + lam_init


def _diff_combine(acc, l, lam, g, lam_init, t):
    o = acc[:t] / l[:t] - lam * (acc[t:] / l[t:])
    ms = jnp.mean(o * o, axis=-1, keepdims=True)
    return o * lax.rsqrt(ms + SUBLN_EPS) * g * (1.0 - lam_init)


def _rel_bucket(rel):
    half = N_BUCKETS // 2
    max_exact = half // 2
    ret = jnp.where(rel > 0, half, 0)
    n = jnp.abs(rel)
    nf = jnp.maximum(n, 1).astype(F32)
    large = max_exact + (jnp.log(nf / max_exact) / math.log(MAX_DISTANCE / max_exact)
                         * (half - max_exact)).astype(jnp.int32)
    large = jnp.minimum(large, half - 1)
    return ret + jnp.where(n < max_exact, n, large)


def _bias_band(rel_bias, n_q, n_k, rel00):
    length = n_q + n_k
    rel = rel00 - (n_q - 1) + jnp.arange(length, dtype=jnp.int32)
    far = rel_bias[N_BUCKETS // 2 - 1].astype(F32)
    w = ((rel_bias[_rel_bucket(rel)].astype(F32) - far[None, :]) * LOG2_E).T
    skew = jnp.tile(w, (1, n_q))[:, :n_q * (length - 1)].reshape(-1, n_q, length - 1)
    return skew[:, :, n_q - 1:n_q - 1 + n_k]


def _chunk_mask(tile, q_pos, k_pos):
    visible = (k_pos[None, :] // CHUNK) <= (q_pos[:, None] // CHUNK)
    return jnp.where(visible[None], tile, -jnp.inf)


SUM_LO = 2.0 ** -60
SUM_HI = 2.0 ** 60


def _prompt_attn_kernel(lamv_ref, g_ref, q_ref, k_ref, v_ref, bd_ref, bp_ref, o_ref,
                        acc_ref, lp_ref, l_ref, m_ref, *, tq, tk, hp, tps, lam_init):
    def tile(t, carry):
        rows = pl.ds(pl.multiple_of(t * tq, tq), tq)
        _prompt_attn_tile(pl.program_id(2) * tps + t, q_ref.at[0, t], o_ref.at[0, rows], lamv_ref, g_ref,
                          k_ref, v_ref, bd_ref, bp_ref, acc_ref, lp_ref, l_ref, m_ref,
                          tq=tq, tk=tk, hp=hp, lam_init=lam_init)
        return carry

    lax.fori_loop(0, tps, tile, 0)


def _prompt_attn_tile(qt, q_ref, o_ref, lamv_ref, g_ref, k_ref, v_ref, bd_ref, bp_ref,
                      acc_ref, lp_ref, l_ref, m_ref, *, tq, tk, hp, lam_init):
    sub = tk // tq
    feats = [slice(h * HEAD_V, (h + 1) * HEAD_V) for h in range(hp)]

    def stack_maps(qT):
        row = lax.broadcasted_iota(jnp.int32, qT.shape, 0)
        zero = jnp.zeros_like(qT)
        return jnp.concatenate([jnp.where(row < HEAD_QK, qT, zero), jnp.where(row < HEAD_QK, zero, qT)], axis=1)

    qz = [stack_maps(q_ref[ft, :]) for ft in feats]

    def kv(j, h):
        start = pl.multiple_of(j * tq, tq)
        return k_ref[0, pl.ds(start, tq), feats[h]], v_ref[0, j, feats[h], :]

    def logits(h, kj, bias):
        s = jnp.dot(kj, qz[h], preferred_element_type=F32)
        if bias is not None:
            s = s + jnp.concatenate([bias, bias], axis=1)
        return s

    def sublane_fold(p):
        return jnp.sum(p.reshape(p.shape[0] // V7X_SUBLANES, V7X_SUBLANES, p.shape[1]), axis=0)

    def numerators(chains, lag, first=False):
        totals = {}
        pending = []

        def values(h, vj, p):
            pv = jnp.dot(vj, p.astype(BF16), preferred_element_type=F32)
            lp = sublane_fold(p)
            totals[h] = (pv, lp) if h not in totals else (totals[h][0] + pv, totals[h][1] + lp)

        for h, j, bias in chains:
            kj, vj = kv(j, h)
            pending.append((h, vj, jnp.exp2(logits(h, kj, bias))))
            if len(pending) > lag:
                values(*pending.pop(0))
        while pending:
            values(*pending.pop(0))
        for h, (pv, lp) in totals.items():
            if first:
                acc_ref[h] = pv
                lp_ref[h] = lp
            else:
                acc_ref[h] += pv
                lp_ref[h] += lp

    heads = range(hp)
    n_far = jnp.maximum(qt - 1, 0)
    n_lead = n_far % sub
    n_blocks = n_far // sub

    @pl.when(qt == 0)
    def _():
        numerators([(h, qt, bd_ref[h]) for h in heads], hp, first=True)

    for lead in range(sub):
        @pl.when((qt > 0) & (n_lead == lead))
        def _(lead=lead):
            near = ((0, bd_ref), (1, bp_ref))
            numerators([(h, qt - d, None if b_ref is None else b_ref[h]) for h in heads
                        for d, b_ref in near + tuple((2 + f, None) for f in range(lead))], hp, first=True)

    def far_region(first_block, blocks):
        numerators([(h, first_block * sub + c, None) for c in range(blocks * sub) for h in heads], FAR_LAG)

    def quad(jb, carry):
        far_region(jb * FAR_BLOCKS, FAR_BLOCKS)
        return carry

    lax.fori_loop(0, n_blocks // FAR_BLOCKS, quad, 0)

    for rem in range(1, FAR_BLOCKS):
        @pl.when(n_blocks % FAR_BLOCKS == rem)
        def _(rem=rem):
            far_region(n_blocks - rem, rem)

    lam = _lambda(lamv_ref, lam_init)

    def write_output():
        for h in range(hp):
            o = acc_ref[h] / l_ref[h]
            o = o[:, :tq] - lam * o[:, tq:]
            ms = jnp.mean(o * o, axis=0, keepdims=True)
            o = o * lax.rsqrt(ms + SUBLN_EPS) * g_ref[...] * (1.0 - lam_init)
            o_ref[:, feats[h]] = o.T.astype(o_ref.dtype)

    for h in range(hp):
        l_ref[h] = jnp.sum(lp_ref[h], axis=0, keepdims=True)
    l_all = l_ref[...]
    in_range = (l_all >= SUM_LO) & (l_all <= SUM_HI)
    write_output()

    @pl.when(jnp.min(jnp.where(in_range, 1.0, 0.0)) < 0.5)
    def _():
        for h in range(hp):
            kd, vd = kv(qt, h)
            s = logits(h, kd, bd_ref[h])
            m0 = jnp.max(s, axis=0, keepdims=True)
            p = jnp.exp2(s - m0)
            m_ref[h] = m0
            l_ref[h] = jnp.sum(p, axis=0, keepdims=True)
            acc_ref[h] = jnp.dot(vd, p.astype(BF16), preferred_element_type=F32)

            def step(j, carry, h=h):
                kj, vj = kv(j, h)
                s = logits(h, kj, jnp.where(j == qt - 1, bp_ref[h], jnp.zeros_like(bp_ref[h])))
                m_prev = m_ref[h]
                m_new = jnp.maximum(m_prev, jnp.max(s, axis=0, keepdims=True))
                alpha = jnp.exp2(m_prev - m_new)
                p = jnp.exp2(s - m_new)
                l_ref[h] = alpha * l_ref[h] + jnp.sum(p, axis=0, keepdims=True)
                acc_ref[h] = alpha * acc_ref[h] + jnp.dot(vj, p.astype(BF16), preferred_element_type=F32)
                m_ref[h] = m_new
                return carry

            lax.fori_loop(0, qt, step, 0)
        write_output()


def _prompt_attention(qT, kb, vT, rel_bias, lamv, subln_g, lam_init):
    b, t, _ = kb.shape
    tq = qT.shape[-1]
    tk = min(K_BLOCK, t)
    sub = tk // tq
    assert t % tq == 0 and tk % tq == 0 and tq % CHUNK == 0 and tq >= MAX_DISTANCE
    pos = jnp.arange(tq, dtype=jnp.int32)
    band = _bias_band(rel_bias, tq, 2 * tq, -tq)
    prev = jnp.swapaxes(band[:, :, :tq], 1, 2)
    diag = jnp.swapaxes(_chunk_mask(band[:, :, tq:], pos, pos), 1, 2)
    gain = jnp.broadcast_to(subln_g.astype(F32)[:, None], (HEAD_V, tq))
    hp = HEADS_PER_STEP
    tps = min(TILES_PER_STEP, t // tq)
    assert N_HEADS % hp == 0 and (t // tq) % tps == 0
    qspec = pl.BlockSpec((1, tps, hp * HEAD_V, tq), lambda bi, g, i: (bi, i, g, 0))
    kspec = pl.BlockSpec((1, t, hp * HEAD_V), lambda bi, g, i: (bi, 0, g))
    vspec = pl.BlockSpec((1, t // tq, hp * HEAD_V, tq), lambda bi, g, i: (bi, 0, g, 0))
    bspec = pl.BlockSpec((hp, tq, tq), lambda bi, g, i: (g, 0, 0))
    return pl.pallas_call(
        functools.partial(_prompt_attn_kernel, tq=tq, tk=tk, hp=hp, tps=tps, lam_init=lam_init),
        out_shape=jax.ShapeDtypeStruct(kb.shape, BF16),
        grid=(b, N_HEADS // hp, t // (tq * tps)),
        in_specs=[_resident(lamv.shape), _resident((HEAD_V, tq)), qspec, kspec, vspec, bspec, bspec],
        out_specs=pl.BlockSpec((1, tq * tps, hp * HEAD_V), lambda bi, g, i: (bi, i, g)),
        scratch_shapes=[pltpu.VMEM((hp, HEAD_V, 2 * tq), F32), pltpu.VMEM((hp, V7X_SUBLANES, 2 * tq), F32),
                        pltpu.VMEM((hp, 1, 2 * tq), F32), pltpu.VMEM((hp, 1, 2 * tq), F32)],
        compiler_params=pltpu.CompilerParams(
            dimension_semantics=("arbitrary", "arbitrary", "arbitrary"),
            vmem_limit_bytes=V7X_VMEM_LIMIT_BYTES),
        name="prompt_attention",
    )(lamv, gain, qT, kb, vT, diag, prev)


def _sample_attn_kernel(lamv_ref, g_ref, q_ref, ck_ref, cv_ref, kn_ref, vn_ref, bias_ref, o_ref,
                        *, ts, past, near, lam_init):
    far = past - near
    lam = _lambda(lamv_ref, lam_init)
    lanes = [slice(h * HEAD_V, (h + 1) * HEAD_V) for h in range(N_HEADS)]

    def logits(h):
        qz = _stack_maps(q_ref[0, :, lanes[h]])
        ck = ck_ref[0, pl.ds(h, past, stride=N_HEADS), :].astype(BF16)
        bias = bias_ref[h]
        return (_scores(qz, ck[:far]),
                _add_to_both_maps(_scores(qz, ck[far:]), bias[:, :near]),
                _add_to_both_maps(_scores(qz, kn_ref[0, :, lanes[h]]), bias[:, near:]))

    def attend(h, s_far, s_near, s_new):
        cv = cv_ref[0, pl.ds(h, past, stride=N_HEADS), :].astype(BF16)
        m = jnp.maximum(jnp.max(s_far, axis=-1, keepdims=True),
                        jnp.maximum(jnp.max(s_near, axis=-1, keepdims=True),
                                    jnp.max(s_new, axis=-1, keepdims=True)))
        p_far = jnp.exp2(s_far - m)
        p_near = jnp.exp2(s_near - m)
        p_new = jnp.exp2(s_new - m)
        l = (jnp.sum(p_far, axis=-1, keepdims=True) + jnp.sum(p_near, axis=-1, keepdims=True)
             + jnp.sum(p_new, axis=-1, keepdims=True))
        acc = (jnp.dot(p_far.astype(BF16), cv[:far], preferred_element_type=F32)
               + jnp.dot(p_near.astype(BF16), cv[far:], preferred_element_type=F32)
               + jnp.dot(p_new.astype(BF16), vn_ref[0, :, lanes[h]], preferred_element_type=F32))
        o_ref[0, :, lanes[h]] = _diff_combine(acc, l, lam, g_ref[...], lam_init, ts).astype(o_ref.dtype)

    s = logits(0)
    for h in range(1, N_HEADS):
        s_next = logits(h)
        attend(h - 1, *s)
        s = s_next
    attend(N_HEADS - 1, *s)


def _sample_attention(q, kb, vb, cache_k, cache_v, rel_bias, lamv, subln_g, lam_init):
    s, ts, _ = q.shape
    past = cache_k.shape[1] // N_HEADS
    near = MAX_DISTANCE
    assert past >= near and past % V7X_SUBLANES == 0
    q_pos = past + jnp.arange(ts, dtype=jnp.int32)
    k_pos = (past - near) + jnp.arange(near + ts, dtype=jnp.int32)
    bias = _chunk_mask(_bias_band(rel_bias, ts, near + ts, -near), q_pos, k_pos)
    new = pl.BlockSpec((1, ts, ATTN_WIDTH), lambda si: (si, 0, 0))
    cache = pl.BlockSpec((1, past * N_HEADS, HEAD_V), lambda si: (si, 0, 0))
    return pl.pallas_call(
        functools.partial(_sample_attn_kernel, ts=ts, past=past, near=near, lam_init=lam_init),
        out_shape=jax.ShapeDtypeStruct(q.shape, BF16),
        grid=(s,),
        in_specs=[_resident(lamv.shape), _resident((1, HEAD_V)), new, cache, cache, new, new,
                  _resident(bias.shape)],
        out_specs=new,
        compiler_params=pltpu.CompilerParams(
            dimension_semantics=("arbitrary",), vmem_limit_bytes=V7X_VMEM_LIMIT_BYTES),
        name="sample_attention",
    )(lamv, subln_g.reshape(1, HEAD_V), q, cache_k, cache_v, kb, vb, bias)


def _finish_kernel(x_ref, a_ref, c_ref, g0_ref, b0_ref, wo_ref, g1_ref, b1_ref, w1_ref, w2_ref,
                   g2_ref, b2_ref, o_ref, *, alpha, chunks):
    rows = x_ref.shape[0] // chunks
    rs = [pl.ds(c * rows, rows) for c in range(chunks)]
    xn, y, x1, h, f = [], [], [], [], []
    for c in range(chunks):
        xn.append(_layer_norm(x_ref[rs[c], :], g0_ref[...], b0_ref[...]))
        mix = jnp.concatenate([a_ref[rs[c], :], c_ref[rs[c], :]], axis=-1)
        y.append(jnp.dot(mix, wo_ref[...], preferred_element_type=F32))
    for c in range(chunks):
        x1.append(_layer_norm(alpha * xn[c] + y[c], g1_ref[...], b1_ref[...]))
        h.append(jnp.dot(x1[c].astype(BF16), w1_ref[...], preferred_element_type=F32))
    for c in range(chunks):
        hc = jnp.square(jnp.maximum(h[c], 0.0)).astype(BF16)
        f.append(jnp.dot(hc, w2_ref[...], preferred_element_type=F32))
    for c in range(chunks):
        o_ref[rs[c], :] = _layer_norm(alpha * x1[c] + f[c], g2_ref[...], b2_ref[...])


def _finish(x2d, attn, conv, ln0_g, ln0_b, w_out_bf, ln1_g, ln1_b, w1_bf, w2_bf, ln2_g, ln2_b, alpha):
    n, d = x2d.shape
    a = attn.shape[1]
    c = conv.shape[1]
    f = w1_bf.shape[1]
    tm = min(ROW_TILE, n)
    assert n % tm == 0
    row = lambda w: pl.BlockSpec((tm, w), lambda i: (i, 0))
    vec = lambda p: p.reshape(1, d)
    return pl.pallas_call(
        functools.partial(_finish_kernel, alpha=alpha, chunks=FINISH_CHUNKS),
        out_shape=jax.ShapeDtypeStruct((n, d), F32),
        grid=(n // tm,),
        in_specs=[row(d), row(a), row(c), _resident((1, d)), _resident((1, d)), _resident((a + c, d)),
                  _resident((1, d)), _resident((1, d)), _resident((d, f)), _resident((f, d)),
                  _resident((1, d)), _resident((1, d))],
        out_specs=row(d),
        compiler_params=pltpu.CompilerParams(
            dimension_semantics=("arbitrary",), vmem_limit_bytes=V7X_VMEM_LIMIT_BYTES),
        name="finish",
    )(x2d, attn, conv, vec(ln0_g), vec(ln0_b), w_out_bf, vec(ln1_g), vec(ln1_b), w1_bf, w2_bf,
      vec(ln2_g), vec(ln2_b))


def kernel(x_prompt, x_sample, cache_k, cache_v, cache_conv, ln0_g, ln0_b, rel_bias, w_in, conv_w,
           lambda_q1, lambda_k1, lambda_q2, lambda_k2, subln_g, w_out, ln1_g, ln1_b,
           w_ff1, w_ff2, ln2_g, ln2_b):
    depth = w_in.shape[0]
    assert depth == 1, "single-layer step"
    layer = 0
    b, t, d = x_prompt.shape
    s, ts, _ = x_sample.shape
    past = cache_k.shape[2]
    c = conv_w.shape[-1]
    alpha = (2.0 * depth) ** 0.25
    lam_init = _lambda_init(layer)

    w_in_f = w_in[layer].astype(F32)
    lamv = jnp.stack([lambda_q1[layer], lambda_k1[layer], lambda_q2[layer], lambda_k2[layer]]).astype(F32)

    xp2 = x_prompt.reshape(b * t, d)
    hist0 = jnp.zeros((b, CONV_K - 1, c), F32)
    tq = min(Q_TILE, t)
    qp, kp, vp, kpb, vpb, convp, tailp, w_out_bf, w1_bf, w2_bf = _in_proj(
        xp2, t, ln0_g, ln0_b, w_in_f, conv_w[layer], hist0, qv_tile=tq,
        cast=(w_out[layer].astype(F32), w_ff1[layer].astype(F32), w_ff2[layer].astype(F32)))
    ffn = (ln0_g, ln0_b, w_out_bf, ln1_g[layer], ln1_b[layer], w1_bf, w2_bf, ln2_g[layer], ln2_b[layer], alpha)
    tiles = (b, t // tq, ATTN_WIDTH, tq)
    attnp = _prompt_attention(qp.reshape(tiles), kpb.reshape(b, t, ATTN_WIDTH), vpb.reshape(tiles), rel_bias,
                              lamv, subln_g[layer], lam_init)
    yp = _finish(xp2, attnp.reshape(b * t, ATTN_WIDTH), convp, *ffn).reshape(b, t, d)

    xs2 = x_sample.reshape(s * ts, d)
    qs, ks, vs, ksb, vsb, convs, tails = _in_proj(xs2, ts, ln0_g, ln0_b, w_in_f, conv_w[layer],
                                                  cache_conv[layer].astype(F32))
    shs = (s, ts, ATTN_WIDTH)
    attns = _sample_attention(qs.reshape(shs), ksb.reshape(shs), vsb.reshape(shs),
                              cache_k[layer].astype(F32).reshape(s, past * N_HEADS, HEAD_V),
                              cache_v[layer].astype(F32).reshape(s, past * N_HEADS, HEAD_V),
                              rel_bias, lamv, subln_g[layer], lam_init)
    ys = _finish(xs2, attns.reshape(s * ts, ATTN_WIDTH), convs, *ffn).reshape(s, ts, d)

    keep = CONV_K - 1
    return (yp, ys,
            kp.reshape(1, b, t, N_HEADS, HEAD_V), vp.reshape(1, b, t, N_HEADS, HEAD_V),
            tailp[:, V7X_SUBLANES - keep:, :].reshape(1, b, keep, c),
            ks.reshape(1, s, ts, N_HEADS, HEAD_V), vs.reshape(1, s, ts, N_HEADS, HEAD_V),
            tails[:, V7X_SUBLANES - keep:, :].reshape(1, s, keep, c))
```

```python
import functools
import math

import jax
import jax.numpy as jnp
from jax import lax
from jax.experimental import pallas as pl
from jax.experimental.pallas import tpu as pltpu

CHUNK = 64
N_HEADS = 4
HEAD_V = 128
HEAD_QK = HEAD_V // 2
ATTN_WIDTH = N_HEADS * HEAD_V
CONV_K = 3
N_BUCKETS = 32
MAX_DISTANCE = 128
LN_EPS = 1e-5
SUBLN_EPS = 1e-5
LOG2_E = math.log2(math.e)
Q_SCALE = LOG2_E * HEAD_QK ** -0.5

V7X_SUBLANES = 8
V7X_BF16_SUBLANES = 16
V7X_VMEM_LIMIT_BYTES = 56 * 1024 * 1024

ROW_TILE = 512
IN_PROJ_ROW_TILE = 1024
FINISH_CHUNKS = 2
Q_TILE = 256
K_BLOCK = 1024
HEADS_PER_STEP = 4
TILES_PER_STEP = 4
CACHE_RING = 3
FAR_BLOCKS = 4
FAR_LAG = 2

F32 = jnp.float32
BF16 = jnp.bfloat16


def _lambda_init(layer):
    return 0.8 - 0.6 * math.exp(-0.3 * layer)


def _layer_norm(x, g, b):
    mu = jnp.mean(x, axis=-1, keepdims=True)
    xc = x - mu
    var = jnp.mean(xc * xc, axis=-1, keepdims=True)
    return xc * lax.rsqrt(var + LN_EPS) * g + b


def _resident(shape):
    nd = len(shape)
    return pl.BlockSpec(shape, lambda *_: (0,) * nd, pipeline_mode=pl.Buffered(1))


def _in_proj_kernel(x_ref, g_ref, b_ref, w_ref, cw_ref, hist_ref, *rest, seg_len, tm, qv_tile, n_cast):
    cast_in, rest = rest[:n_cast], rest[n_cast:]
    q_ref, k_ref, v_ref, kb_ref, vb_ref, conv_ref, utail_ref = rest[:7]
    cast_out, (carry_ref,) = rest[7:7 + n_cast], rest[7 + n_cast:]
    a = ATTN_WIDTH
    c = (w_ref.shape[1] - 3 * a) // 3
    for src, dst in zip(cast_in, cast_out):
        dst[...] = src[...].astype(BF16)
    xn = _layer_norm(x_ref[...], g_ref[...], b_ref[...])
    z = jnp.dot(xn, w_ref[...], preferred_element_type=F32)
    gb = z[:, 3 * a:3 * a + c]
    u = z[:, 3 * a + c:3 * a + 2 * c] * z[:, 3 * a + 2 * c:]
    q = z[:, :a] * Q_SCALE
    k = z[:, a:2 * a]
    v = z[:, 2 * a:3 * a]
    for h in range(N_HEADS):
        k_ref[pl.ds(h, tm, stride=N_HEADS), :] = k[:, h * HEAD_V:(h + 1) * HEAD_V]
        v_ref[pl.ds(h, tm, stride=N_HEADS), :] = v[:, h * HEAD_V:(h + 1) * HEAD_V]
    kb_ref[...] = k.astype(BF16)
    if qv_tile is None:
        q_ref[...] = q.astype(BF16)
        vb_ref[...] = v.astype(BF16)
    else:
        for s in range(tm // qv_tile):
            rows_s = slice(s * qv_tile, (s + 1) * qv_tile)
            q_ref[s] = q[rows_s, :].T.astype(BF16)
            vb_ref[s] = v[rows_s, :].T.astype(BF16)
    rows = lax.broadcasted_iota(jnp.int32, (tm, 1), 0)
    if seg_len >= tm:
        tiles_per_seg = seg_len // tm
        first = (pl.program_id(0) % tiles_per_seg) == 0
        hist = hist_ref[0]
        prev = carry_ref[...]
        p1 = jnp.where(first, hist[1:2, :], prev[V7X_SUBLANES - 1:, :])
        p0 = jnp.where(first, hist[0:1, :], prev[V7X_SUBLANES - 2:V7X_SUBLANES - 1, :])
        pos = rows
        tail = u[tm - V7X_SUBLANES:, :]
        carry_ref[...] = tail
        utail_ref[0] = tail
    else:
        nseg = tm // seg_len
        hist = hist_ref[...]
        p1 = jnp.broadcast_to(hist[:, 1:2, :], (nseg, seg_len, c)).reshape(tm, c)
        p0 = jnp.broadcast_to(hist[:, 0:1, :], (nseg, seg_len, c)).reshape(tm, c)
        pos = rows & (seg_len - 1)
        utail_ref[...] = u.reshape(nseg, seg_len, c)[:, seg_len - V7X_SUBLANES:, :]
    u1 = jnp.where(pos == 0, p1, pltpu.roll(u, 1, 0))
    u2 = jnp.where(pos == 0, p0, jnp.where(pos == 1, p1, pltpu.roll(u, 2, 0)))
    cw = cw_ref[...]
    conv = gb * (u2 * cw[0:1, :] + u1 * cw[1:2, :] + u * cw[2:3, :])
    conv_ref[...] = conv.astype(BF16)


def _in_proj(x2d, seg_len, ln_g, ln_b, w_in, conv_w, hist, qv_tile=None, cast=()):
    n, d = x2d.shape
    e = w_in.shape[1]
    a = ATTN_WIDTH
    c = (e - 3 * a) // 3
    tm = min(IN_PROJ_ROW_TILE, n)
    assert n % tm == 0 and (seg_len % tm == 0 or tm % seg_len == 0)
    assert seg_len & (seg_len - 1) == 0 and seg_len % V7X_SUBLANES == 0
    nseq = n // seg_len
    if seg_len >= tm:
        tiles_per_seg = seg_len // tm
        hist_spec = pl.BlockSpec((1, CONV_K - 1, c), lambda i: (i // tiles_per_seg, 0, 0))
        tail_spec = pl.BlockSpec((1, V7X_SUBLANES, c), lambda i: (i // tiles_per_seg, 0, 0))
    else:
        nseg = tm // seg_len
        hist_spec = pl.BlockSpec((nseg, CONV_K - 1, c), lambda i: (i, 0, 0))
        tail_spec = pl.BlockSpec((nseg, V7X_SUBLANES, c), lambda i: (i, 0, 0))
    row = lambda w: pl.BlockSpec((tm, w), lambda i: (i, 0))
    headrow = pl.BlockSpec((tm * N_HEADS, HEAD_V), lambda i: (i, 0))
    if qv_tile is None:
        qv_shape = jax.ShapeDtypeStruct((n, a), BF16)
        qv_spec = row(a)
    else:
        assert tm % qv_tile == 0
        qv_shape = jax.ShapeDtypeStruct((n // qv_tile, a, qv_tile), BF16)
        qv_spec = pl.BlockSpec((tm // qv_tile, a, qv_tile), lambda i: (i, 0, 0))
    out_shape = (
        qv_shape,
        jax.ShapeDtypeStruct((n * N_HEADS, HEAD_V), F32),
        jax.ShapeDtypeStruct((n * N_HEADS, HEAD_V), F32),
        jax.ShapeDtypeStruct((n, a), BF16),
        qv_shape,
        jax.ShapeDtypeStruct((n, c), BF16),
        jax.ShapeDtypeStruct((nseq, V7X_SUBLANES, c), F32),
    ) + tuple(jax.ShapeDtypeStruct(m.shape, BF16) for m in cast)
    steps = n // tm
    slab_specs = []
    for m in cast:
        assert m.shape[0] % (steps * V7X_BF16_SUBLANES) == 0
        slab_specs.append(pl.BlockSpec((m.shape[0] // steps, m.shape[1]), lambda i: (i, 0)))
    return pl.pallas_call(
        functools.partial(_in_proj_kernel, seg_len=seg_len, tm=tm, qv_tile=qv_tile, n_cast=len(cast)),
        out_shape=out_shape,
        grid=(steps,),
        in_specs=[row(d), _resident((1, d)), _resident((1, d)), _resident((d, e)),
                  _resident((CONV_K, c)), hist_spec, *slab_specs],
        out_specs=(qv_spec, headrow, headrow, row(a), qv_spec, row(c), tail_spec, *slab_specs),
        scratch_shapes=[pltpu.VMEM((V7X_SUBLANES, c), F32)],
        compiler_params=pltpu.CompilerParams(
            dimension_semantics=("arbitrary",), vmem_limit_bytes=V7X_VMEM_LIMIT_BYTES),
        name="in_proj",
    )(x2d, ln_g.reshape(1, d), ln_b.reshape(1, d), w_in, conv_w, hist, *cast)


def _stack_maps(q):
    lane = lax.broadcasted_iota(jnp.int32, q.shape, 1)
    zero = jnp.zeros_like(q)
    return jnp.concatenate([jnp.where(lane < HEAD_QK, q, zero), jnp.where(lane < HEAD_QK, zero, q)], axis=0)


def _scores(qz, kblk):
    return lax.dot_general(qz, kblk, (((1,), (1,)), ((), ())), preferred_element_type=F32)


def _add_to_both_maps(s, bias):
    t, n = bias.shape
    return (s.reshape(2, t, n) + bias[None]).reshape(2 * t, n)


def _lambda(lamv_ref, lam_init):
    lv = lamv_ref[...]
    s1 = jnp.sum(lv[0:1, :] * lv[1:2, :], axis=-1, keepdims=True)
    s2 = jnp.sum(lv[2:3, :] * lv[3:4, :], axis=-1, keepdims=True)
    return jnp.exp(s1) - jnp.exp(s2) + lam_init


def _diff_combine(acc, l, lam, g, lam_init, t):
    o = acc[:t] / l[:t] - lam * (acc[t:] / l[t:])
    ms = jnp.mean(o * o, axis=-1, keepdims=True)
    return o * lax.rsqrt(ms + SUBLN_EPS) * g * (1.0 - lam_init)


def _rel_bucket(rel):
    half = N_BUCKETS // 2
    max_exact = half // 2
    ret = jnp.where(rel > 0, half, 0)
    n = jnp.abs(rel)
    nf = jnp.maximum(n, 1).astype(F32)
    large = max_exact + (jnp.log(nf / max_exact) / math.log(MAX_DISTANCE / max_exact)
                         * (half - max_exact)).astype(jnp.int32)
    large = jnp.minimum(large, half - 1)
    return ret + jnp.where(n < max_exact, n, large)


def _bias_band(rel_bias, n_q, n_k, rel00):
    length = n_q + n_k
    rel = rel00 - (n_q - 1) + jnp.arange(length, dtype=jnp.int32)
    far = rel_bias[N_BUCKETS // 2 - 1].astype(F32)
    w = ((rel_bias[_rel_bucket(rel)].astype(F32) - far[None, :]) * LOG2_E).T
    skew = jnp.tile(w, (1, n_q))[:, :n_q * (length - 1)].reshape(-1, n_q, length - 1)
    return skew[:, :, n_q - 1:n_q - 1 + n_k]


def _chunk_mask(tile, q_pos, k_pos):
    visible = (k_pos[None, :] // CHUNK) <= (q_pos[:, None] // CHUNK)
    return jnp.where(visible[None], tile, -jnp.inf)


SUM_LO = 2.0 ** -60
SUM_HI = 2.0 ** 60


def _prompt_attn_kernel(lamv_ref, g_ref, q_ref, k_ref, v_ref, bd_ref, bp_ref, o_ref,
                        acc_ref, lp_ref, l_ref, m_ref, *, tq, tk, hp, tps, lam_init):
    def tile(t, carry):
        rows = pl.ds(pl.multiple_of(t * tq, tq), tq)
        _prompt_attn_tile(pl.program_id(2) * tps + t, q_ref.at[0, t], o_ref.at[0, rows], lamv_ref, g_ref,
                          k_ref, v_ref, bd_ref, bp_ref, acc_ref, lp_ref, l_ref, m_ref,
                          tq=tq, tk=tk, hp=hp, lam_init=lam_init)
        return carry

    lax.fori_loop(0, tps, tile, 0)


def _prompt_attn_tile(qt, q_ref, o_ref, lamv_ref, g_ref, k_ref, v_ref, bd_ref, bp_ref,
                      acc_ref, lp_ref, l_ref, m_ref, *, tq, tk, hp, lam_init):
    sub = tk // tq
    feats = [slice(h * HEAD_V, (h + 1) * HEAD_V) for h in range(hp)]

    def stack_maps(qT):
        row = lax.broadcasted_iota(jnp.int32, qT.shape, 0)
        zero = jnp.zeros_like(qT)
        return jnp.concatenate([jnp.where(row < HEAD_QK, qT, zero), jnp.where(row < HEAD_QK, zero, qT)], axis=1)

    qz = [stack_maps(q_ref[ft, :]) for ft in feats]

    def kv(j, h):
        start = pl.multiple_of(j * tq, tq)
        return k_ref[0, pl.ds(start, tq), feats[h]], v_ref[0, j, feats[h], :]

    def logits(h, kj, bias):
        s = jnp.dot(kj, qz[h], preferred_element_type=F32)
        if bias is not None:
            s = s + jnp.concatenate([bias, bias], axis=1)
        return s

    def sublane_fold(p):
        return jnp.sum(p.reshape(p.shape[0] // V7X_SUBLANES, V7X_SUBLANES, p.shape[1]), axis=0)

    def numerators(chains, lag, first=False):
        totals = {}
        pending = []

        def values(h, vj, p):
            pv = jnp.dot(vj, p.astype(BF16), preferred_element_type=F32)
            lp = sublane_fold(p)
            totals[h] = (pv, lp) if h not in totals else (totals[h][0] + pv, totals[h][1] + lp)

        for h, j, bias in chains:
            kj, vj = kv(j, h)
            pending.append((h, vj, jnp.exp2(logits(h, kj, bias))))
            if len(pending) > lag:
                values(*pending.pop(0))
        while pending:
            values(*pending.pop(0))
        for h, (pv, lp) in totals.items():
            if first:
                acc_ref[h] = pv
                lp_ref[h] = lp
            else:
                acc_ref[h] += pv
                lp_ref[h] += lp

    heads = range(hp)
    n_far = jnp.maximum(qt - 1, 0)
    n_lead = n_far % sub
    n_blocks = n_far // sub

    @pl.when(qt == 0)
    def _():
        numerators([(h, qt, bd_ref[h]) for h in heads], hp, first=True)

    for lead in range(sub):
        @pl.when((qt > 0) & (n_lead == lead))
        def _(lead=lead):
            near = ((0, bd_ref), (1, bp_ref))
            numerators([(h, qt - d, None if b_ref is None else b_ref[h]) for h in heads
                        for d, b_ref in near + tuple((2 + f, None) for f in range(lead))], hp, first=True)

    def far_region(first_block, blocks):
        numerators([(h, first_block * sub + c, None) for c in range(blocks * sub) for h in heads], FAR_LAG)

    def quad(jb, carry):
        far_region(jb * FAR_BLOCKS, FAR_BLOCKS)
        return carry

    lax.fori_loop(0, n_blocks // FAR_BLOCKS, quad, 0)

    for rem in range(1, FAR_BLOCKS):
        @pl.when(n_blocks % FAR_BLOCKS == rem)
        def _(rem=rem):
            far_region(n_blocks - rem, rem)

    lam = _lambda(lamv_ref, lam_init)

    def write_output():
        for h in range(hp):
            o = acc_ref[h] / l_ref[h]
            o = o[:, :tq] - lam * o[:, tq:]
            ms = jnp.mean(o * o, axis=0, keepdims=True)
            o = o * lax.rsqrt(ms + SUBLN_EPS) * g_ref[...] * (1.0 - lam_init)
            o_ref[:, feats[h]] = o.T.astype(o_ref.dtype)

    for h in range(hp):
        l_ref[h] = jnp.sum(lp_ref[h], axis=0, keepdims=True)
    l_all = l_ref[...]
    in_range = (l_all >= SUM_LO) & (l_all <= SUM_HI)
    write_output()

    @pl.when(jnp.min(jnp.where(in_range, 1.0, 0.0)) < 0.5)
    def _():
        for h in range(hp):
            kd, vd = kv(qt, h)
            s = logits(h, kd, bd_ref[h])
            m0 = jnp.max(s, axis=0, keepdims=True)
            p = jnp.exp2(s - m0)
            m_ref[h] = m0
            l_ref[h] = jnp.sum(p, axis=0, keepdims=True)
            acc_ref[h] = jnp.dot(vd, p.astype(BF16), preferred_element_type=F32)

            def step(j, carry, h=h):
                kj, vj = kv(j, h)
                s = logits(h, kj, jnp.where(j == qt - 1, bp_ref[h], jnp.zeros_like(bp_ref[h])))
                m_prev = m_ref[h]
                m_new = jnp.maximum(m_prev, jnp.max(s, axis=0, keepdims=True))
                alpha = jnp.exp2(m_prev - m_new)
                p = jnp.exp2(s - m_new)
                l_ref[h] = alpha * l_ref[h] + jnp.sum(p, axis=0, keepdims=True)
                acc_ref[h] = alpha * acc_ref[h] + jnp.dot(vj, p.astype(BF16), preferred_element_type=F32)
                m_ref[h] = m_new
                return carry

            lax.fori_loop(0, qt, step, 0)
        write_output()


def _prompt_attention(qT, kb, vT, rel_bias, lamv, subln_g, lam_init):
    b, t, _ = kb.shape
    tq = qT.shape[-1]
    tk = min(K_BLOCK, t)
    sub = tk // tq
    assert t % tq == 0 and tk % tq == 0 and tq % CHUNK == 0 and tq >= MAX_DISTANCE
    pos = jnp.arange(tq, dtype=jnp.int32)
    band = _bias_band(rel_bias, tq, 2 * tq, -tq)
    prev = jnp.swapaxes(band[:, :, :tq], 1, 2)
    diag = jnp.swapaxes(_chunk_mask(band[:, :, tq:], pos, pos), 1, 2)
    gain = jnp.broadcast_to(subln_g.astype(F32)[:, None], (HEAD_V, tq))
    hp = HEADS_PER_STEP
    tps = min(TILES_PER_STEP, t // tq)
    assert N_HEADS % hp == 0 and (t // tq) % tps == 0
    qspec = pl.BlockSpec((1, tps, hp * HEAD_V, tq), lambda bi, g, i: (bi, i, g, 0))
    kspec = pl.BlockSpec((1, t, hp * HEAD_V), lambda bi, g, i: (bi, 0, g))
    vspec = pl.BlockSpec((1, t // tq, hp * HEAD_V, tq), lambda bi, g, i: (bi, 0, g, 0))
    bspec = pl.BlockSpec((hp, tq, tq), lambda bi, g, i: (g, 0, 0))
    return pl.pallas_call(
        functools.partial(_prompt_attn_kernel, tq=tq, tk=tk, hp=hp, tps=tps, lam_init=lam_init),
        out_shape=jax.ShapeDtypeStruct(kb.shape, BF16),
        grid=(b, N_HEADS // hp, t // (tq * tps)),
        in_specs=[_resident(lamv.shape), _resident((HEAD_V, tq)), qspec, kspec, vspec, bspec, bspec],
        out_specs=pl.BlockSpec((1, tq * tps, hp * HEAD_V), lambda bi, g, i: (bi, i, g)),
        scratch_shapes=[pltpu.VMEM((hp, HEAD_V, 2 * tq), F32), pltpu.VMEM((hp, V7X_SUBLANES, 2 * tq), F32),
                        pltpu.VMEM((hp, 1, 2 * tq), F32), pltpu.VMEM((hp, 1, 2 * tq), F32)],
        compiler_params=pltpu.CompilerParams(
            dimension_semantics=("arbitrary", "arbitrary", "arbitrary"),
            vmem_limit_bytes=V7X_VMEM_LIMIT_BYTES),
        name="prompt_attention",
    )(lamv, gain, qT, kb, vT, diag, prev)


def _sample_attn_kernel(lamv_ref, g_ref, q_ref, ck_hbm, cv_hbm, kn_ref, vn_ref, bias_ref, o_ref,
                        ck_buf, cv_buf, sem, *, ts, past, near, lam_init, n_streams):
    step = pl.program_id(0)

    def cache_copies(stream, slot):
        return (pltpu.make_async_copy(ck_hbm.at[stream], ck_buf.at[slot], sem.at[0, slot]),
                pltpu.make_async_copy(cv_hbm.at[stream], cv_buf.at[slot], sem.at[1, slot]))

    @pl.when(step == 0)
    def _():
        for ahead in range(min(CACHE_RING - 1, n_streams)):
            for copy in cache_copies(ahead, ahead):
                copy.start()

    ahead = step + CACHE_RING - 1

    @pl.when(ahead < n_streams)
    def _():
        for copy in cache_copies(ahead, ahead % CACHE_RING):
            copy.start()

    slot = step % CACHE_RING
    for copy in cache_copies(step, slot):
        copy.wait()
    ck_ref = ck_buf.at[slot]
    cv_ref = cv_buf.at[slot]
    far = past - near
    lam = _lambda(lamv_ref, lam_init)
    lanes = [slice(h * HEAD_V, (h + 1) * HEAD_V) for h in range(N_HEADS)]

    def logits(h):
        qz = _stack_maps(q_ref[0, :, lanes[h]])
        ck = ck_ref[pl.ds(h, past, stride=N_HEADS), :].astype(BF16)
        bias = bias_ref[h]
        return (_scores(qz, ck[:far]),
                _add_to_both_maps(_scores(qz, ck[far:]), bias[:, :near]),
                _add_to_both_maps(_scores(qz, kn_ref[0, :, lanes[h]]), bias[:, near:]))

    def attend(h, s_far, s_near, s_new):
        cv = cv_ref[pl.ds(h, past, stride=N_HEADS), :].astype(BF16)
        m = jnp.maximum(jnp.max(s_far, axis=-1, keepdims=True),
                        jnp.maximum(jnp.max(s_near, axis=-1, keepdims=True),
                                    jnp.max(s_new, axis=-1, keepdims=True)))
        p_far = jnp.exp2(s_far - m)
        p_near = jnp.exp2(s_near - m)
        p_new = jnp.exp2(s_new - m)
        l = (jnp.sum(p_far, axis=-1, keepdims=True) + jnp.sum(p_near, axis=-1, keepdims=True)
             + jnp.sum(p_new, axis=-1, keepdims=True))
        acc = (jnp.dot(p_far.astype(BF16), cv[:far], preferred_element_type=F32)
               + jnp.dot(p_near.astype(BF16), cv[far:], preferred_element_type=F32)
               + jnp.dot(p_new.astype(BF16), vn_ref[0, :, lanes[h]], preferred_element_type=F32))
        o_ref[0, :, lanes[h]] = _diff_combine(acc, l, lam, g_ref[...], lam_init, ts).astype(o_ref.dtype)

    s = logits(0)
    for h in range(1, N_HEADS):
        s_next = logits(h)
        attend(h - 1, *s)
        s = s_next
    attend(N_HEADS - 1, *s)


def _sample_attention(q, kb, vb, cache_k, cache_v, rel_bias, lamv, subln_g, lam_init):
    s, ts, _ = q.shape
    past = cache_k.shape[1] // N_HEADS
    near = MAX_DISTANCE
    assert past >= near and past % V7X_SUBLANES == 0
    q_pos = past + jnp.arange(ts, dtype=jnp.int32)
    k_pos = (past - near) + jnp.arange(near + ts, dtype=jnp.int32)
    bias = _chunk_mask(_bias_band(rel_bias, ts, near + ts, -near), q_pos, k_pos)
    new = pl.BlockSpec((1, ts, ATTN_WIDTH), lambda si: (si, 0, 0))
    cache = pl.BlockSpec(memory_space=pl.ANY)
    return pl.pallas_call(
        functools.partial(_sample_attn_kernel, ts=ts, past=past, near=near, lam_init=lam_init, n_streams=s),
        out_shape=jax.ShapeDtypeStruct(q.shape, BF16),
        grid=(s,),
        in_specs=[_resident(lamv.shape), _resident((1, HEAD_V)), new, cache, cache, new, new,
                  _resident(bias.shape)],
        out_specs=new,
        scratch_shapes=[pltpu.VMEM((CACHE_RING, past * N_HEADS, HEAD_V), F32),
                        pltpu.VMEM((CACHE_RING, past * N_HEADS, HEAD_V), F32),
                        pltpu.SemaphoreType.DMA((2, CACHE_RING))],
        compiler_params=pltpu.CompilerParams(
            dimension_semantics=("arbitrary",), vmem_limit_bytes=V7X_VMEM_LIMIT_BYTES),
        name="sample_attention",
    )(lamv, subln_g.reshape(1, HEAD_V), q, cache_k, cache_v, kb, vb, bias)


def _finish_kernel(x_ref, a_ref, c_ref, g0_ref, b0_ref, wo_ref, g1_ref, b1_ref, w1_ref, w2_ref,
                   g2_ref, b2_ref, o_ref, *, alpha, chunks):
    rows = x_ref.shape[0] // chunks
    rs = [pl.ds(c * rows, rows) for c in range(chunks)]
    xn, y, x1, h, f = [], [], [], [], []
    for c in range(chunks):
        xn.append(_layer_norm(x_ref[rs[c], :], g0_ref[...], b0_ref[...]))
        mix = jnp.concatenate([a_ref[rs[c], :], c_ref[rs[c], :]], axis=-1)
        y.append(jnp.dot(mix, wo_ref[...], preferred_element_type=F32))
    for c in range(chunks):
        x1.append(_layer_norm(alpha * xn[c] + y[c], g1_ref[...], b1_ref[...]))
        h.append(jnp.dot(x1[c].astype(BF16), w1_ref[...], preferred_element_type=F32))
    for c in range(chunks):
        hc = jnp.square(jnp.maximum(h[c], 0.0)).astype(BF16)
        f.append(jnp.dot(hc, w2_ref[...], preferred_element_type=F32))
    for c in range(chunks):
        o_ref[rs[c], :] = _layer_norm(alpha * x1[c] + f[c], g2_ref[...], b2_ref[...])


def _finish(x2d, attn, conv, ln0_g, ln0_b, w_out_bf, ln1_g, ln1_b, w1_bf, w2_bf, ln2_g, ln2_b, alpha):
    n, d = x2d.shape
    a = attn.shape[1]
    c = conv.shape[1]
    f = w1_bf.shape[1]
    tm = min(ROW_TILE, n)
    assert n % tm == 0
    row = lambda w: pl.BlockSpec((tm, w), lambda i: (i, 0))
    vec = lambda p: p.reshape(1, d)
    return pl.pallas_call(
        functools.partial(_finish_kernel, alpha=alpha, chunks=FINISH_CHUNKS),
        out_shape=jax.ShapeDtypeStruct((n, d), F32),
        grid=(n // tm,),
        in_specs=[row(d), row(a), row(c), _resident((1, d)), _resident((1, d)), _resident((a + c, d)),
                  _resident((1, d)), _resident((1, d)), _resident((d, f)), _resident((f, d)),
                  _resident((1, d)), _resident((1, d))],
        out_specs=row(d),
        compiler_params=pltpu.CompilerParams(
            dimension_semantics=("arbitrary",), vmem_limit_bytes=V7X_VMEM_LIMIT_BYTES),
        name="finish",
    )(x2d, attn, conv, vec(ln0_g), vec(ln0_b), w_out_bf, vec(ln1_g), vec(ln1_b), w1_bf, w2_bf,
      vec(ln2_g), vec(ln2_b))


def kernel(x_prompt, x_sample, cache_k, cache_v, cache_conv, ln0_g, ln0_b, rel_bias, w_in, conv_w,
           lambda_q1, lambda_k1, lambda_q2, lambda_k2, subln_g, w_out, ln1_g, ln1_b,
           w_ff1, w_ff2, ln2_g, ln2_b):
    depth = w_in.shape[0]
    assert depth == 1, "single-layer step"
    layer = 0
    b, t, d = x_prompt.shape
    s, ts, _ = x_sample.shape
    past = cache_k.shape[2]
    c = conv_w.shape[-1]
    alpha = (2.0 * depth) ** 0.25
    lam_init = _lambda_init(layer)

    w_in_f = w_in[layer].astype(F32)
    lamv = jnp.stack([lambda_q1[layer], lambda_k1[layer], lambda_q2[layer], lambda_k2[layer]]).astype(F32)

    xp2 = x_prompt.reshape(b * t, d)
    hist0 = jnp.zeros((b, CONV_K - 1, c), F32)
    tq = min(Q_TILE, t)
    qp, kp, vp, kpb, vpb, convp, tailp, w_out_bf, w1_bf, w2_bf = _in_proj(
        xp2, t, ln0_g, ln0_b, w_in_f, conv_w[layer], hist0, qv_tile=tq,
        cast=(w_out[layer].astype(F32), w_ff1[layer].astype(F32), w_ff2[layer].astype(F32)))
    ffn = (ln0_g, ln0_b, w_out_bf, ln1_g[layer], ln1_b[layer], w1_bf, w2_bf, ln2_g[layer], ln2_b[layer], alpha)
    tiles = (b, t // tq, ATTN_WIDTH, tq)
    attnp = _prompt_attention(qp.reshape(tiles), kpb.reshape(b, t, ATTN_WIDTH), vpb.reshape(tiles), rel_bias,
                              lamv, subln_g[layer], lam_init)
    yp = _finish(xp2, attnp.reshape(b * t, ATTN_WIDTH), convp, *ffn).reshape(b, t, d)

    xs2 = x_sample.reshape(s * ts, d)
    qs, ks, vs, ksb, vsb, convs, tails = _in_proj(xs2, ts, ln0_g, ln0_b, w_in_f, conv_w[layer],
                                                  cache_conv[layer].astype(F32))
    shs = (s, ts, ATTN_WIDTH)
    attns = _sample_attention(qs.reshape(shs), ksb.reshape(shs), vsb.reshape(shs),
                              cache_k[layer].astype(F32).reshape(s, past * N_HEADS, HEAD_V),
                              cache_v[layer].astype(F32).reshape(s, past * N_HEADS, HEAD_V),
                              rel_bias, lamv, subln_g[layer], lam_init)
    ys = _finish(xs2, attns.reshape(s * ts, ATTN_WIDTH), convs, *ffn).reshape(s, ts, d)

    keep = CONV_K - 1
    return (yp, ys,
            kp.reshape(1, b, t, N_HEADS, HEAD_V), vp.reshape(1, b, t, N_HEADS, HEAD_V),
            tailp[:, V7X_SUBLANES - keep:, :].reshape(1, b, keep, c),
            ks.reshape(1, s, ts, N_HEADS, HEAD_V), vs.reshape(1, s, ts, N_HEADS, HEAD_V),
            tails[:, V7X_SUBLANES - keep:, :].reshape(1, s, keep, c))
```

```python
import functools
import math

import jax
import jax.numpy as jnp
from jax import lax
from jax.experimental import pallas as pl
from jax.experimental.pallas import tpu as pltpu

CHUNK = 64
N_HEADS = 4
HEAD_V = 128
HEAD_QK = HEAD_V // 2
ATTN_WIDTH = N_HEADS * HEAD_V
CONV_K = 3
N_BUCKETS = 32
MAX_DISTANCE = 128
LN_EPS = 1e-5
SUBLN_EPS = 1e-5
LOG2_E = math.log2(math.e)
Q_SCALE = LOG2_E * HEAD_QK ** -0.5

V7X_SUBLANES = 8
V7X_BF16_SUBLANES = 16
V7X_VMEM_LIMIT_BYTES = 56 * 1024 * 1024

ROW_TILE = 512
IN_PROJ_ROW_TILE = 1024
FINISH_CHUNKS = 2
Q_TILE = 256
K_BLOCK = 1024
HEADS_PER_STEP = 4
TILES_PER_STEP = 4
CACHE_RING = 3
FAR_BLOCKS = 4
FAR_LAG = 2

F32 = jnp.float32
BF16 = jnp.bfloat16


def _lambda_init(layer):
    return 0.8 - 0.6 * math.exp(-0.3 * layer)


def _layer_norm(x, g, b):
    mu = jnp.mean(x, axis=-1, keepdims=True)
    xc = x - mu
    var = jnp.mean(xc * xc, axis=-1, keepdims=True)
    return xc * lax.rsqrt(var + LN_EPS) * g + b


def _resident(shape):
    nd = len(shape)
    return pl.BlockSpec(shape, lambda *_: (0,) * nd, pipeline_mode=pl.Buffered(1))


def _in_proj_kernel(x_ref, g_ref, b_ref, w_ref, cw_ref, hist_ref, *rest, seg_len, tm, qv_tile, n_cast):
    cast_in, rest = rest[:n_cast], rest[n_cast:]
    q_ref, k_ref, v_ref, kb_ref, vb_ref, conv_ref, utail_ref = rest[:7]
    cast_out, (carry_ref,) = rest[7:7 + n_cast], rest[7 + n_cast:]
    a = ATTN_WIDTH
    c = (w_ref.shape[1] - 3 * a) // 3
    for src, dst in zip(cast_in, cast_out):
        dst[...] = src[...].astype(BF16)
    xn = _layer_norm(x_ref[...], g_ref[...], b_ref[...])
    z = jnp.dot(xn, w_ref[...], preferred_element_type=F32)
    gb = z[:, 3 * a:3 * a + c]
    u = z[:, 3 * a + c:3 * a + 2 * c] * z[:, 3 * a + 2 * c:]
    q = z[:, :a] * Q_SCALE
    k = z[:, a:2 * a]
    v = z[:, 2 * a:3 * a]
    for h in range(N_HEADS):
        k_ref[pl.ds(h, tm, stride=N_HEADS), :] = k[:, h * HEAD_V:(h + 1) * HEAD_V]
        v_ref[pl.ds(h, tm, stride=N_HEADS), :] = v[:, h * HEAD_V:(h + 1) * HEAD_V]
    kb_ref[...] = k.astype(BF16)
    if qv_tile is None:
        q_ref[...] = q.astype(BF16)
        vb_ref[...] = v.astype(BF16)
    else:
        for s in range(tm // qv_tile):
            rows_s = slice(s * qv_tile, (s + 1) * qv_tile)
            q_ref[s] = q[rows_s, :].T.astype(BF16)
            vb_ref[s] = v[rows_s, :].T.astype(BF16)
    rows = lax.broadcasted_iota(jnp.int32, (tm, 1), 0)
    if seg_len >= tm:
        tiles_per_seg = seg_len // tm
        first = (pl.program_id(0) % tiles_per_seg) == 0
        hist = hist_ref[0]
        prev = carry_ref[...]
        p1 = jnp.where(first, hist[1:2, :], prev[V7X_SUBLANES - 1:, :])
        p0 = jnp.where(first, hist[0:1, :], prev[V7X_SUBLANES - 2:V7X_SUBLANES - 1, :])
        pos = rows
        tail = u[tm - V7X_SUBLANES:, :]
        carry_ref[...] = tail
        utail_ref[0] = tail
    else:
        nseg = tm // seg_len
        hist = hist_ref[...]
        p1 = jnp.broadcast_to(hist[:, 1:2, :], (nseg, seg_len, c)).reshape(tm, c)
        p0 = jnp.broadcast_to(hist[:, 0:1, :], (nseg, seg_len, c)).reshape(tm, c)
        pos = rows & (seg_len - 1)
        utail_ref[...] = u.reshape(nseg, seg_len, c)[:, seg_len - V7X_SUBLANES:, :]
    u1 = jnp.where(pos == 0, p1, pltpu.roll(u, 1, 0))
    u2 = jnp.where(pos == 0, p0, jnp.where(pos == 1, p1, pltpu.roll(u, 2, 0)))
    cw = cw_ref[...]
    conv = gb * (u2 * cw[0:1, :] + u1 * cw[1:2, :] + u * cw[2:3, :])
    conv_ref[...] = conv.astype(BF16)


def _in_proj(x2d, seg_len, ln_g, ln_b, w_in, conv_w, hist, qv_tile=None, cast=()):
    n, d = x2d.shape
    e = w_in.shape[1]
    a = ATTN_WIDTH
    c = (e - 3 * a) // 3
    tm = min(IN_PROJ_ROW_TILE, n)
    assert n % tm == 0 and (seg_len % tm == 0 or tm % seg_len == 0)
    assert seg_len & (seg_len - 1) == 0 and seg_len % V7X_SUBLANES == 0
    nseq = n // seg_len
    if seg_len >= tm:
        tiles_per_seg = seg_len // tm
        hist_spec = pl.BlockSpec((1, CONV_K - 1, c), lambda i: (i // tiles_per_seg, 0, 0))
        tail_spec = pl.BlockSpec((1, V7X_SUBLANES, c), lambda i: (i // tiles_per_seg, 0, 0))
    else:
        nseg = tm // seg_len
        hist_spec = pl.BlockSpec((nseg, CONV_K - 1, c), lambda i: (i, 0, 0))
        tail_spec = pl.BlockSpec((nseg, V7X_SUBLANES, c), lambda i: (i, 0, 0))
    row = lambda w: pl.BlockSpec((tm, w), lambda i: (i, 0))
    headrow = pl.BlockSpec((tm * N_HEADS, HEAD_V), lambda i: (i, 0))
    if qv_tile is None:
        qv_shape = jax.ShapeDtypeStruct((n, a), BF16)
        qv_spec = row(a)
    else:
        assert tm % qv_tile == 0
        qv_shape = jax.ShapeDtypeStruct((n // qv_tile, a, qv_tile), BF16)
        qv_spec = pl.BlockSpec((tm // qv_tile, a, qv_tile), lambda i: (i, 0, 0))
    out_shape = (
        qv_shape,
        jax.ShapeDtypeStruct((n * N_HEADS, HEAD_V), F32),
        jax.ShapeDtypeStruct((n * N_HEADS, HEAD_V), F32),
        jax.ShapeDtypeStruct((n, a), BF16),
        qv_shape,
        jax.ShapeDtypeStruct((n, c), BF16),
        jax.ShapeDtypeStruct((nseq, V7X_SUBLANES, c), F32),
    ) + tuple(jax.ShapeDtypeStruct(m.shape, BF16) for m in cast)
    steps = n // tm
    slab_specs = []
    for m in cast:
        assert m.shape[0] % (steps * V7X_BF16_SUBLANES) == 0
        slab_specs.append(pl.BlockSpec((m.shape[0] // steps, m.shape[1]), lambda i: (i, 0)))
    return pl.pallas_call(
        functools.partial(_in_proj_kernel, seg_len=seg_len, tm=tm, qv_tile=qv_tile, n_cast=len(cast)),
        out_shape=out_shape,
        grid=(steps,),
        in_specs=[row(d), _resident((1, d)), _resident((1, d)), _resident((d, e)),
                  _resident((CONV_K, c)), hist_spec, *slab_specs],
        out_specs=(qv_spec, headrow, headrow, row(a), qv_spec, row(c), tail_spec, *slab_specs),
        scratch_shapes=[pltpu.VMEM((V7X_SUBLANES, c), F32)],
        compiler_params=pltpu.CompilerParams(
            dimension_semantics=("arbitrary",), vmem_limit_bytes=V7X_VMEM_LIMIT_BYTES),
        name="in_proj",
    )(x2d, ln_g.reshape(1, d), ln_b.reshape(1, d), w_in, conv_w, hist, *cast)


def _stack_maps(q):
    lane = lax.broadcasted_iota(jnp.int32, q.shape, 1)
    zero = jnp.zeros_like(q)
    return jnp.concatenate([jnp.where(lane < HEAD_QK, q, zero), jnp.where(lane < HEAD_QK, zero, q)], axis=0)


def _scores(qz, kblk):
    return lax.dot_general(qz, kblk, (((1,), (1,)), ((), ())), preferred_element_type=F32)


def _add_to_both_maps(s, bias):
    t, n = bias.shape
    return (s.reshape(2, t, n) + bias[None]).reshape(2 * t, n)


def _lambda(lamv_ref, lam_init):
    lv = lamv_ref[...]
    s1 = jnp.sum(lv[0:1, :] * lv[1:2, :], axis=-1, keepdims=True)
    s2 = jnp.sum(lv[2:3, :] * lv[3:4, :], axis=-1, keepdims=True)
    return jnp.exp(s1) - jnp.exp(s2) + lam_init


def _diff_combine(acc, l, lam, g, lam_init, t):
    o = acc[:t] / l[:t] - lam * (acc[t:] / l[t:])
    ms = jnp.mean(o * o, axis=-1, keepdims=True)
    return o * lax.rsqrt(ms + SUBLN_EPS) * g * (1.0 - lam_init)


def _rel_bucket(rel):
    half = N_BUCKETS // 2
    max_exact = half // 2
    ret = jnp.where(rel > 0, half, 0)
    n = jnp.abs(rel)
    nf = jnp.maximum(n, 1).astype(F32)
    large = max_exact + (jnp.log(nf / max_exact) / math.log(MAX_DISTANCE / max_exact)
                         * (half - max_exact)).astype(jnp.int32)
    large = jnp.minimum(large, half - 1)
    return ret + jnp.where(n < max_exact, n, large)


def _bias_band(rel_bias, n_q, n_k, rel00):
    length = n_q + n_k
    rel = rel00 - (n_q - 1) + jnp.arange(length, dtype=jnp.int32)
    far = rel_bias[N_BUCKETS // 2 - 1].astype(F32)
    w = ((rel_bias[_rel_bucket(rel)].astype(F32) - far[None, :]) * LOG2_E).T
    skew = jnp.tile(w, (1, n_q))[:, :n_q * (length - 1)].reshape(-1, n_q, length - 1)
    return skew[:, :, n_q - 1:n_q - 1 + n_k]


def _chunk_mask(tile, q_pos, k_pos):
    visible = (k_pos[None, :] // CHUNK) <= (q_pos[:, None] // CHUNK)
    return jnp.where(visible[None], tile, -jnp.inf)


SUM_LO = 2.0 ** -60
SUM_HI = 2.0 ** 60


def _prompt_attn_kernel(lamv_ref, g_ref, q_ref, k_ref, v_ref, bd_ref, bp_ref, o_ref,
                        acc_ref, lp_ref, l_ref, m_ref, *, tq, tk, hp, tps, lam_init):
    def tile(t, carry):
        rows = pl.ds(pl.multiple_of(t * tq, tq), tq)
        _prompt_attn_tile(pl.program_id(2) * tps + t, q_ref.at[0, t], o_ref.at[0, rows], lamv_ref, g_ref,
                          k_ref, v_ref, bd_ref, bp_ref, acc_ref, lp_ref, l_ref, m_ref,
                          tq=tq, tk=tk, hp=hp, lam_init=lam_init)
        return carry

    lax.fori_loop(0, tps, tile, 0)


def _prompt_attn_tile(qt, q_ref, o_ref, lamv_ref, g_ref, k_ref, v_ref, bd_ref, bp_ref,
                      acc_ref, lp_ref, l_ref, m_ref, *, tq, tk, hp, lam_init):
    sub = tk // tq
    feats = [slice(h * HEAD_V, (h + 1) * HEAD_V) for h in range(hp)]

    def stack_maps(qT):
        row = lax.broadcasted_iota(jnp.int32, qT.shape, 0)
        zero = jnp.zeros_like(qT)
        return jnp.concatenate([jnp.where(row < HEAD_QK, qT, zero), jnp.where(row < HEAD_QK, zero, qT)], axis=1)

    qz = [stack_maps(q_ref[ft, :]) for ft in feats]

    def kv(j, h):
        start = pl.multiple_of(j * tq, tq)
        return k_ref[0, pl.ds(start, tq), feats[h]], v_ref[0, j, feats[h], :]

    def logits(h, kj, bias):
        s = jnp.dot(kj, qz[h], preferred_element_type=F32)
        if bias is not None:
            s = s + jnp.concatenate([bias, bias], axis=1)
        return s

    def sublane_fold(p):
        return jnp.sum(p.reshape(p.shape[0] // V7X_SUBLANES, V7X_SUBLANES, p.shape[1]), axis=0)

    def numerators(chains, lag, first=False):
        totals = {}
        pending = []

        def values(h, vj, p):
            pv = jnp.dot(vj, p.astype(BF16), preferred_element_type=F32)
            lp = sublane_fold(p)
            totals[h] = (pv, lp) if h not in totals else (totals[h][0] + pv, totals[h][1] + lp)

        for h, j, bias in chains:
            kj, vj = kv(j, h)
            pending.append((h, vj, jnp.exp2(logits(h, kj, bias))))
            if len(pending) > lag:
                values(*pending.pop(0))
        while pending:
            values(*pending.pop(0))
        for h, (pv, lp) in totals.items():
            if first:
                acc_ref[h] = pv
                lp_ref[h] = lp
            else:
                acc_ref[h] += pv
                lp_ref[h] += lp

    heads = range(hp)
    n_far = jnp.maximum(qt - 1, 0)
    n_lead = n_far % sub
    n_blocks = n_far // sub

    @pl.when(qt == 0)
    def _():
        numerators([(h, qt, bd_ref[h]) for h in heads], hp, first=True)

    for lead in range(sub):
        @pl.when((qt > 0) & (n_lead == lead))
        def _(lead=lead):
            near = ((0, bd_ref), (1, bp_ref))
            numerators([(h, qt - d, None if b_ref is None else b_ref[h]) for h in heads
                        for d, b_ref in near + tuple((2 + f, None) for f in range(lead))], hp, first=True)

    def far_region(first_block, blocks):
        numerators([(h, first_block * sub + c, None) for c in range(blocks * sub) for h in heads], FAR_LAG)

    def quad(jb, carry):
        far_region(jb * FAR_BLOCKS, FAR_BLOCKS)
        return carry

    lax.fori_loop(0, n_blocks // FAR_BLOCKS, quad, 0)

    for rem in range(1, FAR_BLOCKS):
        @pl.when(n_blocks % FAR_BLOCKS == rem)
        def _(rem=rem):
            far_region(n_blocks - rem, rem)

    lam = _lambda(lamv_ref, lam_init)

    def write_output():
        for h in range(hp):
            o = acc_ref[h] / l_ref[h]
            o = o[:, :tq] - lam * o[:, tq:]
            ms = jnp.mean(o * o, axis=0, keepdims=True)
            o = o * lax.rsqrt(ms + SUBLN_EPS) * g_ref[...] * (1.0 - lam_init)
            o_ref[:, feats[h]] = o.T.astype(o_ref.dtype)

    for h in range(hp):
        l_ref[h] = jnp.sum(lp_ref[h], axis=0, keepdims=True)
    l_all = l_ref[...]
    in_range = (l_all >= SUM_LO) & (l_all <= SUM_HI)
    write_output()

    @pl.when(jnp.min(jnp.where(in_range, 1.0, 0.0)) < 0.5)
    def _():
        for h in range(hp):
            kd, vd = kv(qt, h)
            s = logits(h, kd, bd_ref[h])
            m0 = jnp.max(s, axis=0, keepdims=True)
            p = jnp.exp2(s - m0)
            m_ref[h] = m0
            l_ref[h] = jnp.sum(p, axis=0, keepdims=True)
            acc_ref[h] = jnp.dot(vd, p.astype(BF16), preferred_element_type=F32)

            def step(j, carry, h=h):
                kj, vj = kv(j, h)
                s = logits(h, kj, jnp.where(j == qt - 1, bp_ref[h], jnp.zeros_like(bp_ref[h])))
                m_prev = m_ref[h]
                m_new = jnp.maximum(m_prev, jnp.max(s, axis=0, keepdims=True))
                alpha = jnp.exp2(m_prev - m_new)
                p = jnp.exp2(s - m_new)
                l_ref[h] = alpha * l_ref[h] + jnp.sum(p, axis=0, keepdims=True)
                acc_ref[h] = alpha * acc_ref[h] + jnp.dot(vj, p.astype(BF16), preferred_element_type=F32)
                m_ref[h] = m_new
                return carry

            lax.fori_loop(0, qt, step, 0)
        write_output()


def _prompt_attention(qT, kb, vT, rel_bias, lamv, subln_g, lam_init):
    b, t, _ = kb.shape
    tq = qT.shape[-1]
    tk = min(K_BLOCK, t)
    sub = tk // tq
    assert t % tq == 0 and tk % tq == 0 and tq % CHUNK == 0 and tq >= MAX_DISTANCE
    pos = jnp.arange(tq, dtype=jnp.int32)
    band = _bias_band(rel_bias, tq, 2 * tq, -tq)
    prev = jnp.swapaxes(band[:, :, :tq], 1, 2)
    diag = jnp.swapaxes(_chunk_mask(band[:, :, tq:], pos, pos), 1, 2)
    gain = jnp.broadcast_to(subln_g.astype(F32)[:, None], (HEAD_V, tq))
    hp = HEADS_PER_STEP
    tps = min(TILES_PER_STEP, t // tq)
    assert N_HEADS % hp == 0 and (t // tq) % tps == 0
    qspec = pl.BlockSpec((1, tps, hp * HEAD_V, tq), lambda bi, g, i: (bi, i, g, 0))
    kspec = pl.BlockSpec((1, t, hp * HEAD_V), lambda bi, g, i: (bi, 0, g))
    vspec = pl.BlockSpec((1, t // tq, hp * HEAD_V, tq), lambda bi, g, i: (bi, 0, g, 0))
    bspec = pl.BlockSpec((hp, tq, tq), lambda bi, g, i: (g, 0, 0))
    return pl.pallas_call(
        functools.partial(_prompt_attn_kernel, tq=tq, tk=tk, hp=hp, tps=tps, lam_init=lam_init),
        out_shape=jax.ShapeDtypeStruct(kb.shape, BF16),
        grid=(b, N_HEADS // hp, t // (tq * tps)),
        in_specs=[_resident(lamv.shape), _resident((HEAD_V, tq)), qspec, kspec, vspec, bspec, bspec],
        out_specs=pl.BlockSpec((1, tq * tps, hp * HEAD_V), lambda bi, g, i: (bi, i, g)),
        scratch_shapes=[pltpu.VMEM((hp, HEAD_V, 2 * tq), F32), pltpu.VMEM((hp, V7X_SUBLANES, 2 * tq), F32),
                        pltpu.VMEM((hp, 1, 2 * tq), F32), pltpu.VMEM((hp, 1, 2 * tq), F32)],
        compiler_params=pltpu.CompilerParams(
            dimension_semantics=("arbitrary", "arbitrary", "arbitrary"),
            vmem_limit_bytes=V7X_VMEM_LIMIT_BYTES),
        name="prompt_attention",
    )(lamv, gain, qT, kb, vT, diag, prev)


def _sample_attn_kernel(lamv_ref, g_ref, q_ref, ck_hbm, cv_hbm, kn_ref, vn_ref, bias_ref, o_ref,
                        ck_buf, cv_buf, sem, *, ts, past, near, lam_init, n_streams):
    step = pl.program_id(0)

    def cache_copies(stream, slot):
        return (pltpu.make_async_copy(ck_hbm.at[stream], ck_buf.at[slot], sem.at[0, slot]),
                pltpu.make_async_copy(cv_hbm.at[stream], cv_buf.at[slot], sem.at[1, slot]))

    @pl.when(step == 0)
    def _():
        for ahead in range(min(CACHE_RING - 1, n_streams)):
            for copy in cache_copies(ahead, ahead):
                copy.start()

    ahead = step + CACHE_RING - 1

    @pl.when(ahead < n_streams)
    def _():
        for copy in cache_copies(ahead, ahead % CACHE_RING):
            copy.start()

    slot = step % CACHE_RING
    for copy in cache_copies(step, slot):
        copy.wait()
    ck_ref = ck_buf.at[slot]
    cv_ref = cv_buf.at[slot]
    far = past - near
    lam = _lambda(lamv_ref, lam_init)
    lanes = [slice(h * HEAD_V, (h + 1) * HEAD_V) for h in range(N_HEADS)]

    def logits(h):
        qz = _stack_maps(q_ref[0, :, lanes[h]])
        ck = ck_ref[pl.ds(h, past, stride=N_HEADS), :].astype(BF16)
        bias = bias_ref[h]
        return (_scores(qz, ck[:far]),
                _add_to_both_maps(_scores(qz, ck[far:]), bias[:, :near]),
                _add_to_both_maps(_scores(qz, kn_ref[0, :, lanes[h]]), bias[:, near:]))

    def attend(h, s_far, s_near, s_new):
        cv = cv_ref[pl.ds(h, past, stride=N_HEADS), :].astype(BF16)
        m = jnp.maximum(jnp.max(s_far, axis=-1, keepdims=True),
                        jnp.maximum(jnp.max(s_near, axis=-1, keepdims=True),
                                    jnp.max(s_new, axis=-1, keepdims=True)))
        p_far = jnp.exp2(s_far - m)
        p_near = jnp.exp2(s_near - m)
        p_new = jnp.exp2(s_new - m)
        l = (jnp.sum(p_far, axis=-1, keepdims=True) + jnp.sum(p_near, axis=-1, keepdims=True)
             + jnp.sum(p_new, axis=-1, keepdims=True))
        acc = (jnp.dot(p_far.astype(BF16), cv[:far], preferred_element_type=F32)
               + jnp.dot(p_near.astype(BF16), cv[far:], preferred_element_type=F32)
               + jnp.dot(p_new.astype(BF16), vn_ref[0, :, lanes[h]], preferred_element_type=F32))
        o_ref[0, :, lanes[h]] = _diff_combine(acc, l, lam, g_ref[...], lam_init, ts).astype(o_ref.dtype)

    s = logits(0)
    for h in range(1, N_HEADS):
        s_next = logits(h)
        attend(h - 1, *s)
        s = s_next
    attend(N_HEADS - 1, *s)


def _sample_attention(q, kb, vb, cache_k, cache_v, rel_bias, lamv, subln_g, lam_init):
    s, ts, _ = q.shape
    past = cache_k.shape[1] // N_HEADS
    near = MAX_DISTANCE
    assert past >= near and past % V7X_SUBLANES == 0
    q_pos = past + jnp.arange(ts, dtype=jnp.int32)
    k_pos = (past - near) + jnp.arange(near + ts, dtype=jnp.int32)
    bias = _chunk_mask(_bias_band(rel_bias, ts, near + ts, -near), q_pos, k_pos)
    new = pl.BlockSpec((1, ts, ATTN_WIDTH), lambda si: (si, 0, 0))
    cache = pl.BlockSpec(memory_space=pl.ANY)
    return pl.pallas_call(
        functools.partial(_sample_attn_kernel, ts=ts, past=past, near=near, lam_init=lam_init, n_streams=s),
        out_shape=jax.ShapeDtypeStruct(q.shape, BF16),
        grid=(s,),
        in_specs=[_resident(lamv.shape), _resident((1, HEAD_V)), new, cache, cache, new, new,
                  _resident(bias.shape)],
        out_specs=new,
        scratch_shapes=[pltpu.VMEM((CACHE_RING, past * N_HEADS, HEAD_V), F32),
                        pltpu.VMEM((CACHE_RING, past * N_HEADS, HEAD_V), F32),
                        pltpu.SemaphoreType.DMA((2, CACHE_RING))],
        compiler_params=pltpu.CompilerParams(
            dimension_semantics=("arbitrary",), vmem_limit_bytes=V7X_VMEM_LIMIT_BYTES),
        name="sample_attention",
    )(lamv, subln_g.reshape(1, HEAD_V), q, cache_k, cache_v, kb, vb, bias)


def _finish_kernel(x_ref, a_ref, c_ref, g0_ref, b0_ref, wo_ref, g1_ref, b1_ref, w1_ref, w2_ref,
                   g2_ref, b2_ref, o_ref, *, alpha, chunks):
    rows = x_ref.shape[0] // chunks
    rs = [pl.ds(c * rows, rows) for c in range(chunks)]
    half = w1_ref.shape[1] // 2
    xn, y, x1, h, f = [], [], [], [], []
    for c in range(chunks):
        xn.append(_layer_norm(x_ref[rs[c], :], g0_ref[...], b0_ref[...]))
        mix = jnp.concatenate([a_ref[rs[c], :], c_ref[rs[c], :]], axis=-1)
        y.append(jnp.dot(mix, wo_ref[...], preferred_element_type=F32))
    for c in range(chunks):
        x1.append(_layer_norm(alpha * xn[c] + y[c], g1_ref[...], b1_ref[...]))
        x1b = x1[c].astype(BF16)
        h.append([jnp.dot(x1b, w1_ref[:, lo:lo + half], preferred_element_type=F32) for lo in (0, half)])
    for c in range(chunks):
        part = None
        for lo, hh in zip((0, half), h[c]):
            hc = jnp.square(jnp.maximum(hh, 0.0)).astype(BF16)
            prod = jnp.dot(hc, w2_ref[lo:lo + half, :], preferred_element_type=F32)
            part = prod if part is None else part + prod
        f.append(part)
    for c in range(chunks):
        o_ref[rs[c], :] = _layer_norm(alpha * x1[c] + f[c], g2_ref[...], b2_ref[...])


def _finish(x2d, attn, conv, ln0_g, ln0_b, w_out_bf, ln1_g, ln1_b, w1_bf, w2_bf, ln2_g, ln2_b, alpha):
    n, d = x2d.shape
    a = attn.shape[1]
    c = conv.shape[1]
    f = w1_bf.shape[1]
    tm = min(ROW_TILE, n)
    assert n % tm == 0
    row = lambda w: pl.BlockSpec((tm, w), lambda i: (i, 0))
    vec = lambda p: p.reshape(1, d)
    return pl.pallas_call(
        functools.partial(_finish_kernel, alpha=alpha, chunks=FINISH_CHUNKS),
        out_shape=jax.ShapeDtypeStruct((n, d), F32),
        grid=(n // tm,),
        in_specs=[row(d), row(a), row(c), _resident((1, d)), _resident((1, d)), _resident((a + c, d)),
                  _resident((1, d)), _resident((1, d)), _resident((d, f)), _resident((f, d)),
                  _resident((1, d)), _resident((1, d))],
        out_specs=row(d),
        compiler_params=pltpu.CompilerParams(
            dimension_semantics=("arbitrary",), vmem_limit_bytes=V7X_VMEM_LIMIT_BYTES),
        name="finish",
    )(x2d, attn, conv, vec(ln0_g), vec(ln0_b), w_out_bf, vec(ln1_g), vec(ln1_b), w1_bf, w2_bf,
      vec(ln2_g), vec(ln2_b))


def kernel(x_prompt, x_sample, cache_k, cache_v, cache_conv, ln0_g, ln0_b, rel_bias, w_in, conv_w,
           lambda_q1, lambda_k1, lambda_q2, lambda_k2, subln_g, w_out, ln1_g, ln1_b,
           w_ff1, w_ff2, ln2_g, ln2_b):
    depth = w_in.shape[0]
    assert depth == 1, "single-layer step"
    layer = 0
    b, t, d = x_prompt.shape
    s, ts, _ = x_sample.shape
    past = cache_k.shape[2]
    c = conv_w.shape[-1]
    alpha = (2.0 * depth) ** 0.25
    lam_init = _lambda_init(layer)

    w_in_f = w_in[layer].astype(F32)
    lamv = jnp.stack([lambda_q1[layer], lambda_k1[layer], lambda_q2[layer], lambda_k2[layer]]).astype(F32)

    xp2 = x_prompt.reshape(b * t, d)
    hist0 = jnp.zeros((b, CONV_K - 1, c), F32)
    tq = min(Q_TILE, t)
    qp, kp, vp, kpb, vpb, convp, tailp, w_out_bf, w1_bf, w2_bf = _in_proj(
        xp2, t, ln0_g, ln0_b, w_in_f, conv_w[layer], hist0, qv_tile=tq,
        cast=(w_out[layer].astype(F32), w_ff1[layer].astype(F32), w_ff2[layer].astype(F32)))
    ffn = (ln0_g, ln0_b, w_out_bf, ln1_g[layer], ln1_b[layer], w1_bf, w2_bf, ln2_g[layer], ln2_b[layer], alpha)
    tiles = (b, t // tq, ATTN_WIDTH, tq)
    attnp = _prompt_attention(qp.reshape(tiles), kpb.reshape(b, t, ATTN_WIDTH), vpb.reshape(tiles), rel_bias,
                              lamv, subln_g[layer], lam_init)
    yp = _finish(xp2, attnp.reshape(b * t, ATTN_WIDTH), convp, *ffn).reshape(b, t, d)

    xs2 = x_sample.reshape(s * ts, d)
    qs, ks, vs, ksb, vsb, convs, tails = _in_proj(xs2, ts, ln0_g, ln0_b, w_in_f, conv_w[layer],
                                                  cache_conv[layer].astype(F32))
    shs = (s, ts, ATTN_WIDTH)
    attns = _sample_attention(qs.reshape(shs), ksb.reshape(shs), vsb.reshape(shs),
                              cache_k[layer].astype(F32).reshape(s, past * N_HEADS, HEAD_V),
                              cache_v[layer].astype(F32).reshape(s, past * N_HEADS, HEAD_V),
                              rel_bias, lamv, subln_g[layer], lam_init)
    ys = _finish(xs2, attns.reshape(s * ts, ATTN_WIDTH), convs, *ffn).reshape(s, ts, d)

    keep = CONV_K - 1
    return (yp, ys,
            kp.reshape(1, b, t, N_HEADS, HEAD_V), vp.reshape(1, b, t, N_HEADS, HEAD_V),
            tailp[:, V7X_SUBLANES - keep:, :].reshape(1, b, keep, c),
            ks.reshape(1, s, ts, N_HEADS, HEAD_V), vs.reshape(1, s, ts, N_HEADS, HEAD_V),
            tails[:, V7X_SUBLANES - keep:, :].reshape(1, s, keep, c))
```
